```python
import math
import jax, jax.numpy as jnp
from jax import lax
import numpy as np

D_MODEL = 1024
BATCH = 2
SEQ = 8192
DEPTH = 1

CHUNK = 64
Q_BLOCK = 128
MEM_LEN = 256

DIFF_HEADS = 4
DIFF_QK_DIM = 64
DIFF_V_DIM = 2 * DIFF_QK_DIM
RET_HEADS = 4
RET_QK_DIM = 64
RET_V_DIM = 128
MEM_HEADS = 4
MEM_HEAD_DIM = 128

N_BRANCHES = 3
FFN_DIM = 2816
CONV_WIDTH = 3
LN_EPS = 1e-5
DEEPNORM_ALPHA = (2.0 * DEPTH) ** 0.25
DEEPNORM_BETA = (8.0 * DEPTH) ** -0.25

DIFF_QK_W = DIFF_HEADS * 2 * DIFF_QK_DIM
DIFF_V_W = DIFF_HEADS * DIFF_V_DIM
RET_QK_W = RET_HEADS * RET_QK_DIM
RET_V_W = RET_HEADS * RET_V_DIM
MEM_W = MEM_HEADS * MEM_HEAD_DIM
IN_WIDTHS = (DIFF_QK_W, DIFF_QK_W, DIFF_V_W, RET_QK_W, RET_QK_W, RET_V_W, RET_V_W, MEM_W)
IN_DIM = sum(IN_WIDTHS)
IN_OFFSETS = tuple(int(v) for v in np.cumsum(IN_WIDTHS)[:-1])

kernel_name = "hybrid_diffattn_retention_memxattn_convffn_deepnorm"


def layer_norm(x, g, b):
    xf = x.astype(jnp.float32)
    mu = jnp.mean(xf, axis=-1, keepdims=True)
    var = jnp.mean(jnp.square(xf - mu), axis=-1, keepdims=True)
    return ((xf - mu) * lax.rsqrt(var + LN_EPS) * g + b).astype(x.dtype)


def head_rmsnorm(y, g):
    B, S, H, E = y.shape
    yf = y.astype(jnp.float32)
    yf = yf * lax.rsqrt(jnp.mean(jnp.square(yf), axis=-1, keepdims=True) + LN_EPS)
    return (yf * g.reshape(H, E)).astype(y.dtype).reshape(B, S, H * E)


def head_groupnorm(y, g):
    B, S, H, E = y.shape
    yf = y.astype(jnp.float32)
    mu = jnp.mean(yf, axis=-1, keepdims=True)
    var = jnp.mean(jnp.square(yf - mu), axis=-1, keepdims=True)
    return ((yf - mu) * lax.rsqrt(var + LN_EPS) * g.reshape(H, E)).astype(y.dtype).reshape(B, S, H * E)


def alibi_slopes(n_heads):
    start = 2.0 ** (-8.0 / n_heads)
    return jnp.asarray([start ** (i + 1) for i in range(n_heads)], dtype=jnp.float32)


def diff_attention(q, k, v, lam):
    B, S, H, _, d = q.shape
    n_qb = S // Q_BLOCK
    scale = 1.0 / math.sqrt(d)
    slopes = alibi_slopes(H)
    pos = jnp.arange(S, dtype=jnp.int32)
    key_chunk = pos // CHUNK
    q_blocks = q.reshape(B, n_qb, Q_BLOCK, H, 2, d).transpose(1, 0, 2, 3, 4, 5)
    q_pos = pos.reshape(n_qb, Q_BLOCK)

    def one_block(args):
        qb, qp = args
        s = jnp.einsum("bqhid,bkhid->bhiqk", qb, k).astype(jnp.float32) * scale
        dist = jnp.abs(qp[:, None] - pos[None, :]).astype(jnp.float32)
        bias = -slopes[:, None, None] * dist
        allowed = key_chunk[None, :] <= (qp // CHUNK)[:, None]
        s = jnp.where(allowed, s + bias[None, :, None], -jnp.inf)
        p = jax.nn.softmax(s, axis=-1)
        a = p[:, :, 0] - lam * p[:, :, 1]
        return jnp.einsum("bhqk,bkhe->bqhe", a.astype(v.dtype), v)

    out = lax.map(one_block, (q_blocks, q_pos))
    return out.transpose(1, 0, 2, 3, 4).reshape(B, S, H, v.shape[-1])


def retention(q, k, v):
    B, S, H, dk = q.shape
    dv = v.shape[-1]
    nc = S // CHUNK
    lg = jnp.log(1.0 - 2.0 ** (-5.0 - jnp.arange(H, dtype=jnp.float32)))
    q = q.reshape(B, nc, CHUNK, H, dk)
    k = k.reshape(B, nc, CHUNK, H, dk) * (dk ** -0.5)
    v = v.reshape(B, nc, CHUNK, H, dv)
    idx = jnp.arange(CHUNK, dtype=jnp.float32)
    n_minus_m = idx[:, None] - idx[None, :]
    intra_decay = jnp.where(n_minus_m >= 0,
                            jnp.exp(lg[:, None, None] * jnp.maximum(n_minus_m, 0.0)), 0.0)
    s = jnp.einsum("bcnhd,bcmhd->bchnm", q, k) * intra_decay
    intra = jnp.einsum("bchnm,bcmhe->bcnhe", s, v)
    k_decay = jnp.exp(lg[None, :] * (CHUNK - 1.0 - idx)[:, None])
    kv = jnp.einsum("bcmhd,bcmhe->cbhde", k * k_decay[:, :, None], v)
    chunk_decay = jnp.exp(lg * CHUNK)[None, :, None, None]

    def step(state, kv_c):
        return state * chunk_decay + kv_c, state

    _, state_prev = lax.scan(step, jnp.zeros(kv.shape[1:], kv.dtype), kv)
    q_decay = jnp.exp(lg[None, :] * (idx + 1.0)[:, None])
    cross = jnp.einsum("bcnhd,cbhde->bcnhe", q * q_decay[:, :, None], state_prev)
    return (intra + cross).reshape(B, S, H, dv)


def memory_attention(q, k, v):
    s = jnp.einsum("bshe,bmhe->bhsm", q, k).astype(jnp.float32) * (q.shape[-1] ** -0.5)
    p = jax.nn.softmax(s, axis=-1)
    return jnp.einsum("bhsm,bmhe->bshe", p.astype(v.dtype), v)


def token_mixers(u, mem, w_in, lam_p, subln_g, ret_g, w_mem_kv, w_diff_o, w_ret_o, w_mem_o,
                 w_gate, b_gate, w_out, lambda_init):
    B, S, _ = u.shape
    proj = u @ w_in
    dq, dk, dv, rq, rk, rv, rg, mq = jnp.split(proj, IN_OFFSETS, axis=-1)

    lpf = lam_p.astype(jnp.float32)
    lam = jnp.exp(jnp.sum(lpf[0] * lpf[1])) - jnp.exp(jnp.sum(lpf[2] * lpf[3])) + lambda_init
    a = diff_attention(dq.reshape(B, S, DIFF_HEADS, 2, DIFF_QK_DIM),
                       dk.reshape(B, S, DIFF_HEADS, 2, DIFF_QK_DIM),
                       dv.reshape(B, S, DIFF_HEADS, DIFF_V_DIM), lam)
    a = head_rmsnorm(a, subln_g) * (1.0 - lambda_init)

    r = retention(rq.reshape(B, S, RET_HEADS, RET_QK_DIM),
                  rk.reshape(B, S, RET_HEADS, RET_QK_DIM),
                  rv.reshape(B, S, RET_HEADS, RET_V_DIM))
    r = head_groupnorm(r, ret_g) * jax.nn.silu(rg)

    mk, mv = jnp.split(mem @ w_mem_kv, 2, axis=-1)
    M = mem.shape[1]
    m = memory_attention(mq.reshape(B, S, MEM_HEADS, MEM_HEAD_DIM),
                         mk.reshape(B, M, MEM_HEADS, MEM_HEAD_DIM),
                         mv.reshape(B, M, MEM_HEADS, MEM_HEAD_DIM)).reshape(B, S, MEM_W)

    gates = jax.nn.sigmoid(u @ w_gate + b_gate).reshape(B, S, N_BRANCHES, D_MODEL)
    merged = (gates[:, :, 0] * (a @ w_diff_o)
              + gates[:, :, 1] * (r @ w_ret_o)
              + gates[:, :, 2] * (m @ w_mem_o))
    return merged @ w_out


def conv_ffn(u, w_up, conv_w, conv_b, w_down):
    h = u @ w_up
    C = h.shape[-1]
    h = lax.conv_general_dilated(h, conv_w.reshape(CONV_WIDTH, 1, C).astype(h.dtype),
                                 window_strides=(1,), padding=((CONV_WIDTH - 1, 0),),
                                 dimension_numbers=("NWC", "WIO", "NWC"),
                                 feature_group_count=C) + conv_b
    g, val = jnp.split(h, 2, axis=-1)
    return (jax.nn.silu(g) * val) @ w_down


def setup_inputs(seed: int = 0) -> dict:
    key = jax.random.key(seed)
    ks = jax.random.split(key, 21)
    f32 = jnp.float32
    L, D, F2 = DEPTH, D_MODEL, 2 * FFN_DIM

    def nrm(k, shape, scale):
        return jax.random.normal(k, shape, f32) * scale

    return {
        "x": nrm(ks[0], (BATCH, SEQ, D), 1.0),
        "mem": nrm(ks[1], (BATCH, MEM_LEN, D), 1.0),
        "w_in": nrm(ks[2], (L, D, IN_DIM), D ** -0.5),
        "diff_lambda": nrm(ks[3], (L, 4, DIFF_QK_DIM), 0.1),
        "diff_subln_g": 1.0 + nrm(ks[4], (L, DIFF_V_W), 0.02),
        "ret_norm_g": 1.0 + nrm(ks[5], (L, RET_V_W), 0.02),
        "w_mem_kv": nrm(ks[6], (L, D, 2 * MEM_W), D ** -0.5),
        "w_diff_o": nrm(ks[7], (L, DIFF_V_W, D), DIFF_V_W ** -0.5 * DEEPNORM_BETA),
        "w_ret_o": nrm(ks[8], (L, RET_V_W, D), RET_V_W ** -0.5 * DEEPNORM_BETA),
        "w_mem_o": nrm(ks[9], (L, MEM_W, D), MEM_W ** -0.5 * DEEPNORM_BETA),
        "w_gate": nrm(ks[10], (L, D, N_BRANCHES * D), D ** -0.5),
        "b_gate": nrm(ks[11], (L, N_BRANCHES * D), 0.01),
        "w_mix_out": nrm(ks[12], (L, D, D), D ** -0.5 * DEEPNORM_BETA),
        "ln1_g": 1.0 + nrm(ks[13], (L, D), 0.02),
        "ln1_b": nrm(ks[14], (L, D), 0.02),
        "w_up": nrm(ks[15], (L, D, F2), D ** -0.5 * DEEPNORM_BETA),
        "conv_w": nrm(ks[16], (L, CONV_WIDTH, F2), CONV_WIDTH ** -0.5),
        "conv_b": nrm(ks[17], (L, F2), 0.01),
        "w_down": nrm(ks[18], (L, FFN_DIM, D), FFN_DIM ** -0.5 * DEEPNORM_BETA),
        "ln2_g": 1.0 + nrm(ks[19], (L, D), 0.02),
        "ln2_b": nrm(ks[20], (L, D), 0.02),
    }


def reference(x, mem, w_in, diff_lambda, diff_subln_g, ret_norm_g, w_mem_kv, w_diff_o, w_ret_o,
              w_mem_o, w_gate, b_gate, w_mix_out, ln1_g, ln1_b, w_up, conv_w, conv_b, w_down,
              ln2_g, ln2_b):
    h = x
    for l in range(DEPTH):
        lambda_init = 0.8 - 0.6 * math.exp(-0.3 * l)
        mix = token_mixers(h, mem, w_in[l], diff_lambda[l], diff_subln_g[l], ret_norm_g[l],
                           w_mem_kv[l], w_diff_o[l], w_ret_o[l], w_mem_o[l], w_gate[l], b_gate[l],
                           w_mix_out[l], lambda_init)
        h = layer_norm(DEEPNORM_ALPHA * h + mix, ln1_g[l], ln1_b[l])
        ffn = conv_ffn(h, w_up[l], conv_w[l], conv_b[l], w_down[l])
        h = layer_norm(DEEPNORM_ALPHA * h + ffn, ln2_g[l], ln2_b[l])
    return h
```

```python
import functools
import math

import numpy as np
import jax
import jax.numpy as jnp
from jax import lax
from jax.experimental import pallas as pl
from jax.experimental.pallas import tpu as pltpu

F32 = jnp.float32
BF16 = jnp.bfloat16

CHUNK = 64
DIFF_HEADS = 4
DIFF_QK_DIM = 64
DIFF_V_DIM = 128
RET_HEADS = 4
RET_QK_DIM = 64
RET_V_DIM = 128
MEM_HEADS = 4
MEM_HEAD_DIM = 128
N_BRANCHES = 3
CONV_WIDTH = 3
LN_EPS = 1e-5
DEPTH = 1
DEEPNORM_ALPHA = (2.0 * DEPTH) ** 0.25
LAMBDA_INIT = 0.8 - 0.6 * math.exp(-0.3 * 0)

OFF_DQ, OFF_DK, OFF_DV = 0, 512, 1024
OFF_RQ, OFF_RK, OFF_RV, OFF_RG, OFF_MQ = 1536, 1792, 2048, 2560, 3072
IN_DIM = 3584

LANES = 128
SUBLANES = 8
VMEM_LIMIT = 56 * 1024 * 1024

PROJ_TM = 512
PROJ_NC = 512
ATT_T = 256
RET_C = 256
MERGE_TM = 256
MERGE_NC = 512
FFN_TM = 512
FFN_FC = 256


def _cparams(sem):
    return pltpu.CompilerParams(dimension_semantics=sem, vmem_limit_bytes=VMEM_LIMIT)


def _resident(shape):
    nd = len(shape)
    return pl.BlockSpec(shape, lambda *_: (0,) * nd, pipeline_mode=pl.Buffered(1))


def _layer_norm(y, g, b):
    mu = jnp.mean(y, axis=-1, keepdims=True)
    d = y - mu
    var = jnp.mean(d * d, axis=-1, keepdims=True)
    return d * lax.rsqrt(var + LN_EPS) * g + b


def _proj_kernel(x_ref, w_ref, o_ref, *, nc):
    xb = x_ref[...].astype(BF16)
    n = w_ref.shape[1]
    for c in range(0, n, nc):
        o_ref[:, c:c + nc] = jnp.dot(xb, w_ref[:, c:c + nc],
                                     preferred_element_type=F32).astype(o_ref.dtype)


def _proj(x2, w, tm, nc):
    t, d = x2.shape
    n = w.shape[1]
    return pl.pallas_call(
        functools.partial(_proj_kernel, nc=nc),
        grid=(t // tm,),
        in_specs=[pl.BlockSpec((tm, d), lambda i: (i, 0)), _resident((d, n))],
        out_specs=pl.BlockSpec((tm, n), lambda i: (i, 0)),
        out_shape=jax.ShapeDtypeStruct((t, n), BF16),
        compiler_params=_cparams(("arbitrary",)),
        name="proj",
    )(x2, w)


def _attn_kernel(slope_ref, q_ref, k_ref, v_ref, boff_ref, bdiag_ref, lam_ref, g_ref, o_ref,
                 qs_scr, m_scr, l_scr, acc_scr, *, t):
    h = pl.program_id(1)
    qi = pl.program_id(2)
    slope = slope_ref[h]

    q = q_ref[...] * jnp.asarray(DIFF_QK_DIM ** -0.5, BF16)
    lane = lax.broadcasted_iota(jnp.int32, q.shape, 1)
    zero = jnp.zeros_like(q)
    qs_scr[0:t, :] = jnp.where(lane < DIFF_QK_DIM, q, zero)
    qs_scr[t:2 * t, :] = jnp.where(lane >= DIFF_QK_DIM, q, zero)

    m_scr[...] = jnp.full(m_scr.shape, -jnp.inf, F32)
    l_scr[...] = jnp.zeros(l_scr.shape, F32)
    acc_scr[...] = jnp.zeros(acc_scr.shape, F32)

    def step(kb, bias, c):
        start = pl.multiple_of(kb * t, t)
        kblk = k_ref[pl.ds(start, t), :]
        vblk = v_ref[pl.ds(start, t), :]
        s = lax.dot_general(qs_scr[...], kblk, (((1,), (1,)), ((), ())),
                            preferred_element_type=F32)
        z = (s.reshape(2, t, t) + bias[None]).reshape(2 * t, t)
        m_prev = m_scr[...]
        m_new = jnp.maximum(m_prev, jnp.max(z, axis=-1, keepdims=True) + c)
        p = jnp.exp(z + (c - m_new))
        alpha = jnp.exp(m_prev - m_new)
        l_scr[...] = alpha * l_scr[...] + jnp.sum(p, axis=-1, keepdims=True)
        acc_scr[...] = alpha * acc_scr[...] + jnp.dot(p.astype(BF16), vblk,
                                                      preferred_element_type=F32)
        m_scr[...] = m_new

    def body(kb, carry):
        c = -slope * ((qi - kb) * t).astype(F32)
        step(kb, boff_ref[0], c)
        return carry

    lax.fori_loop(0, qi, body, 0)
    step(qi, bdiag_ref[0], jnp.asarray(0.0, F32))

    lp = lam_ref[...]
    lam = (jnp.exp(jnp.sum(lp[0:1] * lp[1:2], axis=-1, keepdims=True))
           - jnp.exp(jnp.sum(lp[2:3] * lp[3:4], axis=-1, keepdims=True)) + LAMBDA_INIT)
    o = acc_scr[...] / l_scr[...]
    y = o[0:t] - lam * o[t:2 * t]
    y = y * lax.rsqrt(jnp.mean(y * y, axis=-1, keepdims=True) + LN_EPS)
    o_ref[...] = (y * g_ref[...] * (1.0 - LAMBDA_INIT)).astype(o_ref.dtype)


def _alibi_tables(t):
    slopes = np.array([(2.0 ** (-8.0 / DIFF_HEADS)) ** (i + 1) for i in range(DIFF_HEADS)], np.float64)
    i = np.arange(t)[:, None]
    j = np.arange(t)[None, :]
    off = -slopes[:, None, None] * (i - j)[None]
    diag = -slopes[:, None, None] * np.abs(i - j)[None]
    allowed = (j // CHUNK) <= (i // CHUNK)
    diag = np.where(allowed[None], diag, -np.inf)
    return (jnp.asarray(slopes, F32), jnp.asarray(off, F32), jnp.asarray(diag, F32))


def _diff_attention(proj, lam_p, subln_g, batch, seq):
    t = ATT_T
    nq = seq // t
    slopes, boff, bdiag = _alibi_tables(t)
    kcol = OFF_DK // LANES
    vcol = OFF_DV // LANES
    return pl.pallas_call(
        functools.partial(_attn_kernel, t=t),
        grid=(batch, DIFF_HEADS, nq),
        in_specs=[
            pl.BlockSpec(memory_space=pltpu.SMEM),
            pl.BlockSpec((t, LANES), lambda b, h, qi: (b * nq + qi, h)),
            pl.BlockSpec((seq, LANES), lambda b, h, qi: (b, kcol + h)),
            pl.BlockSpec((seq, LANES), lambda b, h, qi: (b, vcol + h)),
            pl.BlockSpec((1, t, t), lambda b, h, qi: (h, 0, 0)),
            pl.BlockSpec((1, t, t), lambda b, h, qi: (h, 0, 0)),
            pl.BlockSpec(lam_p.shape, lambda b, h, qi: (0, 0)),
            pl.BlockSpec((1, LANES), lambda b, h, qi: (0, h)),
        ],
        out_specs=pl.BlockSpec((t, LANES), lambda b, h, qi: (b * nq + qi, h)),
        out_shape=jax.ShapeDtypeStruct((batch * seq, DIFF_HEADS * DIFF_V_DIM), BF16),
        scratch_shapes=[
            pltpu.VMEM((2 * t, LANES), BF16),
            pltpu.VMEM((2 * t, 1), F32),
            pltpu.VMEM((2 * t, 1), F32),
            pltpu.VMEM((2 * t, DIFF_V_DIM), F32),
        ],
        compiler_params=_cparams(("arbitrary", "arbitrary", "arbitrary")),
        name="diff_attn",
    )(slopes, proj, proj, proj, boff, bdiag, lam_p, subln_g)


def _ret_kernel(q_ref, k_ref, v_ref, g_ref, dtab_ref, qdec_ref, kdec_ref, cdec_ref, gn_ref, o_ref,
                state_scr, *, c):
    @pl.when(pl.program_id(1) == 0)
    def _():
        state_scr[...] = jnp.zeros(state_scr.shape, F32)

    lane = lax.broadcasted_iota(jnp.int32, (c, LANES), 1)
    for p in range(RET_HEADS // 2):
        qp = q_ref[:, p * LANES:(p + 1) * LANES]
        kp = k_ref[:, p * LANES:(p + 1) * LANES]
        vp = v_ref[:, p * 2 * RET_V_DIM:(p + 1) * 2 * RET_V_DIM]
        state = state_scr[p]
        state_b = state.astype(BF16)
        qdec = qdec_ref[:, p * LANES:(p + 1) * LANES]
        for e in range(2):
            head = 2 * p + e
            keep = (lane < RET_QK_DIM) if e == 0 else (lane >= RET_QK_DIM)
            qm = jnp.where(keep, qp, jnp.zeros_like(qp))
            s = lax.dot_general(qm, kp, (((1,), (1,)), ((), ())), preferred_element_type=F32)
            s = s * dtab_ref[head]
            ve = vp[:, e * RET_V_DIM:(e + 1) * RET_V_DIM]
            intra = jnp.dot(s.astype(BF16), ve, preferred_element_type=F32)
            qd = (qm.astype(F32) * qdec).astype(BF16)
            cross = jnp.dot(qd, state_b[:, e * RET_V_DIM:(e + 1) * RET_V_DIM],
                            preferred_element_type=F32)
            o = intra + cross
            mu = jnp.mean(o, axis=-1, keepdims=True)
            d = o - mu
            var = jnp.mean(d * d, axis=-1, keepdims=True)
            cols = slice(head * RET_V_DIM, (head + 1) * RET_V_DIM)
            gate = g_ref[:, cols].astype(F32)
            y = d * lax.rsqrt(var + LN_EPS) * gn_ref[:, cols] * (gate * jax.nn.sigmoid(gate))
            o_ref[:, cols] = y.astype(o_ref.dtype)
        kd = (kp.astype(F32) * kdec_ref[:, p * LANES:(p + 1) * LANES]).astype(BF16)
        kv = lax.dot_general(kd, vp, (((0,), (0,)), ((), ())), preferred_element_type=F32)
        state_scr[p] = state * cdec_ref[p] + kv


def _ret_tables(c):
    hh = np.arange(RET_HEADS, dtype=np.float64)
    lg = np.log(1.0 - 2.0 ** (-5.0 - hh))
    idx = np.arange(c, dtype=np.float64)
    nm = idx[:, None] - idx[None, :]
    dtab = np.where(nm >= 0, np.exp(lg[:, None, None] * np.maximum(nm, 0.0)), 0.0) * RET_QK_DIM ** -0.5
    qdec = np.exp(lg[None, :] * (idx + 1.0)[:, None])
    kdec = np.exp(lg[None, :] * (c - 1.0 - idx)[:, None]) * RET_QK_DIM ** -0.5
    qdec = np.repeat(qdec, RET_QK_DIM, axis=1)
    kdec = np.repeat(kdec, RET_QK_DIM, axis=1)
    cdec = np.repeat(np.exp(lg * c), RET_QK_DIM).reshape(RET_HEADS // 2, 2 * RET_QK_DIM, 1)
    return tuple(jnp.asarray(a, F32) for a in (dtab, qdec, kdec, cdec))


def _retention(proj, ret_g, batch, seq):
    c = RET_C
    nc = seq // c
    dtab, qdec, kdec, cdec = _ret_tables(c)
    qkw = RET_HEADS * RET_QK_DIM
    vw = RET_HEADS * RET_V_DIM
    return pl.pallas_call(
        functools.partial(_ret_kernel, c=c),
        grid=(batch, nc),
        in_specs=[
            pl.BlockSpec((c, qkw), lambda b, i: (b * nc + i, OFF_RQ // qkw)),
            pl.BlockSpec((c, qkw), lambda b, i: (b * nc + i, OFF_RK // qkw)),
            pl.BlockSpec((c, vw), lambda b, i: (b * nc + i, OFF_RV // vw)),
            pl.BlockSpec((c, vw), lambda b, i: (b * nc + i, OFF_RG // vw)),
            _resident(dtab.shape), _resident(qdec.shape), _resident(kdec.shape), _resident(cdec.shape),
            _resident(ret_g.shape),
        ],
        out_specs=pl.BlockSpec((c, vw), lambda b, i: (b * nc + i, 0)),
        out_shape=jax.ShapeDtypeStruct((batch * seq, vw), BF16),
        scratch_shapes=[pltpu.VMEM((RET_HEADS // 2, 2 * RET_QK_DIM, 2 * RET_V_DIM), F32)],
        compiler_params=_cparams(("arbitrary", "arbitrary")),
        name="retention",
    )(proj, proj, proj, proj, dtab, qdec, kdec, cdec, ret_g)


def _merge_kernel(x_ref, a_ref, r_ref, mq_ref, mk_ref, mv_ref, wg_ref, bg_ref, wd_ref, wr_ref, wm_ref,
                  wo_ref, g1_ref, b1_ref, o_ref, m_scr, merged_scr, *, nc):
    d = x_ref.shape[1]
    x = x_ref[...]
    xb = x.astype(BF16)

    scale = jnp.asarray(MEM_HEAD_DIM ** -0.5, BF16)
    for h in range(MEM_HEADS):
        cols = slice(h * MEM_HEAD_DIM, (h + 1) * MEM_HEAD_DIM)
        qh = mq_ref[:, cols] * scale
        s = lax.dot_general(qh, mk_ref[0, :, cols], (((1,), (1,)), ((), ())),
                            preferred_element_type=F32)
        p = jnp.exp(s - jnp.max(s, axis=-1, keepdims=True))
        l = jnp.sum(p, axis=-1, keepdims=True)
        mh = jnp.dot(p.astype(BF16), mv_ref[0, :, cols], preferred_element_type=F32) / l
        m_scr[:, cols] = mh.astype(BF16)

    a = a_ref[...]
    r = r_ref[...]
    m = m_scr[...]
    for c in range(0, d, nc):
        cs = slice(c, c + nc)
        acc = None
        for br, (val, w_ref) in enumerate(((a, wd_ref), (r, wr_ref), (m, wm_ref))):
            gs = slice(br * d + c, br * d + c + nc)
            gate = jax.nn.sigmoid(jnp.dot(xb, wg_ref[:, gs], preferred_element_type=F32) + bg_ref[:, gs])
            term = gate * jnp.dot(val, w_ref[:, cs], preferred_element_type=F32)
            acc = term if acc is None else acc + term
        merged_scr[:, cs] = acc.astype(BF16)

    mix = jnp.dot(merged_scr[...], wo_ref[...], preferred_element_type=F32)
    o_ref[...] = _layer_norm(DEEPNORM_ALPHA * x + mix, g1_ref[...], b1_ref[...])


def _merge(x2, a, r, proj, memkv, wg, bg, wd, wr, wm, wo, g1, b1, batch, seq):
    tm = MERGE_TM
    t, d = x2.shape
    nt = seq // tm
    mlen = memkv.shape[1]
    mw = MEM_HEADS * MEM_HEAD_DIM
    row = lambda i: (i, 0)
    return pl.pallas_call(
        functools.partial(_merge_kernel, nc=MERGE_NC),
        grid=(t // tm,),
        in_specs=[
            pl.BlockSpec((tm, d), row),
            pl.BlockSpec((tm, a.shape[1]), row),
            pl.BlockSpec((tm, r.shape[1]), row),
            pl.BlockSpec((tm, mw), lambda i: (i, OFF_MQ // mw)),
            pl.BlockSpec((1, mlen, mw), lambda i: (i // nt, 0, 0)),
            pl.BlockSpec((1, mlen, mw), lambda i: (i // nt, 0, 1)),
            _resident(wg.shape), _resident(bg.shape), _resident(wd.shape), _resident(wr.shape),
            _resident(wm.shape), _resident(wo.shape), _resident(g1.shape), _resident(b1.shape),
        ],
        out_specs=pl.BlockSpec((tm, d), row),
        out_shape=jax.ShapeDtypeStruct((t, d), F32),
        scratch_shapes=[pltpu.VMEM((tm, mw), BF16), pltpu.VMEM((tm, d), BF16)],
        compiler_params=_cparams(("arbitrary",)),
        name="merge",
    )(x2, a, r, proj, memkv, memkv, wg, bg, wd, wr, wm, wo, g1, b1)


def _ffn_kernel(h_ref, wu_ref, cw_ref, cb_ref, wdn_ref, g2_ref, b2_ref, o_ref,
                carry_scr, hs_scr, acc_scr, *, fc, ffn_dim, tiles_per_seq):
    tm = h_ref.shape[0]
    halo = SUBLANES

    @pl.when(pl.program_id(0) % tiles_per_seq == 0)
    def _():
        carry_scr[...] = jnp.zeros(carry_scr.shape, F32)

    x = h_ref[...]
    xb = x.astype(BF16)

    def conv(col):
        cs = slice(col, col + fc)
        up = jnp.dot(xb, wu_ref[:, cs], preferred_element_type=F32)
        hs_scr[0:halo, :] = carry_scr[:, cs]
        hs_scr[halo:halo + tm, :] = up
        carry_scr[:, cs] = up[tm - halo:tm, :]
        w = cw_ref[:, cs]
        return (hs_scr[halo - 2:halo - 2 + tm, :] * w[0:1]
                + hs_scr[halo - 1:halo - 1 + tm, :] * w[1:2]
                + up * w[2:3] + cb_ref[:, cs])

    for j, c in enumerate(range(0, ffn_dim, fc)):
        g = conv(c)
        val = conv(ffn_dim + c)
        act = (g * jax.nn.sigmoid(g) * val).astype(BF16)
        part = jnp.dot(act, wdn_ref[c:c + fc, :], preferred_element_type=F32)
        if j == 0:
            acc_scr[...] = part
        else:
            acc_scr[...] += part

    o_ref[...] = _layer_norm(DEEPNORM_ALPHA * x + acc_scr[...], g2_ref[...], b2_ref[...])


def _ffn(h1, wu, cw, cb, wdn, g2, b2, seq):
    tm = FFN_TM
    t, d = h1.shape
    ffn_dim = wdn.shape[0]
    row = lambda i: (i, 0)
    return pl.pallas_call(
        functools.partial(_ffn_kernel, fc=FFN_FC, ffn_dim=ffn_dim, tiles_per_seq=seq // tm),
        grid=(t // tm,),
        in_specs=[
            pl.BlockSpec((tm, d), row),
            _resident(wu.shape), _resident(cw.shape), _resident(cb.shape), _resident(wdn.shape),
            _resident(g2.shape), _resident(b2.shape),
        ],
        out_specs=pl.BlockSpec((tm, d), row),
        out_shape=jax.ShapeDtypeStruct((t, d), F32),
        scratch_shapes=[
            pltpu.VMEM((SUBLANES, 2 * ffn_dim), F32),
            pltpu.VMEM((tm + SUBLANES, FFN_FC), F32),
            pltpu.VMEM((tm, d), F32),
        ],
        compiler_params=_cparams(("arbitrary",)),
        name="conv_ffn",
    )(h1, wu, cw, cb, wdn, g2, b2)


def kernel(x, mem, w_in, diff_lambda, diff_subln_g, ret_norm_g, w_mem_kv, w_diff_o, w_ret_o, w_mem_o,
           w_gate, b_gate, w_mix_out, ln1_g, ln1_b, w_up, conv_w, conv_b, w_down, ln2_g, ln2_b):
    batch, seq, d = x.shape
    mlen = mem.shape[1]
    assert w_in.shape[0] == DEPTH == 1
    assert seq % ATT_T == 0 and seq % RET_C == 0 and seq % FFN_TM == 0 and seq % MERGE_TM == 0
    l = 0
    row = lambda v: v[l].reshape(1, -1)

    x2 = x.reshape(batch * seq, d)
    proj = _proj(x2, w_in[l].astype(BF16), PROJ_TM, PROJ_NC)
    memkv = _proj(mem.reshape(batch * mlen, d), w_mem_kv[l].astype(BF16), batch * mlen, PROJ_NC)
    memkv = memkv.reshape(batch, mlen, 2 * MEM_HEADS * MEM_HEAD_DIM)

    a = _diff_attention(proj, diff_lambda[l], row(diff_subln_g), batch, seq)
    r = _retention(proj, row(ret_norm_g), batch, seq)
    h1 = _merge(x2, a, r, proj, memkv, w_gate[l].astype(BF16), row(b_gate),
                w_diff_o[l].astype(BF16), w_ret_o[l].astype(BF16), w_mem_o[l].astype(BF16),
                w_mix_out[l].astype(BF16), row(ln1_g), row(ln1_b), batch, seq)
    out = _ffn(h1, w_up[l].astype(BF16), conv_w[l], row(conv_b), w_down[l].astype(BF16),
               row(ln2_g), row(ln2_b), seq)
    return out.reshape(batch, seq, d)
```

```python
import functools
import math

import numpy as np
import jax
import jax.numpy as jnp
from jax import lax
from jax.experimental import pallas as pl
from jax.experimental.pallas import tpu as pltpu

F32 = jnp.float32
BF16 = jnp.bfloat16

CHUNK = 64
DIFF_HEADS = 4
DIFF_QK_DIM = 64
DIFF_V_DIM = 128
RET_HEADS = 4
RET_QK_DIM = 64
RET_V_DIM = 128
MEM_HEADS = 4
MEM_HEAD_DIM = 128
N_BRANCHES = 3
CONV_WIDTH = 3
LN_EPS = 1e-5
DEPTH = 1
DEEPNORM_ALPHA = (2.0 * DEPTH) ** 0.25
LAMBDA_INIT = 0.8 - 0.6 * math.exp(-0.3 * 0)
LOG2E = math.log2(math.e)

IN_DQ, IN_DK, IN_DV = 0, 512, 1024
IN_RQ, IN_RK, IN_RV, IN_RG, IN_MQ, IN_END = 1536, 1792, 2048, 2560, 3072, 3584
OFF_DQ, OFF_RQ, OFF_RK, OFF_RV, OFF_RG, OFF_MQ, MAIN_W = 0, 512, 768, 1024, 1536, 2048, 2560

LANES = 128
SUBLANES = 8
BF16_ROWS = 16
VMEM_LIMIT = 56 * 1024 * 1024

PROJ_TM = 512
PROJ_NC = 512
ATT_T = 256
VT_ROWS = DIFF_V_DIM + BF16_ROWS
RET_C = 256
MERGE_TM = 256
MERGE_NC = 512
FFN_TM = 512
FFN_FC = 256


def _cparams(sem):
    return pltpu.CompilerParams(dimension_semantics=sem, vmem_limit_bytes=VMEM_LIMIT)


def _resident(shape):
    nd = len(shape)
    return pl.BlockSpec(shape, lambda *_: (0,) * nd, pipeline_mode=pl.Buffered(1))


def _layer_norm(y, g, b):
    mu = jnp.mean(y, axis=-1, keepdims=True)
    d = y - mu
    var = jnp.mean(d * d, axis=-1, keepdims=True)
    return d * lax.rsqrt(var + LN_EPS) * g + b


def _alibi_slopes():
    return np.array([(2.0 ** (-8.0 / DIFF_HEADS)) ** (i + 1) for i in range(DIFF_HEADS)], np.float64)


def _bf16_split3(v):
    parts = []
    rem = np.float64(v)
    for _ in range(3):
        p = np.asarray(rem, np.float32).astype(jnp.bfloat16)
        parts.append(p)
        rem = rem - np.float64(p.astype(np.float32))
    return parts


def _proj_kernel(x_ref, w_ref, o_ref, *, nc):
    xb = x_ref[...].astype(BF16)
    n = w_ref.shape[1]
    for c in range(0, n, nc):
        o_ref[:, c:c + nc] = jnp.dot(xb, w_ref[:, c:c + nc],
                                     preferred_element_type=F32).astype(o_ref.dtype)


def _proj(x2, w, tm, nc):
    t, d = x2.shape
    n = w.shape[1]
    return pl.pallas_call(
        functools.partial(_proj_kernel, nc=nc),
        grid=(t // tm,),
        in_specs=[pl.BlockSpec((tm, d), lambda i: (i, 0)), _resident((d, n))],
        out_specs=pl.BlockSpec((tm, n), lambda i: (i, 0)),
        out_shape=jax.ShapeDtypeStruct((t, n), BF16),
        compiler_params=_cparams(("arbitrary",)),
        name="memkv",
    )(x2, w)


def _inproj_kernel(x_ref, wm_ref, wk_ref, wvt_ref, kaug_ref, main_ref, kp_ref, vt_ref, *, nc):
    xb = x_ref[...].astype(BF16)
    qscale = LOG2E * DIFF_QK_DIM ** -0.5
    for c in range(0, MAIN_W, nc):
        acc = jnp.dot(xb, wm_ref[:, c:c + nc], preferred_element_type=F32)
        if c < OFF_RQ:
            acc = acc * qscale
        main_ref[:, c:c + nc] = acc.astype(main_ref.dtype)

    k = jnp.dot(xb, wk_ref[...], preferred_element_type=F32).astype(kp_ref.dtype)
    for h in range(DIFF_HEADS):
        kp_ref[:, h * 2 * LANES:h * 2 * LANES + LANES] = k[:, h * LANES:(h + 1) * LANES]
        kp_ref[:, h * 2 * LANES + LANES:(h + 1) * 2 * LANES] = kaug_ref[:, h * LANES:(h + 1) * LANES]

    vt = lax.dot_general(wvt_ref[...], xb, (((1,), (1,)), ((), ())), preferred_element_type=F32)
    row = lax.broadcasted_iota(jnp.int32, vt.shape, 0)
    ones_row = row == DIFF_V_DIM
    for h in range(1, DIFF_HEADS):
        ones_row = ones_row | (row == h * VT_ROWS + DIFF_V_DIM)
    vt_ref[...] = jnp.where(ones_row, 1.0, vt).astype(vt_ref.dtype)


def _inproj(x2, w_in):
    t, d = x2.shape
    tm = PROJ_TM
    cols = lambda a, b: w_in[:, a:b]
    w_main = jnp.concatenate([cols(IN_DQ, IN_DK), cols(IN_RQ, IN_END)], axis=1).astype(BF16)
    w_k = cols(IN_DK, IN_DV).astype(BF16)
    wv = cols(IN_DV, IN_RQ).astype(BF16).T.reshape(DIFF_HEADS, DIFF_V_DIM, d)
    w_vt = jnp.pad(wv, ((0, 0), (0, VT_ROWS - DIFF_V_DIM), (0, 0))).reshape(DIFF_HEADS * VT_ROWS, d)

    l3 = _bf16_split3(LOG2E)
    jl = (np.arange(tm) % ATT_T).astype(np.float64)
    kaug = np.zeros((tm, DIFF_HEADS, LANES), np.float32)
    for hh, m in enumerate(_alibi_slopes()):
        for i in range(3):
            kaug[:, hh, i] = np.float32(l3[i])
            kaug[:, hh, 3 + i] = m * jl
    kaug = jnp.asarray(kaug.reshape(tm, DIFF_HEADS * LANES), BF16)

    row = lambda i: (i, 0)
    return pl.pallas_call(
        functools.partial(_inproj_kernel, nc=PROJ_NC),
        grid=(t // tm,),
        in_specs=[pl.BlockSpec((tm, d), row), _resident(w_main.shape), _resident(w_k.shape),
                  _resident(w_vt.shape), _resident(kaug.shape)],
        out_specs=[pl.BlockSpec((tm, MAIN_W), row),
                   pl.BlockSpec((tm, DIFF_HEADS * 2 * LANES), row),
                   pl.BlockSpec((DIFF_HEADS * VT_ROWS, tm), lambda i: (0, i))],
        out_shape=[jax.ShapeDtypeStruct((t, MAIN_W), BF16),
                   jax.ShapeDtypeStruct((t, DIFF_HEADS * 2 * LANES), BF16),
                   jax.ShapeDtypeStruct((DIFF_HEADS * VT_ROWS, t), BF16)],
        compiler_params=_cparams(("arbitrary",)),
        name="inproj",
    )(x2, w_main, w_k, w_vt, kaug)


def _attn_kernel(slope_ref, q_ref, qaug_ref, kp_ref, vt_ref, tdiag_ref, lam_ref, g_ref, o_ref,
                 qs_scr, m_scr, acc_scr, z_scr, p_scr, alpha_scr, *, t):
    h = pl.program_id(1)
    qi = pl.program_id(2)
    slope = slope_ref[h]

    q = q_ref[...]
    lane = lax.broadcasted_iota(jnp.int32, q.shape, 1)
    zero = jnp.zeros_like(q)
    qa = qaug_ref[0]
    qs_scr[0:t, 0:LANES] = jnp.where(lane < DIFF_QK_DIM, q, zero)
    qs_scr[t:2 * t, 0:LANES] = jnp.where(lane >= DIFF_QK_DIM, q, zero)
    qs_scr[0:t, LANES:2 * LANES] = qa
    qs_scr[t:2 * t, LANES:2 * LANES] = qa

    m_scr[...] = jnp.full(m_scr.shape, -jnp.inf, F32)
    acc_scr[...] = jnp.zeros(acc_scr.shape, F32)
    p_scr[1] = jnp.zeros(p_scr.shape[1:], BF16)
    alpha_scr[1] = jnp.ones(alpha_scr.shape[1:], F32)

    def score_stage(kb, slot):
        start = pl.multiple_of(kb * t, t)
        z_scr[slot] = lax.dot_general(kp_ref[pl.ds(start, t), :], qs_scr[...], (((1,), (1,)), ((), ())),
                                      preferred_element_type=F32)

    def softmax_stage(z, c, slot):
        m_prev = m_scr[...]
        m_new = jnp.maximum(m_prev, jnp.max(z, axis=0, keepdims=True) + c)
        p_scr[slot] = jnp.exp2(z - (m_new - c)).astype(BF16)
        alpha_scr[slot] = jnp.exp2(m_prev - m_new)
        m_scr[...] = m_new

    def value_stage(kb, slot):
        start = pl.multiple_of(kb * t, t)
        acc_scr[...] = alpha_scr[slot] * acc_scr[...] + jnp.dot(
            vt_ref[:, pl.ds(start, t)], p_scr[slot], preferred_element_type=F32)

    def pipelined(kb, cur, nxt):
        score_stage(kb + 1, nxt)
        softmax_stage(z_scr[cur], -slope * ((qi - kb) * t).astype(F32), cur)
        value_stage(jnp.maximum(kb - 1, 0), nxt)

    def drain(slot):
        td = tdiag_ref[0]
        z = z_scr[slot]
        z = jnp.concatenate([z[:, 0:t] + td, z[:, t:2 * t] + td], axis=1)
        softmax_stage(z, jnp.asarray(0.0, F32), slot)
        value_stage(jnp.maximum(qi - 1, 0), 1 - slot)
        value_stage(qi, slot)

    score_stage(0, 0)

    def pair(i, carry):
        pipelined(2 * i, 0, 1)
        pipelined(2 * i + 1, 1, 0)
        return carry

    lax.fori_loop(0, qi // 2, pair, 0)

    @pl.when(qi % 2 == 1)
    def _():
        pipelined(qi - 1, 0, 1)
        drain(1)

    @pl.when(qi % 2 == 0)
    def _():
        drain(0)

    lp = lam_ref[...]
    lam = (jnp.exp(jnp.sum(lp[0:1] * lp[1:2], axis=-1, keepdims=True))
           - jnp.exp(jnp.sum(lp[2:3] * lp[3:4], axis=-1, keepdims=True)) + LAMBDA_INIT)
    acc = acc_scr[...]
    o = acc[0:DIFF_V_DIM, :] / acc[DIFF_V_DIM:DIFF_V_DIM + 1, :]
    y = o[:, 0:t] - lam * o[:, t:2 * t]
    y = y * lax.rsqrt(jnp.mean(y * y, axis=0, keepdims=True) + LN_EPS)
    o_ref[...] = (y.T * g_ref[...] * (1.0 - LAMBDA_INIT)).astype(o_ref.dtype)


def _attn_tables(t):
    slopes = _alibi_slopes()
    l3 = _bf16_split3(LOG2E)
    il = np.arange(t, dtype=np.float64)
    qaug = np.zeros((DIFF_HEADS, t, LANES), np.float32)
    for hh, m in enumerate(slopes):
        for i in range(3):
            qaug[hh, :, i] = -m * il
            qaug[hh, :, 3 + i] = np.float32(l3[i])
    j = np.arange(t)[:, None]
    i = np.arange(t)[None, :]
    allowed = (j // CHUNK) <= (i // CHUNK)
    corr = LOG2E * slopes[:, None, None] * ((i - j) - np.abs(i - j))[None]
    tdiag = np.where(allowed[None], corr, -np.inf)
    return (jnp.asarray(slopes * LOG2E, F32), jnp.asarray(qaug, BF16), jnp.asarray(tdiag, F32))


def _diff_attention(main, kp, vt, lam_p, subln_g, batch, seq):
    t = ATT_T
    nq = seq // t
    slopes, qaug, tdiag = _attn_tables(t)
    return pl.pallas_call(
        functools.partial(_attn_kernel, t=t),
        grid=(batch, DIFF_HEADS, nq),
        in_specs=[
            pl.BlockSpec(memory_space=pltpu.SMEM),
            pl.BlockSpec((t, LANES), lambda b, h, qi: (b * nq + qi, OFF_DQ // LANES + h)),
            pl.BlockSpec((1, t, LANES), lambda b, h, qi: (h, 0, 0)),
            pl.BlockSpec((seq, 2 * LANES), lambda b, h, qi: (b, h)),
            pl.BlockSpec((VT_ROWS, seq), lambda b, h, qi: (h, b)),
            pl.BlockSpec((1, t, t), lambda b, h, qi: (h, 0, 0)),
            pl.BlockSpec(lam_p.shape, lambda b, h, qi: (0, 0)),
            pl.BlockSpec((1, LANES), lambda b, h, qi: (0, h)),
        ],
        out_specs=pl.BlockSpec((t, LANES), lambda b, h, qi: (b * nq + qi, h)),
        out_shape=jax.ShapeDtypeStruct((batch * seq, DIFF_HEADS * DIFF_V_DIM), BF16),
        scratch_shapes=[
            pltpu.VMEM((2 * t, 2 * LANES), BF16),
            pltpu.VMEM((1, 2 * t), F32),
            pltpu.VMEM((VT_ROWS, 2 * t), F32),
            pltpu.VMEM((2, t, 2 * t), F32),
            pltpu.VMEM((2, t, 2 * t), BF16),
            pltpu.VMEM((2, 1, 2 * t), F32),
        ],
        compiler_params=_cparams(("arbitrary", "arbitrary", "arbitrary")),
        name="diff_attn",
    )(slopes, main, qaug, kp, vt, tdiag, lam_p, subln_g)


def _ret_kernel(q_ref, k_ref, v_ref, g_ref, dtab_ref, qdec_ref, kdec_ref, cdec_ref, gn_ref, o_ref,
                state_scr, *, c):
    @pl.when(pl.program_id(1) == 0)
    def _():
        state_scr[...] = jnp.zeros(state_scr.shape, F32)

    lane = lax.broadcasted_iota(jnp.int32, (c, LANES), 1)
    for p in range(RET_HEADS // 2):
        qp = q_ref[:, p * LANES:(p + 1) * LANES]
        kp = k_ref[:, p * LANES:(p + 1) * LANES]
        vp = v_ref[:, p * 2 * RET_V_DIM:(p + 1) * 2 * RET_V_DIM]
        state = state_scr[p]
        state_b = state.astype(BF16)
        qdec = qdec_ref[:, p * LANES:(p + 1) * LANES]
        for e in range(2):
            head = 2 * p + e
            keep = (lane < RET_QK_DIM) if e == 0 else (lane >= RET_QK_DIM)
            qm = jnp.where(keep, qp, jnp.zeros_like(qp))
            s = lax.dot_general(qm, kp, (((1,), (1,)), ((), ())), preferred_element_type=F32)
            s = s * dtab_ref[head]
            ve = vp[:, e * RET_V_DIM:(e + 1) * RET_V_DIM]
            intra = jnp.dot(s.astype(BF16), ve, preferred_element_type=F32)
            qd = (qm.astype(F32) * qdec).astype(BF16)
            cross = jnp.dot(qd, state_b[:, e * RET_V_DIM:(e + 1) * RET_V_DIM],
                            preferred_element_type=F32)
            o = intra + cross
            mu = jnp.mean(o, axis=-1, keepdims=True)
            d = o - mu
            var = jnp.mean(d * d, axis=-1, keepdims=True)
            cols = slice(head * RET_V_DIM, (head + 1) * RET_V_DIM)
            gate = g_ref[:, cols].astype(F32)
            y = d * lax.rsqrt(var + LN_EPS) * gn_ref[:, cols] * (gate * jax.nn.sigmoid(gate))
            o_ref[:, cols] = y.astype(o_ref.dtype)
        kd = (kp.astype(F32) * kdec_ref[:, p * LANES:(p + 1) * LANES]).astype(BF16)
        kv = lax.dot_general(kd, vp, (((0,), (0,)), ((), ())), preferred_element_type=F32)
        state_scr[p] = state * cdec_ref[p] + kv


def _ret_tables(c):
    hh = np.arange(RET_HEADS, dtype=np.float64)
    lg = np.log(1.0 - 2.0 ** (-5.0 - hh))
    idx = np.arange(c, dtype=np.float64)
    nm = idx[:, None] - idx[None, :]
    dtab = np.where(nm >= 0, np.exp(lg[:, None, None] * np.maximum(nm, 0.0)), 0.0) * RET_QK_DIM ** -0.5
    qdec = np.exp(lg[None, :] * (idx + 1.0)[:, None])
    kdec = np.exp(lg[None, :] * (c - 1.0 - idx)[:, None]) * RET_QK_DIM ** -0.5
    qdec = np.repeat(qdec, RET_QK_DIM, axis=1)
    kdec = np.repeat(kdec, RET_QK_DIM, axis=1)
    cdec = np.repeat(np.exp(lg * c), RET_QK_DIM).reshape(RET_HEADS // 2, 2 * RET_QK_DIM, 1)
    return tuple(jnp.asarray(a, F32) for a in (dtab, qdec, kdec, cdec))


def _retention(main, ret_g, batch, seq):
    c = RET_C
    nc = seq // c
    dtab, qdec, kdec, cdec = _ret_tables(c)
    qkw = RET_HEADS * RET_QK_DIM
    vw = RET_HEADS * RET_V_DIM
    return pl.pallas_call(
        functools.partial(_ret_kernel, c=c),
        grid=(batch, nc),
        in_specs=[
            pl.BlockSpec((c, qkw), lambda b, i: (b * nc + i, OFF_RQ // qkw)),
            pl.BlockSpec((c, qkw), lambda b, i: (b * nc + i, OFF_RK // qkw)),
            pl.BlockSpec((c, vw), lambda b, i: (b * nc + i, OFF_RV // vw)),
            pl.BlockSpec((c, vw), lambda b, i: (b * nc + i, OFF_RG // vw)),
            _resident(dtab.shape), _resident(qdec.shape), _resident(kdec.shape), _resident(cdec.shape),
            _resident(ret_g.shape),
        ],
        out_specs=pl.BlockSpec((c, vw), lambda b, i: (b * nc + i, 0)),
        out_shape=jax.ShapeDtypeStruct((batch * seq, vw), BF16),
        scratch_shapes=[pltpu.VMEM((RET_HEADS // 2, 2 * RET_QK_DIM, 2 * RET_V_DIM), F32)],
        compiler_params=_cparams(("arbitrary", "arbitrary")),
        name="retention",
    )(main, main, main, main, dtab, qdec, kdec, cdec, ret_g)


def _merge_kernel(x_ref, a_ref, r_ref, mq_ref, mk_ref, mv_ref, wg_ref, bg_ref, wd_ref, wr_ref, wm_ref,
                  wo_ref, g1_ref, b1_ref, o_ref, m_scr, merged_scr, *, nc):
    d = x_ref.shape[1]
    x = x_ref[...]
    xb = x.astype(BF16)

    scale = jnp.asarray(MEM_HEAD_DIM ** -0.5, BF16)
    for h in range(MEM_HEADS):
        cols = slice(h * MEM_HEAD_DIM, (h + 1) * MEM_HEAD_DIM)
        qh = mq_ref[:, cols] * scale
        s = lax.dot_general(qh, mk_ref[0, :, cols], (((1,), (1,)), ((), ())),
                            preferred_element_type=F32)
        p = jnp.exp(s - jnp.max(s, axis=-1, keepdims=True))
        l = jnp.sum(p, axis=-1, keepdims=True)
        mh = jnp.dot(p.astype(BF16), mv_ref[0, :, cols], preferred_element_type=F32) / l
        m_scr[:, cols] = mh.astype(BF16)

    a = a_ref[...]
    r = r_ref[...]
    m = m_scr[...]
    for c in range(0, d, nc):
        cs = slice(c, c + nc)
        acc = None
        for br, (val, w_ref) in enumerate(((a, wd_ref), (r, wr_ref), (m, wm_ref))):
            gs = slice(br * d + c, br * d + c + nc)
            gate = jax.nn.sigmoid(jnp.dot(xb, wg_ref[:, gs], preferred_element_type=F32) + bg_ref[:, gs])
            term = gate * jnp.dot(val, w_ref[:, cs], preferred_element_type=F32)
            acc = term if acc is None else acc + term
        merged_scr[:, cs] = acc.astype(BF16)

    mix = jnp.dot(merged_scr[...], wo_ref[...], preferred_element_type=F32)
    o_ref[...] = _layer_norm(DEEPNORM_ALPHA * x + mix, g1_ref[...], b1_ref[...])


def _merge(x2, a, r, main, memkv, wg, bg, wd, wr, wm, wo, g1, b1, batch, seq):
    tm = MERGE_TM
    t, d = x2.shape
    nt = seq // tm
    mlen = memkv.shape[1]
    mw = MEM_HEADS * MEM_HEAD_DIM
    row = lambda i: (i, 0)
    return pl.pallas_call(
        functools.partial(_merge_kernel, nc=MERGE_NC),
        grid=(t // tm,),
        in_specs=[
            pl.BlockSpec((tm, d), row),
            pl.BlockSpec((tm, a.shape[1]), row),
            pl.BlockSpec((tm, r.shape[1]), row),
            pl.BlockSpec((tm, mw), lambda i: (i, OFF_MQ // mw)),
            pl.BlockSpec((1, mlen, mw), lambda i: (i // nt, 0, 0)),
            pl.BlockSpec((1, mlen, mw), lambda i: (i // nt, 0, 1)),
            _resident(wg.shape), _resident(bg.shape), _resident(wd.shape), _resident(wr.shape),
            _resident(wm.shape), _resident(wo.shape), _resident(g1.shape), _resident(b1.shape),
        ],
        out_specs=pl.BlockSpec((tm, d), row),
        out_shape=jax.ShapeDtypeStruct((t, d), F32),
        scratch_shapes=[pltpu.VMEM((tm, mw), BF16), pltpu.VMEM((tm, d), BF16)],
        compiler_params=_cparams(("arbitrary",)),
        name="merge",
    )(x2, a, r, main, memkv, memkv, wg, bg, wd, wr, wm, wo, g1, b1)


def _ffn_kernel(h_ref, wu_ref, cw_ref, cb_ref, wdn_ref, g2_ref, b2_ref, o_ref,
                carry_scr, hs_scr, acc_scr, *, fc, ffn_dim, tiles_per_seq):
    tm = h_ref.shape[0]
    halo = SUBLANES

    @pl.when(pl.program_id(0) % tiles_per_seq == 0)
    def _():
        carry_scr[...] = jnp.zeros(carry_scr.shape, F32)

    x = h_ref[...]
    xb = x.astype(BF16)

    def conv(col):
        cs = slice(col, col + fc)
        up = jnp.dot(xb, wu_ref[:, cs], preferred_element_type=F32)
        hs_scr[0:halo, :] = carry_scr[:, cs]
        hs_scr[halo:halo + tm, :] = up
        carry_scr[:, cs] = up[tm - halo:tm, :]
        w = cw_ref[:, cs]
        return (hs_scr[halo - 2:halo - 2 + tm, :] * w[0:1]
                + hs_scr[halo - 1:halo - 1 + tm, :] * w[1:2]
                + up * w[2:3] + cb_ref[:, cs])

    for j, c in enumerate(range(0, ffn_dim, fc)):
        g = conv(c)
        val = conv(ffn_dim + c)
        act = (g * jax.nn.sigmoid(g) * val).astype(BF16)
        part = jnp.dot(act, wdn_ref[c:c + fc, :], preferred_element_type=F32)
        if j == 0:
            acc_scr[...] = part
        else:
            acc_scr[...] += part

    o_ref[...] = _layer_norm(DEEPNORM_ALPHA * x + acc_scr[...], g2_ref[...], b2_ref[...])


def _ffn(h1, wu, cw, cb, wdn, g2, b2, seq):
    tm = FFN_TM
    t, d = h1.shape
    ffn_dim = wdn.shape[0]
    row = lambda i: (i, 0)
    return pl.pallas_call(
        functools.partial(_ffn_kernel, fc=FFN_FC, ffn_dim=ffn_dim, tiles_per_seq=seq // tm),
        grid=(t // tm,),
        in_specs=[
            pl.BlockSpec((tm, d), row),
            _resident(wu.shape), _resident(cw.shape), _resident(cb.shape), _resident(wdn.shape),
            _resident(g2.shape), _resident(b2.shape),
        ],
        out_specs=pl.BlockSpec((tm, d), row),
        out_shape=jax.ShapeDtypeStruct((t, d), F32),
        scratch_shapes=[
            pltpu.VMEM((SUBLANES, 2 * ffn_dim), F32),
            pltpu.VMEM((tm + SUBLANES, FFN_FC), F32),
            pltpu.VMEM((tm, d), F32),
        ],
        compiler_params=_cparams(("arbitrary",)),
        name="conv_ffn",
    )(h1, wu, cw, cb, wdn, g2, b2)


def kernel(x, mem, w_in, diff_lambda, diff_subln_g, ret_norm_g, w_mem_kv, w_diff_o, w_ret_o, w_mem_o,
           w_gate, b_gate, w_mix_out, ln1_g, ln1_b, w_up, conv_w, conv_b, w_down, ln2_g, ln2_b):
    batch, seq, d = x.shape
    mlen = mem.shape[1]
    assert w_in.shape[0] == DEPTH == 1
    assert seq % PROJ_TM == 0 and seq % RET_C == 0 and seq % FFN_TM == 0 and seq % MERGE_TM == 0
    assert PROJ_TM % ATT_T == 0 and ATT_T % CHUNK == 0
    l = 0
    row = lambda v: v[l].reshape(1, -1)

    x2 = x.reshape(batch * seq, d)
    main, kp, vt = _inproj(x2, w_in[l])
    memkv = _proj(mem.reshape(batch * mlen, d), w_mem_kv[l].astype(BF16), batch * mlen, PROJ_NC)
    memkv = memkv.reshape(batch, mlen, 2 * MEM_HEADS * MEM_HEAD_DIM)

    a = _diff_attention(main, kp, vt, diff_lambda[l], row(diff_subln_g), batch, seq)
    r = _retention(main, row(ret_norm_g), batch, seq)
    h1 = _merge(x2, a, r, main, memkv, w_gate[l].astype(BF16), row(b_gate),
                w_diff_o[l].astype(BF16), w_ret_o[l].astype(BF16), w_mem_o[l].astype(BF16),
                w_mix_out[l].astype(BF16), row(ln1_g), row(ln1_b), batch, seq)
    out = _ffn(h1, w_up[l].astype(BF16), conv_w[l], row(conv_b), w_down[l].astype(BF16),
               row(ln2_g), row(ln2_b), seq)
    return out.reshape(batch, seq, d)
```

```python
import functools
import math

import numpy as np
import jax
import jax.numpy as jnp
from jax import lax
from jax.experimental import pallas as pl
from jax.experimental.pallas import tpu as pltpu

F32 = jnp.float32
BF16 = jnp.bfloat16

CHUNK = 64
DIFF_HEADS = 4
DIFF_QK_DIM = 64
DIFF_V_DIM = 128
RET_HEADS = 4
RET_QK_DIM = 64
RET_V_DIM = 128
MEM_HEADS = 4
MEM_HEAD_DIM = 128
N_BRANCHES = 3
CONV_WIDTH = 3
LN_EPS = 1e-5
DEPTH = 1
DEEPNORM_ALPHA = (2.0 * DEPTH) ** 0.25
LAMBDA_INIT = 0.8 - 0.6 * math.exp(-0.3 * 0)
LOG2E = math.log2(math.e)

IN_DQ, IN_DK, IN_DV = 0, 512, 1024
IN_RQ, IN_RK, IN_RV, IN_RG, IN_MQ, IN_END = 1536, 1792, 2048, 2560, 3072, 3584
OFF_DQ, OFF_RQ, OFF_RK, OFF_RV, OFF_RG, OFF_MQ, MAIN_W = 0, 512, 768, 1024, 1536, 2048, 2560

LANES = 128
SUBLANES = 8
BF16_ROWS = 16
VMEM_LIMIT = 56 * 1024 * 1024

PROJ_TM = 512
PROJ_NC = 512
ATT_TQ = 256
ATT_TK = 512
ATT_NS = 4
ATT_UNROLL = 2
VT_ROWS = DIFF_V_DIM + BF16_ROWS
RET_C = 256
MERGE_TM = 512
MERGE_NC = 512
FFN_TM = 512
FFN_FC = 256
FFN_RB = 128


def _cparams(sem):
    return pltpu.CompilerParams(dimension_semantics=sem, vmem_limit_bytes=VMEM_LIMIT)


def _resident(shape):
    nd = len(shape)
    return pl.BlockSpec(shape, lambda *_: (0,) * nd, pipeline_mode=pl.Buffered(1))


def _layer_norm(y, g, b):
    mu = jnp.mean(y, axis=-1, keepdims=True)
    d = y - mu
    var = jnp.mean(d * d, axis=-1, keepdims=True)
    return d * lax.rsqrt(var + LN_EPS) * g + b


def _alibi_slopes():
    return np.array([(2.0 ** (-8.0 / DIFF_HEADS)) ** (i + 1) for i in range(DIFF_HEADS)], np.float64)


def _bf16_split3(v):
    parts = []
    rem = np.float64(v)
    for _ in range(3):
        p = np.asarray(rem, np.float32).astype(jnp.bfloat16)
        parts.append(p)
        rem = rem - np.float64(p.astype(np.float32))
    return parts


def _proj_kernel(x_ref, w_ref, o_ref, *, nc):
    xb = x_ref[...].astype(BF16)
    n = w_ref.shape[1]
    for c in range(0, n, nc):
        o_ref[:, c:c + nc] = jnp.dot(xb, w_ref[:, c:c + nc],
                                     preferred_element_type=F32).astype(o_ref.dtype)


def _proj(x2, w, tm, nc):
    t, d = x2.shape
    n = w.shape[1]
    return pl.pallas_call(
        functools.partial(_proj_kernel, nc=nc),
        grid=(t // tm,),
        in_specs=[pl.BlockSpec((tm, d), lambda i: (i, 0)), _resident((d, n))],
        out_specs=pl.BlockSpec((tm, n), lambda i: (i, 0)),
        out_shape=jax.ShapeDtypeStruct((t, n), BF16),
        compiler_params=_cparams(("arbitrary",)),
        name="memkv",
    )(x2, w)


def _inproj_kernel(x_ref, wm_ref, wk_ref, wvt_ref, kaug_ref, main_ref, kp_ref, vt_ref, *, nc):
    xb = x_ref[...].astype(BF16)
    qscale = LOG2E * DIFF_QK_DIM ** -0.5
    for c in range(0, MAIN_W, nc):
        acc = jnp.dot(xb, wm_ref[:, c:c + nc], preferred_element_type=F32)
        if c < OFF_RQ:
            acc = acc * qscale
        main_ref[:, c:c + nc] = acc.astype(main_ref.dtype)

    k = jnp.dot(xb, wk_ref[...], preferred_element_type=F32).astype(kp_ref.dtype)
    for h in range(DIFF_HEADS):
        kp_ref[:, h * 2 * LANES:h * 2 * LANES + LANES] = k[:, h * LANES:(h + 1) * LANES]
        kp_ref[:, h * 2 * LANES + LANES:(h + 1) * 2 * LANES] = kaug_ref[:, h * LANES:(h + 1) * LANES]

    vt = lax.dot_general(wvt_ref[...], xb, (((1,), (1,)), ((), ())), preferred_element_type=F32)
    row = lax.broadcasted_iota(jnp.int32, vt.shape, 0)
    ones_row = row == DIFF_V_DIM
    for h in range(1, DIFF_HEADS):
        ones_row = ones_row | (row == h * VT_ROWS + DIFF_V_DIM)
    vt_ref[...] = jnp.where(ones_row, 1.0, vt).astype(vt_ref.dtype)


def _inproj(x2, w_in):
    t, d = x2.shape
    tm = PROJ_TM
    cols = lambda a, b: w_in[:, a:b]
    w_main = jnp.concatenate([cols(IN_DQ, IN_DK), cols(IN_RQ, IN_END)], axis=1).astype(BF16)
    w_k = cols(IN_DK, IN_DV).astype(BF16)
    wv = cols(IN_DV, IN_RQ).astype(BF16).T.reshape(DIFF_HEADS, DIFF_V_DIM, d)
    w_vt = jnp.pad(wv, ((0, 0), (0, VT_ROWS - DIFF_V_DIM), (0, 0))).reshape(DIFF_HEADS * VT_ROWS, d)

    kaug = _alibi_key_columns(tm, ATT_TK)

    row = lambda i: (i, 0)
    return pl.pallas_call(
        functools.partial(_inproj_kernel, nc=PROJ_NC),
        grid=(t // tm,),
        in_specs=[pl.BlockSpec((tm, d), row), _resident(w_main.shape), _resident(w_k.shape),
                  _resident(w_vt.shape), _resident(kaug.shape)],
        out_specs=[pl.BlockSpec((tm, MAIN_W), row),
                   pl.BlockSpec((tm, DIFF_HEADS * 2 * LANES), row),
                   pl.BlockSpec((DIFF_HEADS * VT_ROWS, tm), lambda i: (0, i))],
        out_shape=[jax.ShapeDtypeStruct((t, MAIN_W), BF16),
                   jax.ShapeDtypeStruct((t, DIFF_HEADS * 2 * LANES), BF16),
                   jax.ShapeDtypeStruct((DIFF_HEADS * VT_ROWS, t), BF16)],
        compiler_params=_cparams(("arbitrary",)),
        name="inproj",
    )(x2, w_main, w_k, w_vt, kaug)


def _attn_kernel(slope_ref, q_ref, qaug_ref, kp_ref, vt_ref, tdiag_ref, lam_ref, g_ref, o_ref,
                 qs_scr, m_scr, acc_scr, z_scr, zmax_scr, p_scr, alpha_scr, *, tq, tk, ns):
    h = pl.program_id(1)
    per_tile = tk // tq
    n0 = (ns // per_tile) * pl.program_id(2)
    slope = slope_ref[h]

    qa = qaug_ref[0]
    for j in range(ns):
        q = q_ref[j * tq:(j + 1) * tq, :]
        lane = lax.broadcasted_iota(jnp.int32, q.shape, 1)
        zero = jnp.zeros_like(q)
        qs_scr[j, 0:tq, 0:LANES] = jnp.where(lane < DIFF_QK_DIM, q, zero)
        qs_scr[j, tq:2 * tq, 0:LANES] = jnp.where(lane >= DIFF_QK_DIM, q, zero)
        qs_scr[j, 0:tq, LANES:2 * LANES] = qa
        qs_scr[j, tq:2 * tq, LANES:2 * LANES] = qa
        m_scr[j] = jnp.full(m_scr.shape[1:], -jnp.inf, F32)
        acc_scr[j] = jnp.zeros(acc_scr.shape[1:], F32)
        p_scr[j, 1] = jnp.zeros(p_scr.shape[2:], BF16)
        alpha_scr[j, 1] = jnp.ones(alpha_scr.shape[2:], F32)

    def score_stage(j, kb, slot):
        start = pl.multiple_of(kb * tk, tk)
        z = lax.dot_general(kp_ref[pl.ds(start, tk), :], qs_scr[j], (((1,), (1,)), ((), ())),
                            preferred_element_type=F32)
        z_scr[j, slot] = z
        zmax_scr[j, slot] = jnp.max(z, axis=0, keepdims=True)

    def softmax_stage(j, z, zmax, c, slot):
        m_prev = m_scr[j]
        m_new = jnp.maximum(m_prev, zmax + c)
        p_scr[j, slot] = jnp.exp2(z - (m_new - c)).astype(BF16)
        alpha_scr[j, slot] = jnp.exp2(m_prev - m_new)
        m_scr[j] = m_new

    def value_stage(j, kb, slot):
        start = pl.multiple_of(kb * tk, tk)
        acc_scr[j] = alpha_scr[j, slot] * acc_scr[j] + jnp.dot(
            vt_ref[:, pl.ds(start, tk)], p_scr[j, slot], preferred_element_type=F32)

    def tile_bias(j, kb):
        return -slope * ((n0 - kb) * tk + j * tq).astype(F32)

    def full_trip(kb, cur, nxt):
        for j in range(ns):
            score_stage(j, kb + 1, nxt)
            softmax_stage(j, z_scr[j, cur], zmax_scr[j, cur], tile_bias(j, kb), cur)
            value_stage(j, jnp.maximum(kb - 1, 0), nxt)

    for j in range(ns):
        score_stage(j, 0, 0)

    def unrolled(i, carry):
        for u in range(ATT_UNROLL):
            full_trip(ATT_UNROLL * i + u, u % 2, 1 - u % 2)
        return carry

    lax.fori_loop(0, n0 // ATT_UNROLL, unrolled, 0)

    for i in range(ns // per_tile + 1):
        kb = n0 + i
        cur, nxt = i % 2, 1 - i % 2
        for j in range(ns):
            last = j // per_tile
            if i < last:
                score_stage(j, kb + 1, nxt)
                softmax_stage(j, z_scr[j, cur], zmax_scr[j, cur], tile_bias(j, kb), cur)
            elif i == last:
                td = tdiag_ref[0, j % per_tile]
                z = z_scr[j, cur]
                z = jnp.concatenate([z[:, 0:tq] + td, z[:, tq:2 * tq] + td], axis=1)
                softmax_stage(j, z, jnp.max(z, axis=0, keepdims=True), tile_bias(j, kb), cur)
            if i <= last + 1:
                value_stage(j, jnp.maximum(kb - 1, 0), nxt)

    lp = lam_ref[...]
    lam = (jnp.exp(jnp.sum(lp[0:1] * lp[1:2], axis=-1, keepdims=True))
           - jnp.exp(jnp.sum(lp[2:3] * lp[3:4], axis=-1, keepdims=True)) + LAMBDA_INIT)
    for j in range(ns):
        acc = acc_scr[j]
        o = acc[0:DIFF_V_DIM, :] / acc[DIFF_V_DIM:DIFF_V_DIM + 1, :]
        y = o[:, 0:tq] - lam * o[:, tq:2 * tq]
        y = y * lax.rsqrt(jnp.mean(y * y, axis=0, keepdims=True) + LN_EPS)
        o_ref[j * tq:(j + 1) * tq, :] = (y.T * g_ref[...] * (1.0 - LAMBDA_INIT)).astype(o_ref.dtype)


def _alibi_key_columns(n_rows, tk):
    l3 = _bf16_split3(LOG2E)
    off = np.arange(n_rows) % tk
    kaug = np.zeros((n_rows, DIFF_HEADS, LANES), np.float32)
    for hh, m in enumerate(_alibi_slopes()):
        for i in range(3):
            kaug[:, hh, i] = np.float32(l3[i])
            kaug[:, hh, 3 + i] = m * (off % ATT_TQ)
            kaug[:, hh, 6 + i] = m * (off - off % ATT_TQ)
    return jnp.asarray(kaug.reshape(n_rows, DIFF_HEADS * LANES), BF16)


def _attn_tables(tq, tk):
    slopes = _alibi_slopes()
    l3 = _bf16_split3(LOG2E)
    il = np.arange(tq, dtype=np.float64)
    qaug = np.zeros((DIFF_HEADS, tq, LANES), np.float32)
    for hh, m in enumerate(slopes):
        for i in range(3):
            qaug[hh, :, i] = -m * il
            qaug[hh, :, 3 + i] = np.float32(l3[i])
            qaug[hh, :, 6 + i] = np.float32(l3[i])
    j = np.arange(tk)[None, :, None]
    i = np.arange(tq)[None, None, :] + tq * np.arange(tk // tq)[:, None, None]
    allowed = (j // CHUNK) <= (i // CHUNK)
    corr = LOG2E * slopes[:, None, None, None] * ((i - j) - np.abs(i - j))[None]
    tdiag = np.where(allowed[None], corr, -np.inf)
    return (jnp.asarray(slopes * LOG2E, F32), jnp.asarray(qaug, BF16), jnp.asarray(tdiag, F32))


def _diff_attention(main, kp, vt, lam_p, subln_g, batch, seq):
    tq, tk, ns = ATT_TQ, ATT_TK, ATT_NS
    tsup = tq * ns
    nq = seq // tsup
    slopes, qaug, tdiag = _attn_tables(tq, tk)
    return pl.pallas_call(
        functools.partial(_attn_kernel, tq=tq, tk=tk, ns=ns),
        grid=(batch, DIFF_HEADS, nq),
        in_specs=[
            pl.BlockSpec(memory_space=pltpu.SMEM),
            pl.BlockSpec((tsup, LANES), lambda b, h, qi: (b * nq + qi, OFF_DQ // LANES + h)),
            pl.BlockSpec((1, tq, LANES), lambda b, h, qi: (h, 0, 0)),
            pl.BlockSpec((seq, 2 * LANES), lambda b, h, qi: (b, h)),
            pl.BlockSpec((VT_ROWS, seq), lambda b, h, qi: (h, b)),
            pl.BlockSpec((1,) + tdiag.shape[1:], lambda b, h, qi: (h, 0, 0, 0)),
            pl.BlockSpec(lam_p.shape, lambda b, h, qi: (0, 0)),
            pl.BlockSpec((1, LANES), lambda b, h, qi: (0, h)),
        ],
        out_specs=pl.BlockSpec((tsup, LANES), lambda b, h, qi: (b * nq + qi, h)),
        out_shape=jax.ShapeDtypeStruct((batch * seq, DIFF_HEADS * DIFF_V_DIM), BF16),
        scratch_shapes=[
            pltpu.VMEM((ns, 2 * tq, 2 * LANES), BF16),
            pltpu.VMEM((ns, 1, 2 * tq), F32),
            pltpu.VMEM((ns, VT_ROWS, 2 * tq), F32),
            pltpu.VMEM((ns, 2, tk, 2 * tq), F32),
            pltpu.VMEM((ns, 2, 1, 2 * tq), F32),
            pltpu.VMEM((ns, 2, tk, 2 * tq), BF16),
            pltpu.VMEM((ns, 2, 1, 2 * tq), F32),
        ],
        compiler_params=_cparams(("arbitrary", "arbitrary", "arbitrary")),
        name="diff_attn",
    )(slopes, main, qaug, kp, vt, tdiag, lam_p, subln_g)


def _ret_kernel(q_ref, k_ref, v_ref, g_ref, dtab_ref, qdec_ref, kdec_ref, cdec_ref, gn_ref, o_ref,
                state_scr, *, c):
    @pl.when(pl.program_id(1) == 0)
    def _():
        state_scr[...] = jnp.zeros(state_scr.shape, F32)

    lane = lax.broadcasted_iota(jnp.int32, (c, LANES), 1)
    for p in range(RET_HEADS // 2):
        qp = q_ref[:, p * LANES:(p + 1) * LANES]
        kp = k_ref[:, p * LANES:(p + 1) * LANES]
        vp = v_ref[:, p * 2 * RET_V_DIM:(p + 1) * 2 * RET_V_DIM]
        state = state_scr[p]
        state_b = state.astype(BF16)
        qdec = qdec_ref[:, p * LANES:(p + 1) * LANES]
        for e in range(2):
            head = 2 * p + e
            keep = (lane < RET_QK_DIM) if e == 0 else (lane >= RET_QK_DIM)
            qm = jnp.where(keep, qp, jnp.zeros_like(qp))
            s = lax.dot_general(qm, kp, (((1,), (1,)), ((), ())), preferred_element_type=F32)
            s = s * dtab_ref[head]
            ve = vp[:, e * RET_V_DIM:(e + 1) * RET_V_DIM]
            intra = jnp.dot(s.astype(BF16), ve, preferred_element_type=F32)
            qd = (qm.astype(F32) * qdec).astype(BF16)
            cross = jnp.dot(qd, state_b[:, e * RET_V_DIM:(e + 1) * RET_V_DIM],
                            preferred_element_type=F32)
            o = intra + cross
            mu = jnp.mean(o, axis=-1, keepdims=True)
            d = o - mu
            var = jnp.mean(d * d, axis=-1, keepdims=True)
            cols = slice(head * RET_V_DIM, (head + 1) * RET_V_DIM)
            gate = g_ref[:, cols].astype(F32)
            y = d * lax.rsqrt(var + LN_EPS) * gn_ref[:, cols] * (gate * jax.nn.sigmoid(gate))
            o_ref[:, cols] = y.astype(o_ref.dtype)
        kd = (kp.astype(F32) * kdec_ref[:, p * LANES:(p + 1) * LANES]).astype(BF16)
        kv = lax.dot_general(kd, vp, (((0,), (0,)), ((), ())), preferred_element_type=F32)
        state_scr[p] = state * cdec_ref[p] + kv


def _ret_tables(c):
    hh = np.arange(RET_HEADS, dtype=np.float64)
    lg = np.log(1.0 - 2.0 ** (-5.0 - hh))
    idx = np.arange(c, dtype=np.float64)
    nm = idx[:, None] - idx[None, :]
    dtab = np.where(nm >= 0, np.exp(lg[:, None, None] * np.maximum(nm, 0.0)), 0.0) * RET_QK_DIM ** -0.5
    qdec = np.exp(lg[None, :] * (idx + 1.0)[:, None])
    kdec = np.exp(lg[None, :] * (c - 1.0 - idx)[:, None]) * RET_QK_DIM ** -0.5
    qdec = np.repeat(qdec, RET_QK_DIM, axis=1)
    kdec = np.repeat(kdec, RET_QK_DIM, axis=1)
    cdec = np.repeat(np.exp(lg * c), RET_QK_DIM).reshape(RET_HEADS // 2, 2 * RET_QK_DIM, 1)
    return tuple(jnp.asarray(a, F32) for a in (dtab, qdec, kdec, cdec))


def _retention(main, ret_g, batch, seq):
    c = RET_C
    nc = seq // c
    dtab, qdec, kdec, cdec = _ret_tables(c)
    qkw = RET_HEADS * RET_QK_DIM
    vw = RET_HEADS * RET_V_DIM
    return pl.pallas_call(
        functools.partial(_ret_kernel, c=c),
        grid=(batch, nc),
        in_specs=[
            pl.BlockSpec((c, qkw), lambda b, i: (b * nc + i, OFF_RQ // qkw)),
            pl.BlockSpec((c, qkw), lambda b, i: (b * nc + i, OFF_RK // qkw)),
            pl.BlockSpec((c, vw), lambda b, i: (b * nc + i, OFF_RV // vw)),
            pl.BlockSpec((c, vw), lambda b, i: (b * nc + i, OFF_RG // vw)),
            _resident(dtab.shape), _resident(qdec.shape), _resident(kdec.shape), _resident(cdec.shape),
            _resident(ret_g.shape),
        ],
        out_specs=pl.BlockSpec((c, vw), lambda b, i: (b * nc + i, 0)),
        out_shape=jax.ShapeDtypeStruct((batch * seq, vw), BF16),
        scratch_shapes=[pltpu.VMEM((RET_HEADS // 2, 2 * RET_QK_DIM, 2 * RET_V_DIM), F32)],
        compiler_params=_cparams(("arbitrary", "arbitrary")),
        name="retention",
    )(main, main, main, main, dtab, qdec, kdec, cdec, ret_g)


def _merge_kernel(x_ref, a_ref, r_ref, mq_ref, mk_ref, mv_ref, wg_ref, bg_ref, wd_ref, wr_ref, wm_ref,
                  wo_ref, g1_ref, b1_ref, o_ref, m_scr, merged_scr, *, nc):
    d = x_ref.shape[1]
    x = x_ref[...]
    xb = x.astype(BF16)

    scale = jnp.asarray(MEM_HEAD_DIM ** -0.5, BF16)
    for h in range(MEM_HEADS):
        cols = slice(h * MEM_HEAD_DIM, (h + 1) * MEM_HEAD_DIM)
        qh = mq_ref[:, cols] * scale
        s = lax.dot_general(qh, mk_ref[0, :, cols], (((1,), (1,)), ((), ())),
                            preferred_element_type=F32)
        p = jnp.exp(s - jnp.max(s, axis=-1, keepdims=True))
        l = jnp.sum(p, axis=-1, keepdims=True)
        mh = jnp.dot(p.astype(BF16), mv_ref[0, :, cols], preferred_element_type=F32) / l
        m_scr[:, cols] = mh.astype(BF16)

    a = a_ref[...]
    r = r_ref[...]
    m = m_scr[...]
    for c in range(0, d, nc):
        cs = slice(c, c + nc)
        acc = None
        for br, (val, w_ref) in enumerate(((a, wd_ref), (r, wr_ref), (m, wm_ref))):
            gs = slice(br * d + c, br * d + c + nc)
            gate = jax.nn.sigmoid(jnp.dot(xb, wg_ref[:, gs], preferred_element_type=F32) + bg_ref[:, gs])
            term = gate * jnp.dot(val, w_ref[:, cs], preferred_element_type=F32)
            acc = term if acc is None else acc + term
        merged_scr[:, cs] = acc.astype(BF16)

    mix = jnp.dot(merged_scr[...], wo_ref[...], preferred_element_type=F32)
    o_ref[...] = _layer_norm(DEEPNORM_ALPHA * x + mix, g1_ref[...], b1_ref[...])


def _merge(x2, a, r, main, memkv, wg, bg, wd, wr, wm, wo, g1, b1, batch, seq):
    tm = MERGE_TM
    t, d = x2.shape
    nt = seq // tm
    mlen = memkv.shape[1]
    mw = MEM_HEADS * MEM_HEAD_DIM
    row = lambda i: (i, 0)
    return pl.pallas_call(
        functools.partial(_merge_kernel, nc=MERGE_NC),
        grid=(t // tm,),
        in_specs=[
            pl.BlockSpec((tm, d), row),
            pl.BlockSpec((tm, a.shape[1]), row),
            pl.BlockSpec((tm, r.shape[1]), row),
            pl.BlockSpec((tm, mw), lambda i: (i, OFF_MQ // mw)),
            pl.BlockSpec((1, mlen, mw), lambda i: (i // nt, 0, 0)),
            pl.BlockSpec((1, mlen, mw), lambda i: (i // nt, 0, 1)),
            _resident(wg.shape), _resident(bg.shape), _resident(wd.shape), _resident(wr.shape),
            _resident(wm.shape), _resident(wo.shape), _resident(g1.shape), _resident(b1.shape),
        ],
        out_specs=pl.BlockSpec((tm, d), row),
        out_shape=jax.ShapeDtypeStruct((t, d), F32),
        scratch_shapes=[pltpu.VMEM((tm, mw), BF16), pltpu.VMEM((tm, d), BF16)],
        compiler_params=_cparams(("arbitrary",)),
        name="merge",
    )(x2, a, r, main, memkv, memkv, wg, bg, wd, wr, wm, wo, g1, b1)


def _ffn_kernel(h_ref, wu_ref, cw_ref, cb_ref, wdn_ref, g2_ref, b2_ref, o_ref,
                carry_scr, hs_scr, act_scr, *, fc, rb, ffn_dim, tiles_per_seq):
    tm = h_ref.shape[0]
    halo = SUBLANES

    @pl.when(pl.program_id(0) % tiles_per_seq == 0)
    def _():
        carry_scr[...] = jnp.zeros(carry_scr.shape, F32)

    x = h_ref[...]
    xb = x.astype(BF16)

    def up_project(slot, col):
        cs = slice(col, col + fc)
        up = jnp.dot(xb, wu_ref[:, cs], preferred_element_type=F32)
        hs_scr[slot, 0:halo, :] = carry_scr[:, cs]
        hs_scr[slot, halo:halo + tm, :] = up
        carry_scr[:, cs] = up[tm - halo:tm, :]

    def conv_rows(slot, col, r0):
        cs = slice(col, col + fc)
        blk = hs_scr[slot, r0:r0 + rb + halo, :]
        w = cw_ref[:, cs]
        y = pltpu.roll(blk, 2, 0) * w[0:1] + pltpu.roll(blk, 1, 0) * w[1:2] + blk * w[2:3] + cb_ref[:, cs]
        return y[halo:, :]

    for j, c in enumerate(range(0, ffn_dim, fc)):
        sg, sv = 2 * (j % 2), 2 * (j % 2) + 1
        up_project(sg, c)
        up_project(sv, ffn_dim + c)
        for r0 in range(0, tm, rb):
            g = conv_rows(sg, c, r0)
            val = conv_rows(sv, ffn_dim + c, r0)
            act_scr[r0:r0 + rb, c:c + fc] = (g * jax.nn.sigmoid(g) * val).astype(BF16)

    ffn = jnp.dot(act_scr[...], wdn_ref[...], preferred_element_type=F32)
    o_ref[...] = _layer_norm(DEEPNORM_ALPHA * x + ffn, g2_ref[...], b2_ref[...])


def _ffn(h1, wu, cw, cb, wdn, g2, b2, seq):
    tm = FFN_TM
    t, d = h1.shape
    ffn_dim = wdn.shape[0]
    row = lambda i: (i, 0)
    return pl.pallas_call(
        functools.partial(_ffn_kernel, fc=FFN_FC, rb=FFN_RB, ffn_dim=ffn_dim, tiles_per_seq=seq // tm),
        grid=(t // tm,),
        in_specs=[
            pl.BlockSpec((tm, d), row),
            _resident(wu.shape), _resident(cw.shape), _resident(cb.shape), _resident(wdn.shape),
            _resident(g2.shape), _resident(b2.shape),
        ],
        out_specs=pl.BlockSpec((tm, d), row),
        out_shape=jax.ShapeDtypeStruct((t, d), F32),
        scratch_shapes=[
            pltpu.VMEM((SUBLANES, 2 * ffn_dim), F32),
            pltpu.VMEM((4, tm + SUBLANES, FFN_FC), F32),
            pltpu.VMEM((tm, ffn_dim), BF16),
        ],
        compiler_params=_cparams(("arbitrary",)),
        name="conv_ffn",
    )(h1, wu, cw, cb, wdn, g2, b2)


def kernel(x, mem, w_in, diff_lambda, diff_subln_g, ret_norm_g, w_mem_kv, w_diff_o, w_ret_o, w_mem_o,
           w_gate, b_gate, w_mix_out, ln1_g, ln1_b, w_up, conv_w, conv_b, w_down, ln2_g, ln2_b):
    batch, seq, d = x.shape
    mlen = mem.shape[1]
    assert w_in.shape[0] == DEPTH == 1
    assert seq % PROJ_TM == 0 and seq % RET_C == 0 and seq % FFN_TM == 0 and seq % MERGE_TM == 0
    assert PROJ_TM % ATT_TK == 0 and ATT_TQ % CHUNK == 0 and ATT_TK % ATT_TQ == 0 and seq % (ATT_TQ * ATT_NS) == 0
    assert ATT_UNROLL % 2 == 0 and (ATT_NS * ATT_TQ // ATT_TK) % ATT_UNROLL == 0
    l = 0
    row = lambda v: v[l].reshape(1, -1)

    x2 = x.reshape(batch * seq, d)
    main, kp, vt = _inproj(x2, w_in[l])
    memkv = _proj(mem.reshape(batch * mlen, d), w_mem_kv[l].astype(BF16), batch * mlen, PROJ_NC)
    memkv = memkv.reshape(batch, mlen, 2 * MEM_HEADS * MEM_HEAD_DIM)

    a = _diff_attention(main, kp, vt, diff_lambda[l], row(diff_subln_g), batch, seq)
    r = _retention(main, row(ret_norm_g), batch, seq)
    h1 = _merge(x2, a, r, main, memkv, w_gate[l].astype(BF16), row(b_gate),
                w_diff_o[l].astype(BF16), w_ret_o[l].astype(BF16), w_mem_o[l].astype(BF16),
                w_mix_out[l].astype(BF16), row(ln1_g), row(ln1_b), batch, seq)
    out = _ffn(h1, w_up[l].astype(BF16), conv_w[l], row(conv_b), w_down[l].astype(BF16),
               row(ln2_g), row(ln2_b), seq)
    return out.reshape(batch, seq, d)
```

```python
import functools
import math

import numpy as np
import jax
import jax.numpy as jnp
from jax import lax
from jax.experimental import pallas as pl
from jax.experimental.pallas import tpu as pltpu

F32 = jnp.float32
BF16 = jnp.bfloat16

CHUNK = 64
DIFF_HEADS = 4
DIFF_QK_DIM = 64
DIFF_V_DIM = 128
RET_HEADS = 4
RET_QK_DIM = 64
RET_V_DIM = 128
MEM_HEADS = 4
MEM_HEAD_DIM = 128
N_BRANCHES = 3
CONV_WIDTH = 3
LN_EPS = 1e-5
DEPTH = 1
DEEPNORM_ALPHA = (2.0 * DEPTH) ** 0.25
LAMBDA_INIT = 0.8 - 0.6 * math.exp(-0.3 * 0)
LOG2E = math.log2(math.e)

IN_DQ, IN_DK, IN_DV = 0, 512, 1024
IN_RQ, IN_RK, IN_RV, IN_RG, IN_MQ, IN_END = 1536, 1792, 2048, 2560, 3072, 3584
OFF_DQ, OFF_RQ, OFF_RK, OFF_RV, OFF_RG, OFF_MQ, MAIN_W = 0, 512, 768, 1024, 1536, 2048, 2560

LANES = 128
SUBLANES = 8
BF16_ROWS = 16
VMEM_LIMIT = 56 * 1024 * 1024

PROJ_TM = 512
PROJ_NC = 512
ATT_TQ = 256
ATT_TK = 512
ATT_NS = 4
ATT_UNROLL = 2
ATT_UNDERFLOW_LOG2 = 150.0
VT_ROWS = DIFF_V_DIM + BF16_ROWS
RET_C = 256
MERGE_TM = 512
MERGE_NC = 512
FFN_TM = 512
FFN_FC = 256
FFN_RB = 128


def _cparams(sem):
    return pltpu.CompilerParams(dimension_semantics=sem, vmem_limit_bytes=VMEM_LIMIT)


def _resident(shape):
    nd = len(shape)
    return pl.BlockSpec(shape, lambda *_: (0,) * nd, pipeline_mode=pl.Buffered(1))


def _layer_norm(y, g, b):
    mu = jnp.mean(y, axis=-1, keepdims=True)
    d = y - mu
    var = jnp.mean(d * d, axis=-1, keepdims=True)
    return d * lax.rsqrt(var + LN_EPS) * g + b


def _alibi_slopes():
    return np.array([(2.0 ** (-8.0 / DIFF_HEADS)) ** (i + 1) for i in range(DIFF_HEADS)], np.float64)


def _bf16_split3(v):
    parts = []
    rem = np.float64(v)
    for _ in range(3):
        p = np.asarray(rem, np.float32).astype(jnp.bfloat16)
        parts.append(p)
        rem = rem - np.float64(p.astype(np.float32))
    return parts


def _proj_kernel(x_ref, w_ref, o_ref, *, nc):
    xb = x_ref[...].astype(BF16)
    n = w_ref.shape[1]
    for c in range(0, n, nc):
        o_ref[:, c:c + nc] = jnp.dot(xb, w_ref[:, c:c + nc],
                                     preferred_element_type=F32).astype(o_ref.dtype)


def _proj(x2, w, tm, nc):
    t, d = x2.shape
    n = w.shape[1]
    return pl.pallas_call(
        functools.partial(_proj_kernel, nc=nc),
        grid=(t // tm,),
        in_specs=[pl.BlockSpec((tm, d), lambda i: (i, 0)), _resident((d, n))],
        out_specs=pl.BlockSpec((tm, n), lambda i: (i, 0)),
        out_shape=jax.ShapeDtypeStruct((t, n), BF16),
        compiler_params=_cparams(("arbitrary",)),
        name="memkv",
    )(x2, w)


def _inproj_kernel(x_ref, wm_ref, wk_ref, wvt_ref, kaug_ref, sel_ref, main_ref, kp_ref, vt_ref, nrm_ref, *, nc):
    xb = x_ref[...].astype(BF16)
    qscale = LOG2E * DIFF_QK_DIM ** -0.5
    for c in range(0, MAIN_W, nc):
        acc = jnp.dot(xb, wm_ref[:, c:c + nc], preferred_element_type=F32)
        if c < OFF_RQ:
            acc = acc * qscale
        main_ref[:, c:c + nc] = acc.astype(main_ref.dtype)

    k = jnp.dot(xb, wk_ref[...], preferred_element_type=F32).astype(kp_ref.dtype)
    for h in range(DIFF_HEADS):
        kp_ref[:, h * 2 * LANES:h * 2 * LANES + LANES] = k[:, h * LANES:(h + 1) * LANES]
        kp_ref[:, h * 2 * LANES + LANES:(h + 1) * 2 * LANES] = kaug_ref[:, h * LANES:(h + 1) * LANES]

    def max_sq_norm(v):
        vf = v.astype(F32)
        sq = jnp.dot((vf * vf).astype(BF16), sel_ref[...], preferred_element_type=F32)
        return jnp.broadcast_to(jnp.max(sq, axis=0, keepdims=True), (SUBLANES // 2, LANES))

    nrm_ref[0] = jnp.concatenate([max_sq_norm(main_ref[:, OFF_DQ:OFF_RQ]), max_sq_norm(k)], axis=0)

    vt = lax.dot_general(wvt_ref[...], xb, (((1,), (1,)), ((), ())), preferred_element_type=F32)
    row = lax.broadcasted_iota(jnp.int32, vt.shape, 0)
    ones_row = row == DIFF_V_DIM
    for h in range(1, DIFF_HEADS):
        ones_row = ones_row | (row == h * VT_ROWS + DIFF_V_DIM)
    vt_ref[...] = jnp.where(ones_row, 1.0, vt).astype(vt_ref.dtype)


def _inproj(x2, w_in):
    t, d = x2.shape
    tm = PROJ_TM
    cols = lambda a, b: w_in[:, a:b]
    w_main = jnp.concatenate([cols(IN_DQ, IN_DK), cols(IN_RQ, IN_END)], axis=1).astype(BF16)
    w_k = cols(IN_DK, IN_DV).astype(BF16)
    wv = cols(IN_DV, IN_RQ).astype(BF16).T.reshape(DIFF_HEADS, DIFF_V_DIM, d)
    w_vt = jnp.pad(wv, ((0, 0), (0, VT_ROWS - DIFF_V_DIM), (0, 0))).reshape(DIFF_HEADS * VT_ROWS, d)

    kaug = _alibi_key_columns(tm, ATT_TK)
    sel = (np.arange(DIFF_HEADS * 2 * DIFF_QK_DIM)[:, None] // DIFF_QK_DIM == np.arange(LANES)[None, :])
    sel = jnp.asarray(sel, BF16)

    row = lambda i: (i, 0)
    return pl.pallas_call(
        functools.partial(_inproj_kernel, nc=PROJ_NC),
        grid=(t // tm,),
        in_specs=[pl.BlockSpec((tm, d), row), _resident(w_main.shape), _resident(w_k.shape),
                  _resident(w_vt.shape), _resident(kaug.shape), _resident(sel.shape)],
        out_specs=[pl.BlockSpec((tm, MAIN_W), row),
                   pl.BlockSpec((tm, DIFF_HEADS * 2 * LANES), row),
                   pl.BlockSpec((DIFF_HEADS * VT_ROWS, tm), lambda i: (0, i)),
                   pl.BlockSpec((1, SUBLANES, LANES), lambda i: (i, 0, 0))],
        out_shape=[jax.ShapeDtypeStruct((t, MAIN_W), BF16),
                   jax.ShapeDtypeStruct((t, DIFF_HEADS * 2 * LANES), BF16),
                   jax.ShapeDtypeStruct((DIFF_HEADS * VT_ROWS, t), BF16),
                   jax.ShapeDtypeStruct((t // tm, SUBLANES, LANES), F32)],
        compiler_params=_cparams(("arbitrary",)),
        name="inproj",
    )(x2, w_main, w_k, w_vt, kaug, sel)


def _attn_kernel(slope_ref, first_ref, q_ref, qaug_ref, kp_ref, vt_ref, tdiag_ref, lam_ref, g_ref, o_ref,
                 qs_scr, m_scr, acc_scr, z_scr, zmax_scr, p_scr, alpha_scr, *, tq, tk, ns):
    b, h, qi = pl.program_id(0), pl.program_id(1), pl.program_id(2)
    per_tile = tk // tq
    n0 = (ns // per_tile) * qi
    first = first_ref[b, h, qi]
    slope = slope_ref[h]

    @pl.when((b == 0) & (h == 0) & (qi == 0))
    def _():
        p_scr[...] = jnp.zeros(p_scr.shape, BF16)

    qa = qaug_ref[0]
    for j in range(ns):
        q = q_ref[j * tq:(j + 1) * tq, :]
        lane = lax.broadcasted_iota(jnp.int32, q.shape, 1)
        zero = jnp.zeros_like(q)
        qs_scr[j, 0:tq, 0:LANES] = jnp.where(lane < DIFF_QK_DIM, q, zero)
        qs_scr[j, tq:2 * tq, 0:LANES] = jnp.where(lane >= DIFF_QK_DIM, q, zero)
        qs_scr[j, 0:tq, LANES:2 * LANES] = qa
        qs_scr[j, tq:2 * tq, LANES:2 * LANES] = qa
        m_scr[j] = jnp.full(m_scr.shape[1:], -jnp.inf, F32)
        acc_scr[j] = jnp.zeros(acc_scr.shape[1:], F32)
        alpha_scr[j, 1] = jnp.ones(alpha_scr.shape[2:], F32)

    def score_stage(j, kb, slot):
        start = pl.multiple_of(kb * tk, tk)
        z = lax.dot_general(kp_ref[pl.ds(start, tk), :], qs_scr[j], (((1,), (1,)), ((), ())),
                            preferred_element_type=F32)
        z_scr[j, slot] = z
        zmax_scr[j, slot] = jnp.max(z, axis=0, keepdims=True)

    def softmax_stage(j, z, zmax, c, slot):
        m_prev = m_scr[j]
        m_new = jnp.maximum(m_prev, zmax + c)
        p_scr[j, slot] = jnp.exp2(z - (m_new - c)).astype(BF16)
        alpha_scr[j, slot] = jnp.exp2(m_prev - m_new)
        m_scr[j] = m_new

    def value_stage(j, kb, slot):
        start = pl.multiple_of(jnp.maximum(kb, first) * tk, tk)
        v = vt_ref[:, pl.ds(start, tk)]
        v = jnp.where(kb >= first, v, jnp.zeros_like(v))
        acc_scr[j] = alpha_scr[j, slot] * acc_scr[j] + jnp.dot(v, p_scr[j, slot], preferred_element_type=F32)

    def tile_bias(j, kb):
        return -slope * ((n0 - kb) * tk + j * tq).astype(F32)

    def full_trip(kb, cur, nxt):
        for j in range(ns):
            score_stage(j, kb + 1, nxt)
            softmax_stage(j, z_scr[j, cur], zmax_scr[j, cur], tile_bias(j, kb), cur)
            value_stage(j, kb - 1, nxt)

    for j in range(ns):
        score_stage(j, first, 0)

    def unrolled(i, carry):
        for u in range(ATT_UNROLL):
            full_trip(ATT_UNROLL * i + u, u % 2, 1 - u % 2)
        return carry

    lax.fori_loop(first // ATT_UNROLL, n0 // ATT_UNROLL, unrolled, 0)

    for i in range(ns // per_tile + 1):
        kb = n0 + i
        cur, nxt = i % 2, 1 - i % 2
        for j in range(ns):
            last = j // per_tile
            if i < last:
                score_stage(j, kb + 1, nxt)
                softmax_stage(j, z_scr[j, cur], zmax_scr[j, cur], tile_bias(j, kb), cur)
            elif i == last:
                td = tdiag_ref[0, j % per_tile]
                z = z_scr[j, cur]
                z = jnp.concatenate([z[:, 0:tq] + td, z[:, tq:2 * tq] + td], axis=1)
                softmax_stage(j, z, jnp.max(z, axis=0, keepdims=True), tile_bias(j, kb), cur)
            if i <= last + 1:
                value_stage(j, kb - 1, nxt)

    lp = lam_ref[...]
    lam = (jnp.exp(jnp.sum(lp[0:1] * lp[1:2], axis=-1, keepdims=True))
           - jnp.exp(jnp.sum(lp[2:3] * lp[3:4], axis=-1, keepdims=True)) + LAMBDA_INIT)
    for j in range(ns):
        acc = acc_scr[j]
        o = acc[0:DIFF_V_DIM, :] / acc[DIFF_V_DIM:DIFF_V_DIM + 1, :]
        y = o[:, 0:tq] - lam * o[:, tq:2 * tq]
        y = y * lax.rsqrt(jnp.mean(y * y, axis=0, keepdims=True) + LN_EPS)
        o_ref[j * tq:(j + 1) * tq, :] = (y.T * g_ref[...] * (1.0 - LAMBDA_INIT)).astype(o_ref.dtype)


def _alibi_key_columns(n_rows, tk):
    l3 = _bf16_split3(LOG2E)
    off = np.arange(n_rows) % tk
    kaug = np.zeros((n_rows, DIFF_HEADS, LANES), np.float32)
    for hh, m in enumerate(_alibi_slopes()):
        for i in range(3):
            kaug[:, hh, i] = np.float32(l3[i])
            kaug[:, hh, 3 + i] = m * (off % ATT_TQ)
            kaug[:, hh, 6 + i] = m * (off - off % ATT_TQ)
    return jnp.asarray(kaug.reshape(n_rows, DIFF_HEADS * LANES), BF16)


def _attn_tables(tq, tk):
    slopes = _alibi_slopes()
    l3 = _bf16_split3(LOG2E)
    il = np.arange(tq, dtype=np.float64)
    qaug = np.zeros((DIFF_HEADS, tq, LANES), np.float32)
    for hh, m in enumerate(slopes):
        for i in range(3):
            qaug[hh, :, i] = -m * il
            qaug[hh, :, 3 + i] = np.float32(l3[i])
            qaug[hh, :, 6 + i] = np.float32(l3[i])
    j = np.arange(tk)[None, :, None]
    i = np.arange(tq)[None, None, :] + tq * np.arange(tk // tq)[:, None, None]
    allowed = (j // CHUNK) <= (i // CHUNK)
    corr = LOG2E * slopes[:, None, None, None] * ((i - j) - np.abs(i - j))[None]
    tdiag = np.where(allowed[None], corr, -np.inf)
    return (jnp.asarray(slopes * LOG2E, F32), jnp.asarray(qaug, BF16), jnp.asarray(tdiag, F32))


def _first_key_tiles(nrm, batch, seq):
    tiles_per_seq = seq // PROJ_TM
    tsup = ATT_TQ * ATT_NS
    n = nrm.reshape(batch, tiles_per_seq, 2, SUBLANES // 2, LANES)[:, :, :, 0, :DIFF_HEADS * 2]
    n = n.reshape(batch, tiles_per_seq, 2, DIFF_HEADS, 2).max(axis=-1)
    q2 = n[:, :, 0].reshape(batch, seq // tsup, tsup // PROJ_TM, DIFF_HEADS).max(axis=2)
    k2 = n[:, :, 1].max(axis=1)
    bound = jnp.sqrt(q2 * k2[:, None, :]) * 1.02 + 0.05
    slope = jnp.asarray(_alibi_slopes() * LOG2E, F32)
    reach = (2.0 * bound + ATT_UNDERFLOW_LOG2) / slope
    needed = jnp.floor(jnp.maximum(reach - 1.0, 0.0) / ATT_TK).astype(jnp.int32) + 1
    n0 = (tsup // ATT_TK) * jnp.arange(seq // tsup, dtype=jnp.int32)[None, :, None]
    first = jnp.maximum(n0 - needed, 0) // ATT_UNROLL * ATT_UNROLL
    return first.transpose(0, 2, 1)


def _diff_attention(main, kp, vt, nrm, lam_p, subln_g, batch, seq):
    tq, tk, ns = ATT_TQ, ATT_TK, ATT_NS
    tsup = tq * ns
    nq = seq // tsup
    slopes, qaug, tdiag = _attn_tables(tq, tk)
    first = _first_key_tiles(nrm, batch, seq)
    return pl.pallas_call(
        functools.partial(_attn_kernel, tq=tq, tk=tk, ns=ns),
        grid=(batch, DIFF_HEADS, nq),
        in_specs=[
            pl.BlockSpec(memory_space=pltpu.SMEM),
            pl.BlockSpec(memory_space=pltpu.SMEM),
            pl.BlockSpec((tsup, LANES), lambda b, h, qi: (b * nq + qi, OFF_DQ // LANES + h)),
            pl.BlockSpec((1, tq, LANES), lambda b, h, qi: (h, 0, 0)),
            pl.BlockSpec((seq, 2 * LANES), lambda b, h, qi: (b, h)),
            pl.BlockSpec((VT_ROWS, seq), lambda b, h, qi: (h, b)),
            pl.BlockSpec((1,) + tdiag.shape[1:], lambda b, h, qi: (h, 0, 0, 0)),
            pl.BlockSpec(lam_p.shape, lambda b, h, qi: (0, 0)),
            pl.BlockSpec((1, LANES), lambda b, h, qi: (0, h)),
        ],
        out_specs=pl.BlockSpec((tsup, LANES), lambda b, h, qi: (b * nq + qi, h)),
        out_shape=jax.ShapeDtypeStruct((batch * seq, DIFF_HEADS * DIFF_V_DIM), BF16),
        scratch_shapes=[
            pltpu.VMEM((ns, 2 * tq, 2 * LANES), BF16),
            pltpu.VMEM((ns, 1, 2 * tq), F32),
            pltpu.VMEM((ns, VT_ROWS, 2 * tq), F32),
            pltpu.VMEM((ns, 2, tk, 2 * tq), F32),
            pltpu.VMEM((ns, 2, 1, 2 * tq), F32),
            pltpu.VMEM((ns, 2, tk, 2 * tq), BF16),
            pltpu.VMEM((ns, 2, 1, 2 * tq), F32),
        ],
        compiler_params=_cparams(("arbitrary", "arbitrary", "arbitrary")),
        name="diff_attn",
    )(slopes, first, main, qaug, kp, vt, tdiag, lam_p, subln_g)


def _ret_kernel(q_ref, k_ref, v_ref, g_ref, dtab_ref, qdec_ref, kdec_ref, cdec_ref, gn_ref, o_ref,
                state_scr, *, c):
    @pl.when(pl.program_id(1) == 0)
    def _():
        state_scr[...] = jnp.zeros(state_scr.shape, F32)

    lane = lax.broadcasted_iota(jnp.int32, (c, LANES), 1)
    for p in range(RET_HEADS // 2):
        qp = q_ref[:, p * LANES:(p + 1) * LANES]
        kp = k_ref[:, p * LANES:(p + 1) * LANES]
        vp = v_ref[:, p * 2 * RET_V_DIM:(p + 1) * 2 * RET_V_DIM]
        state = state_scr[p]
        state_b = state.astype(BF16)
        qdec = qdec_ref[:, p * LANES:(p + 1) * LANES]
        for e in range(2):
            head = 2 * p + e
            keep = (lane < RET_QK_DIM) if e == 0 else (lane >= RET_QK_DIM)
            qm = jnp.where(keep, qp, jnp.zeros_like(qp))
            s = lax.dot_general(qm, kp, (((1,), (1,)), ((), ())), preferred_element_type=F32)
            s = s * dtab_ref[head]
            ve = vp[:, e * RET_V_DIM:(e + 1) * RET_V_DIM]
            intra = jnp.dot(s.astype(BF16), ve, preferred_element_type=F32)
            qd = (qm.astype(F32) * qdec).astype(BF16)
            cross = jnp.dot(qd, state_b[:, e * RET_V_DIM:(e + 1) * RET_V_DIM],
                            preferred_element_type=F32)
            o = intra + cross
            mu = jnp.mean(o, axis=-1, keepdims=True)
            d = o - mu
            var = jnp.mean(d * d, axis=-1, keepdims=True)
            cols = slice(head * RET_V_DIM, (head + 1) * RET_V_DIM)
            gate = g_ref[:, cols].astype(F32)
            y = d * lax.rsqrt(var + LN_EPS) * gn_ref[:, cols] * (gate * jax.nn.sigmoid(gate))
            o_ref[:, cols] = y.astype(o_ref.dtype)
        kd = (kp.astype(F32) * kdec_ref[:, p * LANES:(p + 1) * LANES]).astype(BF16)
        kv = lax.dot_general(kd, vp, (((0,), (0,)), ((), ())), preferred_element_type=F32)
        state_scr[p] = state * cdec_ref[p] + kv


def _ret_tables(c):
    hh = np.arange(RET_HEADS, dtype=np.float64)
    lg = np.log(1.0 - 2.0 ** (-5.0 - hh))
    idx = np.arange(c, dtype=np.float64)
    nm = idx[:, None] - idx[None, :]
    dtab = np.where(nm >= 0, np.exp(lg[:, None, None] * np.maximum(nm, 0.0)), 0.0) * RET_QK_DIM ** -0.5
    qdec = np.exp(lg[None, :] * (idx + 1.0)[:, None])
    kdec = np.exp(lg[None, :] * (c - 1.0 - idx)[:, None]) * RET_QK_DIM ** -0.5
    qdec = np.repeat(qdec, RET_QK_DIM, axis=1)
    kdec = np.repeat(kdec, RET_QK_DIM, axis=1)
    cdec = np.repeat(np.exp(lg * c), RET_QK_DIM).reshape(RET_HEADS // 2, 2 * RET_QK_DIM, 1)
    return tuple(jnp.asarray(a, F32) for a in (dtab, qdec, kdec, cdec))


def _retention(main, ret_g, batch, seq):
    c = RET_C
    nc = seq // c
    dtab, qdec, kdec, cdec = _ret_tables(c)
    qkw = RET_HEADS * RET_QK_DIM
    vw = RET_HEADS * RET_V_DIM
    return pl.pallas_call(
        functools.partial(_ret_kernel, c=c),
        grid=(batch, nc),
        in_specs=[
            pl.BlockSpec((c, qkw), lambda b, i: (b * nc + i, OFF_RQ // qkw)),
            pl.BlockSpec((c, qkw), lambda b, i: (b * nc + i, OFF_RK // qkw)),
            pl.BlockSpec((c, vw), lambda b, i: (b * nc + i, OFF_RV // vw)),
            pl.BlockSpec((c, vw), lambda b, i: (b * nc + i, OFF_RG // vw)),
            _resident(dtab.shape), _resident(qdec.shape), _resident(kdec.shape), _resident(cdec.shape),
            _resident(ret_g.shape),
        ],
        out_specs=pl.BlockSpec((c, vw), lambda b, i: (b * nc + i, 0)),
        out_shape=jax.ShapeDtypeStruct((batch * seq, vw), BF16),
        scratch_shapes=[pltpu.VMEM((RET_HEADS // 2, 2 * RET_QK_DIM, 2 * RET_V_DIM), F32)],
        compiler_params=_cparams(("arbitrary", "arbitrary")),
        name="retention",
    )(main, main, main, main, dtab, qdec, kdec, cdec, ret_g)


def _merge_kernel(x_ref, a_ref, r_ref, mq_ref, mk_ref, mv_ref, wg_ref, bg_ref, wd_ref, wr_ref, wm_ref,
                  wo_ref, g1_ref, b1_ref, o_ref, m_scr, merged_scr, *, nc):
    d = x_ref.shape[1]
    x = x_ref[...]
    xb = x.astype(BF16)

    scale = jnp.asarray(MEM_HEAD_DIM ** -0.5, BF16)
    for h in range(MEM_HEADS):
        cols = slice(h * MEM_HEAD_DIM, (h + 1) * MEM_HEAD_DIM)
        qh = mq_ref[:, cols] * scale
        s = lax.dot_general(qh, mk_ref[0, :, cols], (((1,), (1,)), ((), ())),
                            preferred_element_type=F32)
        p = jnp.exp(s - jnp.max(s, axis=-1, keepdims=True))
        l = jnp.sum(p, axis=-1, keepdims=True)
        mh = jnp.dot(p.astype(BF16), mv_ref[0, :, cols], preferred_element_type=F32) / l
        m_scr[:, cols] = mh.astype(BF16)

    a = a_ref[...]
    r = r_ref[...]
    m = m_scr[...]
    for c in range(0, d, nc):
        cs = slice(c, c + nc)
        acc = None
        for br, (val, w_ref) in enumerate(((a, wd_ref), (r, wr_ref), (m, wm_ref))):
            gs = slice(br * d + c, br * d + c + nc)
            gate = jax.nn.sigmoid(jnp.dot(xb, wg_ref[:, gs], preferred_element_type=F32) + bg_ref[:, gs])
            term = gate * jnp.dot(val, w_ref[:, cs], preferred_element_type=F32)
            acc = term if acc is None else acc + term
        merged_scr[:, cs] = acc.astype(BF16)

    mix = jnp.dot(merged_scr[...], wo_ref[...], preferred_element_type=F32)
    o_ref[...] = _layer_norm(DEEPNORM_ALPHA * x + mix, g1_ref[...], b1_ref[...])


def _merge(x2, a, r, main, memkv, wg, bg, wd, wr, wm, wo, g1, b1, batch, seq):
    tm = MERGE_TM
    t, d = x2.shape
    nt = seq // tm
    mlen = memkv.shape[1]
    mw = MEM_HEADS * MEM_HEAD_DIM
    row = lambda i: (i, 0)
    return pl.pallas_call(
        functools.partial(_merge_kernel, nc=MERGE_NC),
        grid=(t // tm,),
        in_specs=[
            pl.BlockSpec((tm, d), row),
            pl.BlockSpec((tm, a.shape[1]), row),
            pl.BlockSpec((tm, r.shape[1]), row),
            pl.BlockSpec((tm, mw), lambda i: (i, OFF_MQ // mw)),
            pl.BlockSpec((1, mlen, mw), lambda i: (i // nt, 0, 0)),
            pl.BlockSpec((1, mlen, mw), lambda i: (i // nt, 0, 1)),
            _resident(wg.shape), _resident(bg.shape), _resident(wd.shape), _resident(wr.shape),
            _resident(wm.shape), _resident(wo.shape), _resident(g1.shape), _resident(b1.shape),
        ],
        out_specs=pl.BlockSpec((tm, d), row),
        out_shape=jax.ShapeDtypeStruct((t, d), F32),
        scratch_shapes=[pltpu.VMEM((tm, mw), BF16), pltpu.VMEM((tm, d), BF16)],
        compiler_params=_cparams(("arbitrary",)),
        name="merge",
    )(x2, a, r, main, memkv, memkv, wg, bg, wd, wr, wm, wo, g1, b1)


def _ffn_kernel(h_ref, wu_ref, cw_ref, cb_ref, wdn_ref, g2_ref, b2_ref, o_ref,
                carry_scr, hs_scr, act_scr, *, fc, rb, ffn_dim, tiles_per_seq):
    tm = h_ref.shape[0]
    halo = SUBLANES

    @pl.when(pl.program_id(0) % tiles_per_seq == 0)
    def _():
        carry_scr[...] = jnp.zeros(carry_scr.shape, F32)

    x = h_ref[...]
    xb = x.astype(BF16)

    def up_project(slot, col):
        cs = slice(col, col + fc)
        up = jnp.dot(xb, wu_ref[:, cs], preferred_element_type=F32)
        hs_scr[slot, 0:halo, :] = carry_scr[:, cs]
        hs_scr[slot, halo:halo + tm, :] = up
        carry_scr[:, cs] = up[tm - halo:tm, :]

    def conv_rows(slot, col, r0):
        cs = slice(col, col + fc)
        blk = hs_scr[slot, r0:r0 + rb + halo, :]
        w = cw_ref[:, cs]
        y = pltpu.roll(blk, 2, 0) * w[0:1] + pltpu.roll(blk, 1, 0) * w[1:2] + blk * w[2:3] + cb_ref[:, cs]
        return y[halo:, :]

    for j, c in enumerate(range(0, ffn_dim, fc)):
        sg, sv = 2 * (j % 2), 2 * (j % 2) + 1
        up_project(sg, c)
        up_project(sv, ffn_dim + c)
        for r0 in range(0, tm, rb):
            g = conv_rows(sg, c, r0)
            val = conv_rows(sv, ffn_dim + c, r0)
            act_scr[r0:r0 + rb, c:c + fc] = (g * jax.nn.sigmoid(g) * val).astype(BF16)

    ffn = jnp.dot(act_scr[...], wdn_ref[...], preferred_element_type=F32)
    o_ref[...] = _layer_norm(DEEPNORM_ALPHA * x + ffn, g2_ref[...], b2_ref[...])


def _ffn(h1, wu, cw, cb, wdn, g2, b2, seq):
    tm = FFN_TM
    t, d = h1.shape
    ffn_dim = wdn.shape[0]
    row = lambda i: (i, 0)
    return pl.pallas_call(
        functools.partial(_ffn_kernel, fc=FFN_FC, rb=FFN_RB, ffn_dim=ffn_dim, tiles_per_seq=seq // tm),
        grid=(t // tm,),
        in_specs=[
            pl.BlockSpec((tm, d), row),
            _resident(wu.shape), _resident(cw.shape), _resident(cb.shape), _resident(wdn.shape),
            _resident(g2.shape), _resident(b2.shape),
        ],
        out_specs=pl.BlockSpec((tm, d), row),
        out_shape=jax.ShapeDtypeStruct((t, d), F32),
        scratch_shapes=[
            pltpu.VMEM((SUBLANES, 2 * ffn_dim), F32),
            pltpu.VMEM((4, tm + SUBLANES, FFN_FC), F32),
            pltpu.VMEM((tm, ffn_dim), BF16),
        ],
        compiler_params=_cparams(("arbitrary",)),
        name="conv_ffn",
    )(h1, wu, cw, cb, wdn, g2, b2)


def kernel(x, mem, w_in, diff_lambda, diff_subln_g, ret_norm_g, w_mem_kv, w_diff_o, w_ret_o, w_mem_o,
           w_gate, b_gate, w_mix_out, ln1_g, ln1_b, w_up, conv_w, conv_b, w_down, ln2_g, ln2_b):
    batch, seq, d = x.shape
    mlen = mem.shape[1]
    assert w_in.shape[0] == DEPTH == 1
    assert seq % PROJ_TM == 0 and seq % RET_C == 0 and seq % FFN_TM == 0 and seq % MERGE_TM == 0
    assert PROJ_TM % ATT_TK == 0 and ATT_TQ % CHUNK == 0 and ATT_TK % ATT_TQ == 0 and seq % (ATT_TQ * ATT_NS) == 0
    assert ATT_UNROLL % 2 == 0 and (ATT_NS * ATT_TQ // ATT_TK) % ATT_UNROLL == 0
    l = 0
    row = lambda v: v[l].reshape(1, -1)

    x2 = x.reshape(batch * seq, d)
    main, kp, vt, nrm = _inproj(x2, w_in[l])
    memkv = _proj(mem.reshape(batch * mlen, d), w_mem_kv[l].astype(BF16), batch * mlen, PROJ_NC)
    memkv = memkv.reshape(batch, mlen, 2 * MEM_HEADS * MEM_HEAD_DIM)

    a = _diff_attention(main, kp, vt, nrm, diff_lambda[l], row(diff_subln_g), batch, seq)
    r = _retention(main, row(ret_norm_g), batch, seq)
    h1 = _merge(x2, a, r, main, memkv, w_gate[l].astype(BF16), row(b_gate),
                w_diff_o[l].astype(BF16), w_ret_o[l].astype(BF16), w_mem_o[l].astype(BF16),
                w_mix_out[l].astype(BF16), row(ln1_g), row(ln1_b), batch, seq)
    out = _ffn(h1, w_up[l].astype(BF16), conv_w[l], row(conv_b), w_down[l].astype(BF16),
               row(ln2_g), row(ln2_b), seq)
    return out.reshape(batch, seq, d)
```

```python
import functools
import math

import numpy as np
import jax
import jax.numpy as jnp
from jax import lax
from jax.experimental import pallas as pl
from jax.experimental.pallas import tpu as pltpu

F32 = jnp.float32
BF16 = jnp.bfloat16

CHUNK = 64
DIFF_HEADS = 4
DIFF_QK_DIM = 64
DIFF_V_DIM = 128
RET_HEADS = 4
RET_QK_DIM = 64
RET_V_DIM = 128
MEM_HEADS = 4
MEM_HEAD_DIM = 128
N_BRANCHES = 3
CONV_WIDTH = 3
LN_EPS = 1e-5
DEPTH = 1
DEEPNORM_ALPHA = (2.0 * DEPTH) ** 0.25
LAMBDA_INIT = 0.8 - 0.6 * math.exp(-0.3 * 0)
LOG2E = math.log2(math.e)

IN_DQ, IN_DK, IN_DV = 0, 512, 1024
IN_RQ, IN_RK, IN_RV, IN_RG, IN_MQ, IN_END = 1536, 1792, 2048, 2560, 3072, 3584
OFF_DQ, OFF_RQ, OFF_RK, OFF_RV, OFF_RG, OFF_MQ, MAIN_W = 0, 512, 768, 1024, 1536, 2048, 2560

LANES = 128
SUBLANES = 8
BF16_ROWS = 16
VMEM_LIMIT = 56 * 1024 * 1024

PROJ_TM = 512
PROJ_NC = 512
ATT_TQ = 256
ATT_TK = 512
ATT_NS = 4
ATT_UNROLL = 2
ATT_RB = 128
ATT_UNDERFLOW_LOG2 = 150.0
VT_ROWS = DIFF_V_DIM + BF16_ROWS
RET_C = 256
MERGE_TM = 512
MERGE_NC = 512
FFN_TM = 512
FFN_FC = 256
FFN_RB = 128
EPI_RB = 256


def _cparams(sem):
    return pltpu.CompilerParams(dimension_semantics=sem, vmem_limit_bytes=VMEM_LIMIT)


def _resident(shape):
    nd = len(shape)
    return pl.BlockSpec(shape, lambda *_: (0,) * nd, pipeline_mode=pl.Buffered(1))


def _layer_norm(y, g, b):
    mu = jnp.mean(y, axis=-1, keepdims=True)
    d = y - mu
    var = jnp.mean(d * d, axis=-1, keepdims=True)
    return d * lax.rsqrt(var + LN_EPS) * g + b


def _alibi_slopes():
    return np.array([(2.0 ** (-8.0 / DIFF_HEADS)) ** (i + 1) for i in range(DIFF_HEADS)], np.float64)


def _bf16_split3(v):
    parts = []
    rem = np.float64(v)
    for _ in range(3):
        p = np.asarray(rem, np.float32).astype(jnp.bfloat16)
        parts.append(p)
        rem = rem - np.float64(p.astype(np.float32))
    return parts


def _proj_kernel(x_ref, w_ref, o_ref, *, nc):
    xb = x_ref[...].astype(BF16)
    n = w_ref.shape[1]
    for c in range(0, n, nc):
        o_ref[:, c:c + nc] = jnp.dot(xb, w_ref[:, c:c + nc],
                                     preferred_element_type=F32).astype(o_ref.dtype)


def _proj(x2, w, tm, nc):
    t, d = x2.shape
    n = w.shape[1]
    return pl.pallas_call(
        functools.partial(_proj_kernel, nc=nc),
        grid=(t // tm,),
        in_specs=[pl.BlockSpec((tm, d), lambda i: (i, 0)), _resident((d, n))],
        out_specs=pl.BlockSpec((tm, n), lambda i: (i, 0)),
        out_shape=jax.ShapeDtypeStruct((t, n), BF16),
        compiler_params=_cparams(("arbitrary",)),
        name="memkv",
    )(x2, w)


def _inproj_kernel(x_ref, wm_ref, wk_ref, wvt_ref, kaug_ref, sel_ref, main_ref, kp_ref, vt_ref, nrm_ref, *, nc):
    xb = x_ref[...].astype(BF16)
    qscale = LOG2E * DIFF_QK_DIM ** -0.5
    for c in range(0, MAIN_W, nc):
        acc = jnp.dot(xb, wm_ref[:, c:c + nc], preferred_element_type=F32)
        if c < OFF_RQ:
            acc = acc * qscale
        main_ref[:, c:c + nc] = acc.astype(main_ref.dtype)

    k = jnp.dot(xb, wk_ref[...], preferred_element_type=F32).astype(kp_ref.dtype)
    for h in range(DIFF_HEADS):
        kp_ref[:, h * 2 * LANES:h * 2 * LANES + LANES] = k[:, h * LANES:(h + 1) * LANES]
        kp_ref[:, h * 2 * LANES + LANES:(h + 1) * 2 * LANES] = kaug_ref[:, h * LANES:(h + 1) * LANES]

    def max_sq_norm(v):
        vf = v.astype(F32)
        sq = jnp.dot((vf * vf).astype(BF16), sel_ref[...], preferred_element_type=F32)
        return jnp.broadcast_to(jnp.max(sq, axis=0, keepdims=True), (SUBLANES // 2, LANES))

    nrm_ref[0] = jnp.concatenate([max_sq_norm(main_ref[:, OFF_DQ:OFF_RQ]), max_sq_norm(k)], axis=0)

    vt = lax.dot_general(wvt_ref[...], xb, (((1,), (1,)), ((), ())), preferred_element_type=F32)
    row = lax.broadcasted_iota(jnp.int32, vt.shape, 0)
    ones_row = row == DIFF_V_DIM
    for h in range(1, DIFF_HEADS):
        ones_row = ones_row | (row == h * VT_ROWS + DIFF_V_DIM)
    vt_ref[...] = jnp.where(ones_row, 1.0, vt).astype(vt_ref.dtype)


def _inproj(x2, w_in):
    t, d = x2.shape
    tm = PROJ_TM
    cols = lambda a, b: w_in[:, a:b]
    w_main = jnp.concatenate([cols(IN_DQ, IN_DK), cols(IN_RQ, IN_END)], axis=1).astype(BF16)
    w_k = cols(IN_DK, IN_DV).astype(BF16)
    wv = cols(IN_DV, IN_RQ).astype(BF16).T.reshape(DIFF_HEADS, DIFF_V_DIM, d)
    w_vt = jnp.pad(wv, ((0, 0), (0, VT_ROWS - DIFF_V_DIM), (0, 0))).reshape(DIFF_HEADS * VT_ROWS, d)

    kaug = _alibi_key_columns(tm, ATT_TK)
    sel = (np.arange(DIFF_HEADS * 2 * DIFF_QK_DIM)[:, None] // DIFF_QK_DIM == np.arange(LANES)[None, :])
    sel = jnp.asarray(sel, BF16)

    row = lambda i: (i, 0)
    return pl.pallas_call(
        functools.partial(_inproj_kernel, nc=PROJ_NC),
        grid=(t // tm,),
        in_specs=[pl.BlockSpec((tm, d), row), _resident(w_main.shape), _resident(w_k.shape),
                  _resident(w_vt.shape), _resident(kaug.shape), _resident(sel.shape)],
        out_specs=[pl.BlockSpec((tm, MAIN_W), row),
                   pl.BlockSpec((tm, DIFF_HEADS * 2 * LANES), row),
                   pl.BlockSpec((DIFF_HEADS * VT_ROWS, tm), lambda i: (0, i)),
                   pl.BlockSpec((1, SUBLANES, LANES), lambda i: (i, 0, 0))],
        out_shape=[jax.ShapeDtypeStruct((t, MAIN_W), BF16),
                   jax.ShapeDtypeStruct((t, DIFF_HEADS * 2 * LANES), BF16),
                   jax.ShapeDtypeStruct((DIFF_HEADS * VT_ROWS, t), BF16),
                   jax.ShapeDtypeStruct((t // tm, SUBLANES, LANES), F32)],
        compiler_params=_cparams(("arbitrary",)),
        name="inproj",
    )(x2, w_main, w_k, w_vt, kaug, sel)


def _attn_kernel(slope_ref, first_ref, q_ref, qaug_ref, kp_ref, vt_ref, tdiag_ref, lam_ref, g_ref, o_ref,
                 qs_scr, m_scr, acc_scr, z_scr, zmax_scr, p_scr, alpha_scr, *, tq, tk, ns):
    b, h, qi = pl.program_id(0), pl.program_id(1), pl.program_id(2)
    per_tile = tk // tq
    n0 = (ns // per_tile) * qi
    first = first_ref[b, h, qi]
    slope = slope_ref[h]

    @pl.when((b == 0) & (h == 0) & (qi == 0))
    def _():
        p_scr[...] = jnp.zeros(p_scr.shape, BF16)

    qa = qaug_ref[0]
    for j in range(ns):
        q = q_ref[j * tq:(j + 1) * tq, :]
        lane = lax.broadcasted_iota(jnp.int32, q.shape, 1)
        zero = jnp.zeros_like(q)
        q1 = jnp.concatenate([jnp.where(lane < DIFF_QK_DIM, q, zero), qa], axis=1).astype(F32)
        q2 = jnp.concatenate([jnp.where(lane >= DIFF_QK_DIM, q, zero), qa], axis=1).astype(F32)
        qs_scr[j, :, 0:tq] = q1.T.astype(BF16)
        qs_scr[j, :, tq:2 * tq] = q2.T.astype(BF16)
        m_scr[j] = jnp.full(m_scr.shape[1:], -jnp.inf, F32)
        acc_scr[j] = jnp.zeros(acc_scr.shape[1:], F32)
        alpha_scr[j, 1] = jnp.ones(alpha_scr.shape[2:], F32)

    def score_stage(j, kb, slot):
        start = pl.multiple_of(kb * tk, tk)
        z = jnp.dot(kp_ref[pl.ds(start, tk), :], qs_scr[j],
                    preferred_element_type=F32)
        zmax = None
        for r in range(0, tk, ATT_RB):
            zb = z[r:r + ATT_RB]
            z_scr[j, slot, r:r + ATT_RB] = zb
            bmax = jnp.max(zb, axis=0, keepdims=True)
            zmax = bmax if zmax is None else jnp.maximum(zmax, bmax)
        zmax_scr[j, slot] = zmax

    def softmax_stage(j, z, zmax, c, slot):
        m_prev = m_scr[j]
        m_new = jnp.maximum(m_prev, zmax + c)
        shift = m_new - c
        for r in range(0, tk, ATT_RB):
            p_scr[j, slot, r:r + ATT_RB] = jnp.exp2(z[r:r + ATT_RB] - shift).astype(BF16)
        alpha_scr[j, slot] = jnp.exp2(m_prev - m_new)
        m_scr[j] = m_new

    def value_stage(j, kb, slot):
        start = pl.multiple_of(jnp.maximum(kb, first) * tk, tk)
        v = vt_ref[:, pl.ds(start, tk)]
        v = jnp.where(kb >= first, v, jnp.zeros_like(v))
        acc_scr[j] = alpha_scr[j, slot] * acc_scr[j] + jnp.dot(v, p_scr[j, slot], preferred_element_type=F32)

    def tile_bias(j, kb):
        return -slope * ((n0 - kb) * tk + j * tq).astype(F32)

    def full_trip(kb, cur, nxt):
        for j in range(ns):
            score_stage(j, kb + 1, nxt)
            softmax_stage(j, z_scr.at[j, cur], zmax_scr[j, cur], tile_bias(j, kb), cur)
            value_stage(j, kb - 1, nxt)

    for j in range(ns):
        score_stage(j, first, 0)

    def unrolled(i, carry):
        for u in range(ATT_UNROLL):
            full_trip(ATT_UNROLL * i + u, u % 2, 1 - u % 2)
        return carry

    lax.fori_loop(first // ATT_UNROLL, n0 // ATT_UNROLL, unrolled, 0)

    for i in range(ns // per_tile + 1):
        kb = n0 + i
        cur, nxt = i % 2, 1 - i % 2
        for j in range(ns):
            last = j // per_tile
            if i < last:
                score_stage(j, kb + 1, nxt)
                softmax_stage(j, z_scr.at[j, cur], zmax_scr[j, cur], tile_bias(j, kb), cur)
            elif i == last:
                td = tdiag_ref[0, j % per_tile]
                z = z_scr[j, cur]
                z = jnp.concatenate([z[:, 0:tq] + td, z[:, tq:2 * tq] + td], axis=1)
                softmax_stage(j, z, jnp.max(z, axis=0, keepdims=True), tile_bias(j, kb), cur)
            if i <= last + 1:
                value_stage(j, kb - 1, nxt)

    lp = lam_ref[...]
    lam = (jnp.exp(jnp.sum(lp[0:1] * lp[1:2], axis=-1, keepdims=True))
           - jnp.exp(jnp.sum(lp[2:3] * lp[3:4], axis=-1, keepdims=True)) + LAMBDA_INIT)
    for j in range(ns):
        acc = acc_scr[j]
        o = acc[0:DIFF_V_DIM, :] / acc[DIFF_V_DIM:DIFF_V_DIM + 1, :]
        y = o[:, 0:tq] - lam * o[:, tq:2 * tq]
        y = y * lax.rsqrt(jnp.mean(y * y, axis=0, keepdims=True) + LN_EPS)
        o_ref[j * tq:(j + 1) * tq, :] = (y.T * g_ref[...] * (1.0 - LAMBDA_INIT)).astype(o_ref.dtype)


def _alibi_key_columns(n_rows, tk):
    l3 = _bf16_split3(LOG2E)
    off = np.arange(n_rows) % tk
    kaug = np.zeros((n_rows, DIFF_HEADS, LANES), np.float32)
    for hh, m in enumerate(_alibi_slopes()):
        for i in range(3):
            kaug[:, hh, i] = np.float32(l3[i])
            kaug[:, hh, 3 + i] = m * (off % ATT_TQ)
            kaug[:, hh, 6 + i] = m * (off - off % ATT_TQ)
    return jnp.asarray(kaug.reshape(n_rows, DIFF_HEADS * LANES), BF16)


def _attn_tables(tq, tk):
    slopes = _alibi_slopes()
    l3 = _bf16_split3(LOG2E)
    il = np.arange(tq, dtype=np.float64)
    qaug = np.zeros((DIFF_HEADS, tq, LANES), np.float32)
    for hh, m in enumerate(slopes):
        for i in range(3):
            qaug[hh, :, i] = -m * il
            qaug[hh, :, 3 + i] = np.float32(l3[i])
            qaug[hh, :, 6 + i] = np.float32(l3[i])
    j = np.arange(tk)[None, :, None]
    i = np.arange(tq)[None, None, :] + tq * np.arange(tk // tq)[:, None, None]
    allowed = (j // CHUNK) <= (i // CHUNK)
    corr = LOG2E * slopes[:, None, None, None] * ((i - j) - np.abs(i - j))[None]
    tdiag = np.where(allowed[None], corr, -np.inf)
    return (jnp.asarray(slopes * LOG2E, F32), jnp.asarray(qaug, BF16), jnp.asarray(tdiag, F32))


def _first_key_tiles(nrm, batch, seq):
    tiles_per_seq = seq // PROJ_TM
    tsup = ATT_TQ * ATT_NS
    n = nrm.reshape(batch, tiles_per_seq, 2, SUBLANES // 2, LANES)[:, :, :, 0, :DIFF_HEADS * 2]
    n = n.reshape(batch, tiles_per_seq, 2, DIFF_HEADS, 2).max(axis=-1)
    q2 = n[:, :, 0].reshape(batch, seq // tsup, tsup // PROJ_TM, DIFF_HEADS).max(axis=2)
    k2 = n[:, :, 1].max(axis=1)
    bound = jnp.sqrt(q2 * k2[:, None, :]) * 1.02 + 0.05
    slope = jnp.asarray(_alibi_slopes() * LOG2E, F32)
    reach = (2.0 * bound + ATT_UNDERFLOW_LOG2) / slope
    needed = jnp.floor(jnp.maximum(reach - 1.0, 0.0) / ATT_TK).astype(jnp.int32) + 1
    n0 = (tsup // ATT_TK) * jnp.arange(seq // tsup, dtype=jnp.int32)[None, :, None]
    first = jnp.maximum(n0 - needed, 0) // ATT_UNROLL * ATT_UNROLL
    return first.transpose(0, 2, 1)


def _diff_attention(main, kp, vt, nrm, lam_p, subln_g, batch, seq):
    tq, tk, ns = ATT_TQ, ATT_TK, ATT_NS
    tsup = tq * ns
    nq = seq // tsup
    slopes, qaug, tdiag = _attn_tables(tq, tk)
    first = _first_key_tiles(nrm, batch, seq)
    return pl.pallas_call(
        functools.partial(_attn_kernel, tq=tq, tk=tk, ns=ns),
        grid=(batch, DIFF_HEADS, nq),
        in_specs=[
            pl.BlockSpec(memory_space=pltpu.SMEM),
            pl.BlockSpec(memory_space=pltpu.SMEM),
            pl.BlockSpec((tsup, LANES), lambda b, h, qi: (b * nq + qi, OFF_DQ // LANES + h)),
            pl.BlockSpec((1, tq, LANES), lambda b, h, qi: (h, 0, 0)),
            pl.BlockSpec((seq, 2 * LANES), lambda b, h, qi: (b, h)),
            pl.BlockSpec((VT_ROWS, seq), lambda b, h, qi: (h, b)),
            pl.BlockSpec((1,) + tdiag.shape[1:], lambda b, h, qi: (h, 0, 0, 0)),
            pl.BlockSpec(lam_p.shape, lambda b, h, qi: (0, 0)),
            pl.BlockSpec((1, LANES), lambda b, h, qi: (0, h)),
        ],
        out_specs=pl.BlockSpec((tsup, LANES), lambda b, h, qi: (b * nq + qi, h)),
        out_shape=jax.ShapeDtypeStruct((batch * seq, DIFF_HEADS * DIFF_V_DIM), BF16),
        scratch_shapes=[
            pltpu.VMEM((ns, 2 * LANES, 2 * tq), BF16),
            pltpu.VMEM((ns, 1, 2 * tq), F32),
            pltpu.VMEM((ns, VT_ROWS, 2 * tq), F32),
            pltpu.VMEM((ns, 2, tk, 2 * tq), F32),
            pltpu.VMEM((ns, 2, 1, 2 * tq), F32),
            pltpu.VMEM((ns, 2, tk, 2 * tq), BF16),
            pltpu.VMEM((ns, 2, 1, 2 * tq), F32),
        ],
        compiler_params=_cparams(("arbitrary", "arbitrary", "arbitrary")),
        name="diff_attn",
    )(slopes, first, main, qaug, kp, vt, tdiag, lam_p, subln_g)


def _ret_kernel(q_ref, k_ref, v_ref, g_ref, dtab_ref, qdec_ref, kdec_ref, cdec_ref, gn_ref, o_ref,
                state_scr, *, c):
    @pl.when(pl.program_id(1) == 0)
    def _():
        state_scr[...] = jnp.zeros(state_scr.shape, F32)

    lane = lax.broadcasted_iota(jnp.int32, (c, LANES), 1)
    for p in range(RET_HEADS // 2):
        qp = q_ref[:, p * LANES:(p + 1) * LANES]
        kp = k_ref[:, p * LANES:(p + 1) * LANES]
        vp = v_ref[:, p * 2 * RET_V_DIM:(p + 1) * 2 * RET_V_DIM]
        state = state_scr[p]
        state_b = state.astype(BF16)
        qdec = qdec_ref[:, p * LANES:(p + 1) * LANES]
        for e in range(2):
            head = 2 * p + e
            keep = (lane < RET_QK_DIM) if e == 0 else (lane >= RET_QK_DIM)
            qm = jnp.where(keep, qp, jnp.zeros_like(qp))
            s = lax.dot_general(qm, kp, (((1,), (1,)), ((), ())), preferred_element_type=F32)
            s = s * dtab_ref[head]
            ve = vp[:, e * RET_V_DIM:(e + 1) * RET_V_DIM]
            intra = jnp.dot(s.astype(BF16), ve, preferred_element_type=F32)
            qd = (qm.astype(F32) * qdec).astype(BF16)
            cross = jnp.dot(qd, state_b[:, e * RET_V_DIM:(e + 1) * RET_V_DIM],
                            preferred_element_type=F32)
            o = intra + cross
            mu = jnp.mean(o, axis=-1, keepdims=True)
            d = o - mu
            var = jnp.mean(d * d, axis=-1, keepdims=True)
            cols = slice(head * RET_V_DIM, (head + 1) * RET_V_DIM)
            gate = g_ref[:, cols].astype(F32)
            y = d * lax.rsqrt(var + LN_EPS) * gn_ref[:, cols] * (gate * jax.nn.sigmoid(gate))
            o_ref[:, cols] = y.astype(o_ref.dtype)
        kd = (kp.astype(F32) * kdec_ref[:, p * LANES:(p + 1) * LANES]).astype(BF16)
        kv = lax.dot_general(kd, vp, (((0,), (0,)), ((), ())), preferred_element_type=F32)
        state_scr[p] = state * cdec_ref[p] + kv


def _ret_tables(c):
    hh = np.arange(RET_HEADS, dtype=np.float64)
    lg = np.log(1.0 - 2.0 ** (-5.0 - hh))
    idx = np.arange(c, dtype=np.float64)
    nm = idx[:, None] - idx[None, :]
    dtab = np.where(nm >= 0, np.exp(lg[:, None, None] * np.maximum(nm, 0.0)), 0.0) * RET_QK_DIM ** -0.5
    qdec = np.exp(lg[None, :] * (idx + 1.0)[:, None])
    kdec = np.exp(lg[None, :] * (c - 1.0 - idx)[:, None]) * RET_QK_DIM ** -0.5
    qdec = np.repeat(qdec, RET_QK_DIM, axis=1)
    kdec = np.repeat(kdec, RET_QK_DIM, axis=1)
    cdec = np.repeat(np.exp(lg * c), RET_QK_DIM).reshape(RET_HEADS // 2, 2 * RET_QK_DIM, 1)
    return tuple(jnp.asarray(a, F32) for a in (dtab, qdec, kdec, cdec))


def _retention(main, ret_g, batch, seq):
    c = RET_C
    nc = seq // c
    dtab, qdec, kdec, cdec = _ret_tables(c)
    qkw = RET_HEADS * RET_QK_DIM
    vw = RET_HEADS * RET_V_DIM
    return pl.pallas_call(
        functools.partial(_ret_kernel, c=c),
        grid=(batch, nc),
        in_specs=[
            pl.BlockSpec((c, qkw), lambda b, i: (b * nc + i, OFF_RQ // qkw)),
            pl.BlockSpec((c, qkw), lambda b, i: (b * nc + i, OFF_RK // qkw)),
            pl.BlockSpec((c, vw), lambda b, i: (b * nc + i, OFF_RV // vw)),
            pl.BlockSpec((c, vw), lambda b, i: (b * nc + i, OFF_RG // vw)),
            _resident(dtab.shape), _resident(qdec.shape), _resident(kdec.shape), _resident(cdec.shape),
            _resident(ret_g.shape),
        ],
        out_specs=pl.BlockSpec((c, vw), lambda b, i: (b * nc + i, 0)),
        out_shape=jax.ShapeDtypeStruct((batch * seq, vw), BF16),
        scratch_shapes=[pltpu.VMEM((RET_HEADS // 2, 2 * RET_QK_DIM, 2 * RET_V_DIM), F32)],
        compiler_params=_cparams(("arbitrary", "arbitrary")),
        name="retention",
    )(main, main, main, main, dtab, qdec, kdec, cdec, ret_g)


def _merge_kernel(x_ref, a_ref, r_ref, mq_ref, mk_ref, mv_ref, wg_ref, bg_ref, wd_ref, wr_ref, wm_ref,
                  wo_ref, g1_ref, b1_ref, o_ref, m_scr, merged_scr, *, nc):
    d = x_ref.shape[1]
    x = x_ref[...]
    xb = x.astype(BF16)

    scale = jnp.asarray(MEM_HEAD_DIM ** -0.5, BF16)
    for h in range(MEM_HEADS):
        cols = slice(h * MEM_HEAD_DIM, (h + 1) * MEM_HEAD_DIM)
        qh = mq_ref[:, cols] * scale
        s = lax.dot_general(qh, mk_ref[0, :, cols], (((1,), (1,)), ((), ())),
                            preferred_element_type=F32)
        p = jnp.exp(s - jnp.max(s, axis=-1, keepdims=True))
        l = jnp.sum(p, axis=-1, keepdims=True)
        mh = jnp.dot(p.astype(BF16), mv_ref[0, :, cols], preferred_element_type=F32) / l
        m_scr[:, cols] = mh.astype(BF16)

    a = a_ref[...]
    r = r_ref[...]
    m = m_scr[...]
    for c in range(0, d, nc):
        cs = slice(c, c + nc)
        acc = None
        for br, (val, w_ref) in enumerate(((a, wd_ref), (r, wr_ref), (m, wm_ref))):
            gs = slice(br * d + c, br * d + c + nc)
            gate = jax.nn.sigmoid(jnp.dot(xb, wg_ref[:, gs], preferred_element_type=F32) + bg_ref[:, gs])
            term = gate * jnp.dot(val, w_ref[:, cs], preferred_element_type=F32)
            acc = term if acc is None else acc + term
        merged_scr[:, cs] = acc.astype(BF16)

    for r0 in range(0, x_ref.shape[0], EPI_RB):
        rows = slice(r0, r0 + EPI_RB)
        mix = jnp.dot(merged_scr[rows, :], wo_ref[...], preferred_element_type=F32)
        o_ref[rows, :] = _layer_norm(DEEPNORM_ALPHA * x_ref[rows, :] + mix, g1_ref[...], b1_ref[...])


def _merge(x2, a, r, main, memkv, wg, bg, wd, wr, wm, wo, g1, b1, batch, seq):
    tm = MERGE_TM
    t, d = x2.shape
    nt = seq // tm
    mlen = memkv.shape[1]
    mw = MEM_HEADS * MEM_HEAD_DIM
    row = lambda i: (i, 0)
    return pl.pallas_call(
        functools.partial(_merge_kernel, nc=MERGE_NC),
        grid=(t // tm,),
        in_specs=[
            pl.BlockSpec((tm, d), row),
            pl.BlockSpec((tm, a.shape[1]), row),
            pl.BlockSpec((tm, r.shape[1]), row),
            pl.BlockSpec((tm, mw), lambda i: (i, OFF_MQ // mw)),
            pl.BlockSpec((1, mlen, mw), lambda i: (i // nt, 0, 0)),
            pl.BlockSpec((1, mlen, mw), lambda i: (i // nt, 0, 1)),
            _resident(wg.shape), _resident(bg.shape), _resident(wd.shape), _resident(wr.shape),
            _resident(wm.shape), _resident(wo.shape), _resident(g1.shape), _resident(b1.shape),
        ],
        out_specs=pl.BlockSpec((tm, d), row),
        out_shape=jax.ShapeDtypeStruct((t, d), F32),
        scratch_shapes=[pltpu.VMEM((tm, mw), BF16), pltpu.VMEM((tm, d), BF16)],
        compiler_params=_cparams(("arbitrary",)),
        name="merge",
    )(x2, a, r, main, memkv, memkv, wg, bg, wd, wr, wm, wo, g1, b1)


def _ffn_kernel(h_ref, wu_ref, cw_ref, cb_ref, wdn_ref, g2_ref, b2_ref, o_ref,
                carry_scr, hs_scr, act_scr, *, fc, rb, ffn_dim, tiles_per_seq):
    tm = h_ref.shape[0]
    halo = SUBLANES

    @pl.when(pl.program_id(0) % tiles_per_seq == 0)
    def _():
        carry_scr[...] = jnp.zeros(carry_scr.shape, F32)

    x = h_ref[...]
    xb = x.astype(BF16)

    def up_project(slot, col):
        cs = slice(col, col + fc)
        up = jnp.dot(xb, wu_ref[:, cs], preferred_element_type=F32)
        hs_scr[slot, 0:halo, :] = carry_scr[:, cs]
        hs_scr[slot, halo:halo + tm, :] = up
        carry_scr[:, cs] = up[tm - halo:tm, :]

    def conv_rows(slot, col, r0):
        cs = slice(col, col + fc)
        blk = hs_scr[slot, r0:r0 + rb + halo, :]
        w = cw_ref[:, cs]
        y = pltpu.roll(blk, 2, 0) * w[0:1] + pltpu.roll(blk, 1, 0) * w[1:2] + blk * w[2:3] + cb_ref[:, cs]
        return y[halo:, :]

    for j, c in enumerate(range(0, ffn_dim, fc)):
        sg, sv = 2 * (j % 2), 2 * (j % 2) + 1
        up_project(sg, c)
        up_project(sv, ffn_dim + c)
        for r0 in range(0, tm, rb):
            g = conv_rows(sg, c, r0)
            val = conv_rows(sv, ffn_dim + c, r0)
            act_scr[r0:r0 + rb, c:c + fc] = (g * jax.nn.sigmoid(g) * val).astype(BF16)

    for r0 in range(0, tm, EPI_RB):
        rows = slice(r0, r0 + EPI_RB)
        ffn = jnp.dot(act_scr[rows, :], wdn_ref[...], preferred_element_type=F32)
        o_ref[rows, :] = _layer_norm(DEEPNORM_ALPHA * h_ref[rows, :] + ffn, g2_ref[...], b2_ref[...])


def _ffn(h1, wu, cw, cb, wdn, g2, b2, seq):
    tm = FFN_TM
    t, d = h1.shape
    ffn_dim = wdn.shape[0]
    row = lambda i: (i, 0)
    return pl.pallas_call(
        functools.partial(_ffn_kernel, fc=FFN_FC, rb=FFN_RB, ffn_dim=ffn_dim, tiles_per_seq=seq // tm),
        grid=(t // tm,),
        in_specs=[
            pl.BlockSpec((tm, d), row),
            _resident(wu.shape), _resident(cw.shape), _resident(cb.shape), _resident(wdn.shape),
            _resident(g2.shape), _resident(b2.shape),
        ],
        out_specs=pl.BlockSpec((tm, d), row),
        out_shape=jax.ShapeDtypeStruct((t, d), F32),
        scratch_shapes=[
            pltpu.VMEM((SUBLANES, 2 * ffn_dim), F32),
            pltpu.VMEM((4, tm + SUBLANES, FFN_FC), F32),
            pltpu.VMEM((tm, ffn_dim), BF16),
        ],
        compiler_params=_cparams(("arbitrary",)),
        name="conv_ffn",
    )(h1, wu, cw, cb, wdn, g2, b2)


def kernel(x, mem, w_in, diff_lambda, diff_subln_g, ret_norm_g, w_mem_kv, w_diff_o, w_ret_o, w_mem_o,
           w_gate, b_gate, w_mix_out, ln1_g, ln1_b, w_up, conv_w, conv_b, w_down, ln2_g, ln2_b):
    batch, seq, d = x.shape
    mlen = mem.shape[1]
    assert w_in.shape[0] == DEPTH == 1
    assert seq % PROJ_TM == 0 and seq % RET_C == 0 and seq % FFN_TM == 0 and seq % MERGE_TM == 0
    assert PROJ_TM % ATT_TK == 0 and ATT_TQ % CHUNK == 0 and ATT_TK % ATT_TQ == 0 and seq % (ATT_TQ * ATT_NS) == 0
    assert ATT_UNROLL % 2 == 0 and (ATT_NS * ATT_TQ // ATT_TK) % ATT_UNROLL == 0
    l = 0
    row = lambda v: v[l].reshape(1, -1)

    x2 = x.reshape(batch * seq, d)
    main, kp, vt, nrm = _inproj(x2, w_in[l])
    memkv = _proj(mem.reshape(batch * mlen, d), w_mem_kv[l].astype(BF16), batch * mlen, PROJ_NC)
    memkv = memkv.reshape(batch, mlen, 2 * MEM_HEADS * MEM_HEAD_DIM)

    a = _diff_attention(main, kp, vt, nrm, diff_lambda[l], row(diff_subln_g), batch, seq)
    r = _retention(main, row(ret_norm_g), batch, seq)
    h1 = _merge(x2, a, r, main, memkv, w_gate[l].astype(BF16), row(b_gate),
                w_diff_o[l].astype(BF16), w_ret_o[l].astype(BF16), w_mem_o[l].astype(BF16),
                w_mix_out[l].astype(BF16), row(ln1_g), row(ln1_b), batch, seq)
    out = _ffn(h1, w_up[l].astype(BF16), conv_w[l], row(conv_b), w_down[l].astype(BF16),
               row(ln2_g), row(ln2_b), seq)
    return out.reshape(batch, seq, d)
```

```python
import functools
import math

import numpy as np
import jax
import jax.numpy as jnp
from jax import lax
from jax.experimental import pallas as pl
from jax.experimental.pallas import tpu as pltpu

F32 = jnp.float32
BF16 = jnp.bfloat16

CHUNK = 64
DIFF_HEADS = 4
DIFF_QK_DIM = 64
DIFF_V_DIM = 128
RET_HEADS = 4
RET_QK_DIM = 64
RET_V_DIM = 128
MEM_HEADS = 4
MEM_HEAD_DIM = 128
N_BRANCHES = 3
CONV_WIDTH = 3
LN_EPS = 1e-5
DEPTH = 1
DEEPNORM_ALPHA = (2.0 * DEPTH) ** 0.25
LAMBDA_INIT = 0.8 - 0.6 * math.exp(-0.3 * 0)
LOG2E = math.log2(math.e)

IN_DQ, IN_DK, IN_DV = 0, 512, 1024
IN_RQ, IN_RK, IN_RV, IN_RG, IN_MQ, IN_END = 1536, 1792, 2048, 2560, 3072, 3584
OFF_DQ, OFF_RQ, OFF_RK, OFF_RV, OFF_RG, OFF_MQ, MAIN_W = 0, 512, 768, 1024, 1536, 2048, 2560

LANES = 128
SUBLANES = 8
BF16_ROWS = 16
VMEM_LIMIT = 56 * 1024 * 1024

PROJ_TM = 512
PROJ_NC = 512
ATT_TQ = 256
ATT_TK = 512
ATT_NS = 4
ATT_UNROLL = 2
ATT_RB = 128
ATT_UNDERFLOW_LOG2 = 134.0
VT_ROWS = DIFF_V_DIM + BF16_ROWS
RET_C = 256
MERGE_TM = 512
MERGE_NC = 512
FFN_TM = 512
FFN_FC = 256
FFN_RB = 128
EPI_RB = 256


def _cparams(sem):
    return pltpu.CompilerParams(dimension_semantics=sem, vmem_limit_bytes=VMEM_LIMIT)


def _resident(shape):
    nd = len(shape)
    return pl.BlockSpec(shape, lambda *_: (0,) * nd, pipeline_mode=pl.Buffered(1))


def _layer_norm(y, g, b):
    mu = jnp.mean(y, axis=-1, keepdims=True)
    d = y - mu
    var = jnp.mean(d * d, axis=-1, keepdims=True)
    return d * lax.rsqrt(var + LN_EPS) * g + b


def _alibi_slopes():
    return np.array([(2.0 ** (-8.0 / DIFF_HEADS)) ** (i + 1) for i in range(DIFF_HEADS)], np.float64)


def _bf16_split3(v):
    parts = []
    rem = np.float64(v)
    for _ in range(3):
        p = np.asarray(rem, np.float32).astype(jnp.bfloat16)
        parts.append(p)
        rem = rem - np.float64(p.astype(np.float32))
    return parts


def _proj_kernel(x_ref, w_ref, o_ref, *, nc):
    xb = x_ref[...].astype(BF16)
    n = w_ref.shape[1]
    for c in range(0, n, nc):
        o_ref[:, c:c + nc] = jnp.dot(xb, w_ref[:, c:c + nc],
                                     preferred_element_type=F32).astype(o_ref.dtype)


def _proj(x2, w, tm, nc):
    t, d = x2.shape
    n = w.shape[1]
    return pl.pallas_call(
        functools.partial(_proj_kernel, nc=nc),
        grid=(t // tm,),
        in_specs=[pl.BlockSpec((tm, d), lambda i: (i, 0)), _resident((d, n))],
        out_specs=pl.BlockSpec((tm, n), lambda i: (i, 0)),
        out_shape=jax.ShapeDtypeStruct((t, n), BF16),
        compiler_params=_cparams(("arbitrary",)),
        name="memkv",
    )(x2, w)


def _inproj_kernel(x_ref, wm_ref, wk_ref, wvt_ref, kaug_ref, sel_ref, main_ref, kp_ref, vt_ref, nrm_ref, *, nc):
    xb = x_ref[...].astype(BF16)
    qscale = LOG2E * DIFF_QK_DIM ** -0.5
    for c in range(0, MAIN_W, nc):
        acc = jnp.dot(xb, wm_ref[:, c:c + nc], preferred_element_type=F32)
        if c < OFF_RQ:
            acc = acc * qscale
        main_ref[:, c:c + nc] = acc.astype(main_ref.dtype)

    k = jnp.dot(xb, wk_ref[...], preferred_element_type=F32).astype(kp_ref.dtype)
    for h in range(DIFF_HEADS):
        kp_ref[:, h * 2 * LANES:h * 2 * LANES + LANES] = k[:, h * LANES:(h + 1) * LANES]
        kp_ref[:, h * 2 * LANES + LANES:(h + 1) * 2 * LANES] = kaug_ref[:, h * LANES:(h + 1) * LANES]

    def max_sq_norm(v):
        vf = v.astype(F32)
        sq = jnp.dot((vf * vf).astype(BF16), sel_ref[...], preferred_element_type=F32)
        return jnp.broadcast_to(jnp.max(sq, axis=0, keepdims=True), (SUBLANES // 2, LANES))

    nrm_ref[0] = jnp.concatenate([max_sq_norm(main_ref[:, OFF_DQ:OFF_RQ]), max_sq_norm(k)], axis=0)

    vt = lax.dot_general(wvt_ref[...], xb, (((1,), (1,)), ((), ())), preferred_element_type=F32)
    row = lax.broadcasted_iota(jnp.int32, vt.shape, 0)
    ones_row = row == DIFF_V_DIM
    for h in range(1, DIFF_HEADS):
        ones_row = ones_row | (row == h * VT_ROWS + DIFF_V_DIM)
    vt_ref[...] = jnp.where(ones_row, 1.0, vt).astype(vt_ref.dtype)


def _inproj(x2, w_in):
    t, d = x2.shape
    tm = PROJ_TM
    cols = lambda a, b: w_in[:, a:b]
    w_main = jnp.concatenate([cols(IN_DQ, IN_DK), cols(IN_RQ, IN_END)], axis=1).astype(BF16)
    w_k = cols(IN_DK, IN_DV).astype(BF16)
    wv = cols(IN_DV, IN_RQ).astype(BF16).T.reshape(DIFF_HEADS, DIFF_V_DIM, d)
    w_vt = jnp.pad(wv, ((0, 0), (0, VT_ROWS - DIFF_V_DIM), (0, 0))).reshape(DIFF_HEADS * VT_ROWS, d)

    kaug = _alibi_key_columns(tm, ATT_TK)
    sel = (np.arange(DIFF_HEADS * 2 * DIFF_QK_DIM)[:, None] // DIFF_QK_DIM == np.arange(LANES)[None, :])
    sel = jnp.asarray(sel, BF16)

    row = lambda i: (i, 0)
    return pl.pallas_call(
        functools.partial(_inproj_kernel, nc=PROJ_NC),
        grid=(t // tm,),
        in_specs=[pl.BlockSpec((tm, d), row), _resident(w_main.shape), _resident(w_k.shape),
                  _resident(w_vt.shape), _resident(kaug.shape), _resident(sel.shape)],
        out_specs=[pl.BlockSpec((tm, MAIN_W), row),
                   pl.BlockSpec((tm, DIFF_HEADS * 2 * LANES), row),
                   pl.BlockSpec((DIFF_HEADS * VT_ROWS, tm), lambda i: (0, i)),
                   pl.BlockSpec((1, SUBLANES, LANES), lambda i: (i, 0, 0))],
        out_shape=[jax.ShapeDtypeStruct((t, MAIN_W), BF16),
                   jax.ShapeDtypeStruct((t, DIFF_HEADS * 2 * LANES), BF16),
                   jax.ShapeDtypeStruct((DIFF_HEADS * VT_ROWS, t), BF16),
                   jax.ShapeDtypeStruct((t // tm, SUBLANES, LANES), F32)],
        compiler_params=_cparams(("arbitrary",)),
        name="inproj",
    )(x2, w_main, w_k, w_vt, kaug, sel)


def _attn_kernel(slope_ref, first_ref, q_ref, qaug_ref, kp_ref, vt_ref, tdiag_ref, lam_ref, g_ref, o_ref,
                 qs_scr, m_scr, acc_scr, z_scr, zmax_scr, p_scr, alpha_scr, *, tq, tk, ns):
    b, h, qi = pl.program_id(0), pl.program_id(1), pl.program_id(2)
    per_tile = tk // tq
    n0 = (ns // per_tile) * qi
    first = first_ref[b, h, qi]
    slope = slope_ref[h]

    @pl.when((b == 0) & (h == 0) & (qi == 0))
    def _():
        p_scr[...] = jnp.zeros(p_scr.shape, BF16)

    qa = qaug_ref[0]
    for j in range(ns):
        q = q_ref[j * tq:(j + 1) * tq, :]
        lane = lax.broadcasted_iota(jnp.int32, q.shape, 1)
        zero = jnp.zeros_like(q)
        q1 = jnp.concatenate([jnp.where(lane < DIFF_QK_DIM, q, zero), qa], axis=1).astype(F32)
        q2 = jnp.concatenate([jnp.where(lane >= DIFF_QK_DIM, q, zero), qa], axis=1).astype(F32)
        qs_scr[j, :, 0:tq] = q1.T.astype(BF16)
        qs_scr[j, :, tq:2 * tq] = q2.T.astype(BF16)
        m_scr[j] = jnp.full(m_scr.shape[1:], -jnp.inf, F32)
        acc_scr[j] = jnp.zeros(acc_scr.shape[1:], F32)
        alpha_scr[j, 1] = jnp.ones(alpha_scr.shape[2:], F32)

    def score_stage(j, kb, slot):
        start = pl.multiple_of(kb * tk, tk)
        z = jnp.dot(kp_ref[pl.ds(start, tk), :], qs_scr[j],
                    preferred_element_type=F32)
        zmax = None
        for r in range(0, tk, ATT_RB):
            zb = z[r:r + ATT_RB]
            z_scr[j, slot, r:r + ATT_RB] = zb
            bmax = jnp.max(zb, axis=0, keepdims=True)
            zmax = bmax if zmax is None else jnp.maximum(zmax, bmax)
        zmax_scr[j, slot] = zmax

    def softmax_stage(j, z, zmax, c, slot):
        m_prev = m_scr[j]
        m_new = jnp.maximum(m_prev, zmax + c)
        shift = m_new - c
        for r in range(0, tk, ATT_RB):
            p_scr[j, slot, r:r + ATT_RB] = jnp.exp2(z[r:r + ATT_RB] - shift).astype(BF16)
        alpha_scr[j, slot] = jnp.exp2(m_prev - m_new)
        m_scr[j] = m_new

    def value_stage(j, kb, slot):
        start = pl.multiple_of(jnp.maximum(kb, first) * tk, tk)
        v = vt_ref[:, pl.ds(start, tk)]
        v = jnp.where(kb >= first, v, jnp.zeros_like(v))
        acc_scr[j] = alpha_scr[j, slot] * acc_scr[j] + jnp.dot(v, p_scr[j, slot], preferred_element_type=F32)

    def tile_bias(j, kb):
        return -slope * ((n0 - kb) * tk + j * tq).astype(F32)

    def full_trip(kb, cur, nxt):
        for j in range(ns):
            score_stage(j, kb + 1, nxt)
            softmax_stage(j, z_scr.at[j, cur], zmax_scr[j, cur], tile_bias(j, kb), cur)
            value_stage(j, kb - 1, nxt)

    for j in range(ns):
        score_stage(j, first, 0)

    def unrolled(i, carry):
        for u in range(ATT_UNROLL):
            full_trip(ATT_UNROLL * i + u, u % 2, 1 - u % 2)
        return carry

    lax.fori_loop(first // ATT_UNROLL, n0 // ATT_UNROLL, unrolled, 0)

    for i in range(ns // per_tile + 1):
        kb = n0 + i
        cur, nxt = i % 2, 1 - i % 2
        for j in range(ns):
            last = j // per_tile
            if i < last:
                score_stage(j, kb + 1, nxt)
                softmax_stage(j, z_scr.at[j, cur], zmax_scr[j, cur], tile_bias(j, kb), cur)
            elif i == last:
                td = tdiag_ref[0, j % per_tile]
                z = z_scr[j, cur]
                z = jnp.concatenate([z[:, 0:tq] + td, z[:, tq:2 * tq] + td], axis=1)
                softmax_stage(j, z, jnp.max(z, axis=0, keepdims=True), tile_bias(j, kb), cur)
            if i <= last + 1:
                value_stage(j, kb - 1, nxt)

    lp = lam_ref[...]
    lam = (jnp.exp(jnp.sum(lp[0:1] * lp[1:2], axis=-1, keepdims=True))
           - jnp.exp(jnp.sum(lp[2:3] * lp[3:4], axis=-1, keepdims=True)) + LAMBDA_INIT)
    for j in range(ns):
        acc = acc_scr[j]
        o = acc[0:DIFF_V_DIM, :] / acc[DIFF_V_DIM:DIFF_V_DIM + 1, :]
        y = o[:, 0:tq] - lam * o[:, tq:2 * tq]
        y = y * lax.rsqrt(jnp.mean(y * y, axis=0, keepdims=True) + LN_EPS)
        o_ref[j * tq:(j + 1) * tq, :] = (y.T * g_ref[...] * (1.0 - LAMBDA_INIT)).astype(o_ref.dtype)


def _alibi_key_columns(n_rows, tk):
    l3 = _bf16_split3(LOG2E)
    off = np.arange(n_rows) % tk
    kaug = np.zeros((n_rows, DIFF_HEADS, LANES), np.float32)
    for hh, m in enumerate(_alibi_slopes()):
        for i in range(3):
            kaug[:, hh, i] = np.float32(l3[i])
            kaug[:, hh, 3 + i] = m * (off % ATT_TQ)
            kaug[:, hh, 6 + i] = m * (off - off % ATT_TQ)
    return jnp.asarray(kaug.reshape(n_rows, DIFF_HEADS * LANES), BF16)


def _attn_tables(tq, tk):
    slopes = _alibi_slopes()
    l3 = _bf16_split3(LOG2E)
    il = np.arange(tq, dtype=np.float64)
    qaug = np.zeros((DIFF_HEADS, tq, LANES), np.float32)
    for hh, m in enumerate(slopes):
        for i in range(3):
            qaug[hh, :, i] = -m * il
            qaug[hh, :, 3 + i] = np.float32(l3[i])
            qaug[hh, :, 6 + i] = np.float32(l3[i])
    j = np.arange(tk)[None, :, None]
    i = np.arange(tq)[None, None, :] + tq * np.arange(tk // tq)[:, None, None]
    allowed = (j // CHUNK) <= (i // CHUNK)
    corr = LOG2E * slopes[:, None, None, None] * ((i - j) - np.abs(i - j))[None]
    tdiag = np.where(allowed[None], corr, -np.inf)
    return (jnp.asarray(slopes * LOG2E, F32), jnp.asarray(qaug, BF16), jnp.asarray(tdiag, F32))


def _first_key_tiles(nrm, batch, seq):
    tiles_per_seq = seq // PROJ_TM
    tsup = ATT_TQ * ATT_NS
    n = nrm.reshape(batch, tiles_per_seq, 2, SUBLANES // 2, LANES)[:, :, :, 0, :DIFF_HEADS * 2]
    n = n.reshape(batch, tiles_per_seq, 2, DIFF_HEADS, 2).max(axis=-1)
    q2 = n[:, :, 0].reshape(batch, seq // tsup, tsup // PROJ_TM, DIFF_HEADS).max(axis=2)
    k2 = n[:, :, 1].max(axis=1)
    bound = jnp.sqrt(q2 * k2[:, None, :]) * 1.02 + 0.05
    slope = jnp.asarray(_alibi_slopes() * LOG2E, F32)
    reach = (2.0 * bound + ATT_UNDERFLOW_LOG2) / slope
    needed = jnp.floor(jnp.maximum(reach - 1.0, 0.0) / ATT_TK).astype(jnp.int32) + 1
    n0 = (tsup // ATT_TK) * jnp.arange(seq // tsup, dtype=jnp.int32)[None, :, None]
    first = jnp.maximum(n0 - needed, 0) // ATT_UNROLL * ATT_UNROLL
    return first.transpose(0, 2, 1)


def _diff_attention(main, kp, vt, nrm, lam_p, subln_g, batch, seq):
    tq, tk, ns = ATT_TQ, ATT_TK, ATT_NS
    tsup = tq * ns
    nq = seq // tsup
    slopes, qaug, tdiag = _attn_tables(tq, tk)
    first = _first_key_tiles(nrm, batch, seq)
    return pl.pallas_call(
        functools.partial(_attn_kernel, tq=tq, tk=tk, ns=ns),
        grid=(batch, DIFF_HEADS, nq),
        in_specs=[
            pl.BlockSpec(memory_space=pltpu.SMEM),
            pl.BlockSpec(memory_space=pltpu.SMEM),
            pl.BlockSpec((tsup, LANES), lambda b, h, qi: (b * nq + qi, OFF_DQ // LANES + h)),
            pl.BlockSpec((1, tq, LANES), lambda b, h, qi: (h, 0, 0)),
            pl.BlockSpec((seq, 2 * LANES), lambda b, h, qi: (b, h)),
            pl.BlockSpec((VT_ROWS, seq), lambda b, h, qi: (h, b)),
            pl.BlockSpec((1,) + tdiag.shape[1:], lambda b, h, qi: (h, 0, 0, 0)),
            pl.BlockSpec(lam_p.shape, lambda b, h, qi: (0, 0)),
            pl.BlockSpec((1, LANES), lambda b, h, qi: (0, h)),
        ],
        out_specs=pl.BlockSpec((tsup, LANES), lambda b, h, qi: (b * nq + qi, h)),
        out_shape=jax.ShapeDtypeStruct((batch * seq, DIFF_HEADS * DIFF_V_DIM), BF16),
        scratch_shapes=[
            pltpu.VMEM((ns, 2 * LANES, 2 * tq), BF16),
            pltpu.VMEM((ns, 1, 2 * tq), F32),
            pltpu.VMEM((ns, VT_ROWS, 2 * tq), F32),
            pltpu.VMEM((ns, 2, tk, 2 * tq), F32),
            pltpu.VMEM((ns, 2, 1, 2 * tq), F32),
            pltpu.VMEM((ns, 2, tk, 2 * tq), BF16),
            pltpu.VMEM((ns, 2, 1, 2 * tq), F32),
        ],
        compiler_params=_cparams(("arbitrary", "arbitrary", "arbitrary")),
        name="diff_attn",
    )(slopes, first, main, qaug, kp, vt, tdiag, lam_p, subln_g)


def _retention_chunk(rows, q_ref, k_ref, v_ref, g_ref, dtab_ref, qdec_ref, kdec_ref, cdec_ref, gn_ref,
                     state_scr, r_scr):
    c = RET_C
    lane = lax.broadcasted_iota(jnp.int32, (c, LANES), 1)
    for p in range(RET_HEADS // 2):
        qp = q_ref[rows, p * LANES:(p + 1) * LANES]
        kp = k_ref[rows, p * LANES:(p + 1) * LANES]
        vp = v_ref[rows, p * 2 * RET_V_DIM:(p + 1) * 2 * RET_V_DIM]
        state = state_scr[p]
        state_b = state.astype(BF16)
        qdec = qdec_ref[:, p * LANES:(p + 1) * LANES]
        for e in range(2):
            head = 2 * p + e
            keep = (lane < RET_QK_DIM) if e == 0 else (lane >= RET_QK_DIM)
            qm = jnp.where(keep, qp, jnp.zeros_like(qp))
            s = lax.dot_general(qm, kp, (((1,), (1,)), ((), ())), preferred_element_type=F32)
            s = s * dtab_ref[head]
            ve = vp[:, e * RET_V_DIM:(e + 1) * RET_V_DIM]
            intra = jnp.dot(s.astype(BF16), ve, preferred_element_type=F32)
            qd = (qm.astype(F32) * qdec).astype(BF16)
            cross = jnp.dot(qd, state_b[:, e * RET_V_DIM:(e + 1) * RET_V_DIM],
                            preferred_element_type=F32)
            o = intra + cross
            mu = jnp.mean(o, axis=-1, keepdims=True)
            d = o - mu
            var = jnp.mean(d * d, axis=-1, keepdims=True)
            cols = slice(head * RET_V_DIM, (head + 1) * RET_V_DIM)
            gate = g_ref[rows, cols].astype(F32)
            y = d * lax.rsqrt(var + LN_EPS) * gn_ref[:, cols] * (gate * jax.nn.sigmoid(gate))
            r_scr[rows, cols] = y.astype(r_scr.dtype)
        kd = (kp.astype(F32) * kdec_ref[:, p * LANES:(p + 1) * LANES]).astype(BF16)
        kv = lax.dot_general(kd, vp, (((0,), (0,)), ((), ())), preferred_element_type=F32)
        state_scr[p] = state * cdec_ref[p] + kv


def _ret_tables(c):
    hh = np.arange(RET_HEADS, dtype=np.float64)
    lg = np.log(1.0 - 2.0 ** (-5.0 - hh))
    idx = np.arange(c, dtype=np.float64)
    nm = idx[:, None] - idx[None, :]
    dtab = np.where(nm >= 0, np.exp(lg[:, None, None] * np.maximum(nm, 0.0)), 0.0) * RET_QK_DIM ** -0.5
    qdec = np.exp(lg[None, :] * (idx + 1.0)[:, None])
    kdec = np.exp(lg[None, :] * (c - 1.0 - idx)[:, None]) * RET_QK_DIM ** -0.5
    qdec = np.repeat(qdec, RET_QK_DIM, axis=1)
    kdec = np.repeat(kdec, RET_QK_DIM, axis=1)
    cdec = np.repeat(np.exp(lg * c), RET_QK_DIM).reshape(RET_HEADS // 2, 2 * RET_QK_DIM, 1)
    return tuple(jnp.asarray(a, F32) for a in (dtab, qdec, kdec, cdec))


def _merge_kernel(x_ref, a_ref, rq_ref, rk_ref, rv_ref, rg_ref, mq_ref, mk_ref, mv_ref,
                  dtab_ref, qdec_ref, kdec_ref, cdec_ref, gn_ref,
                  wg_ref, bg_ref, wd_ref, wr_ref, wm_ref, wo_ref, g1_ref, b1_ref, o_ref,
                  state_scr, r_scr, m_scr, merged_scr, *, nc, tiles_per_seq):
    tm, d = x_ref.shape
    x = x_ref[...]
    xb = x.astype(BF16)

    @pl.when(pl.program_id(0) % tiles_per_seq == 0)
    def _():
        state_scr[...] = jnp.zeros(state_scr.shape, F32)

    for r0 in range(0, tm, RET_C):
        _retention_chunk(slice(r0, r0 + RET_C), rq_ref, rk_ref, rv_ref, rg_ref, dtab_ref, qdec_ref, kdec_ref,
                         cdec_ref, gn_ref, state_scr, r_scr)

    scale = jnp.asarray(MEM_HEAD_DIM ** -0.5, BF16)
    for h in range(MEM_HEADS):
        cols = slice(h * MEM_HEAD_DIM, (h + 1) * MEM_HEAD_DIM)
        qh = mq_ref[:, cols] * scale
        s = lax.dot_general(qh, mk_ref[0, :, cols], (((1,), (1,)), ((), ())),
                            preferred_element_type=F32)
        p = jnp.exp(s - jnp.max(s, axis=-1, keepdims=True))
        l = jnp.sum(p, axis=-1, keepdims=True)
        mh = jnp.dot(p.astype(BF16), mv_ref[0, :, cols], preferred_element_type=F32) / l
        m_scr[:, cols] = mh.astype(BF16)

    a = a_ref[...]
    r = r_scr[...]
    m = m_scr[...]
    for c in range(0, d, nc):
        cs = slice(c, c + nc)
        acc = None
        for br, (val, w_ref) in enumerate(((a, wd_ref), (r, wr_ref), (m, wm_ref))):
            gs = slice(br * d + c, br * d + c + nc)
            gate = jax.nn.sigmoid(jnp.dot(xb, wg_ref[:, gs], preferred_element_type=F32) + bg_ref[:, gs])
            term = gate * jnp.dot(val, w_ref[:, cs], preferred_element_type=F32)
            acc = term if acc is None else acc + term
        merged_scr[:, cs] = acc.astype(BF16)

    for r0 in range(0, tm, EPI_RB):
        rows = slice(r0, r0 + EPI_RB)
        mix = jnp.dot(merged_scr[rows, :], wo_ref[...], preferred_element_type=F32)
        o_ref[rows, :] = _layer_norm(DEEPNORM_ALPHA * x_ref[rows, :] + mix, g1_ref[...], b1_ref[...])


def _merge(x2, a, main, memkv, ret_g, wg, bg, wd, wr, wm, wo, g1, b1, batch, seq):
    tm = MERGE_TM
    t, d = x2.shape
    nt = seq // tm
    mlen = memkv.shape[1]
    mw = MEM_HEADS * MEM_HEAD_DIM
    qkw = RET_HEADS * RET_QK_DIM
    vw = RET_HEADS * RET_V_DIM
    dtab, qdec, kdec, cdec = _ret_tables(RET_C)
    row = lambda i: (i, 0)
    return pl.pallas_call(
        functools.partial(_merge_kernel, nc=MERGE_NC, tiles_per_seq=nt),
        grid=(t // tm,),
        in_specs=[
            pl.BlockSpec((tm, d), row),
            pl.BlockSpec((tm, a.shape[1]), row),
            pl.BlockSpec((tm, qkw), lambda i: (i, OFF_RQ // qkw)),
            pl.BlockSpec((tm, qkw), lambda i: (i, OFF_RK // qkw)),
            pl.BlockSpec((tm, vw), lambda i: (i, OFF_RV // vw)),
            pl.BlockSpec((tm, vw), lambda i: (i, OFF_RG // vw)),
            pl.BlockSpec((tm, mw), lambda i: (i, OFF_MQ // mw)),
            pl.BlockSpec((1, mlen, mw), lambda i: (i // nt, 0, 0)),
            pl.BlockSpec((1, mlen, mw), lambda i: (i // nt, 0, 1)),
            _resident(dtab.shape), _resident(qdec.shape), _resident(kdec.shape), _resident(cdec.shape),
            _resident(ret_g.shape),
            _resident(wg.shape), _resident(bg.shape), _resident(wd.shape), _resident(wr.shape),
            _resident(wm.shape), _resident(wo.shape), _resident(g1.shape), _resident(b1.shape),
        ],
        out_specs=pl.BlockSpec((tm, d), row),
        out_shape=jax.ShapeDtypeStruct((t, d), F32),
        scratch_shapes=[
            pltpu.VMEM((RET_HEADS // 2, 2 * RET_QK_DIM, 2 * RET_V_DIM), F32),
            pltpu.VMEM((tm, vw), BF16),
            pltpu.VMEM((tm, mw), BF16),
            pltpu.VMEM((tm, d), BF16),
        ],
        compiler_params=_cparams(("arbitrary",)),
        name="merge",
    )(x2, a, main, main, main, main, main, memkv, memkv, dtab, qdec, kdec, cdec, ret_g,
      wg, bg, wd, wr, wm, wo, g1, b1)


def _ffn_kernel(h_ref, wu_ref, cw_ref, cb_ref, wdn_ref, g2_ref, b2_ref, o_ref,
                carry_scr, hs_scr, act_scr, *, fc, rb, ffn_dim, tiles_per_seq):
    tm = h_ref.shape[0]
    halo = SUBLANES

    @pl.when(pl.program_id(0) % tiles_per_seq == 0)
    def _():
        carry_scr[...] = jnp.zeros(carry_scr.shape, F32)

    x = h_ref[...]
    xb = x.astype(BF16)

    def up_project(slot, col):
        cs = slice(col, col + fc)
        up = jnp.dot(xb, wu_ref[:, cs], preferred_element_type=F32)
        hs_scr[slot, 0:halo, :] = carry_scr[:, cs]
        hs_scr[slot, halo:halo + tm, :] = up
        carry_scr[:, cs] = up[tm - halo:tm, :]

    def conv_rows(slot, col, r0):
        cs = slice(col, col + fc)
        blk = hs_scr[slot, r0:r0 + rb + halo, :]
        w = cw_ref[:, cs]
        y = pltpu.roll(blk, 2, 0) * w[0:1] + pltpu.roll(blk, 1, 0) * w[1:2] + blk * w[2:3] + cb_ref[:, cs]
        return y[halo:, :]

    for j, c in enumerate(range(0, ffn_dim, fc)):
        sg, sv = 2 * (j % 2), 2 * (j % 2) + 1
        up_project(sg, c)
        up_project(sv, ffn_dim + c)
        for r0 in range(0, tm, rb):
            g = conv_rows(sg, c, r0)
            val = conv_rows(sv, ffn_dim + c, r0)
            act_scr[r0:r0 + rb, c:c + fc] = (g * jax.nn.sigmoid(g) * val).astype(BF16)

    for r0 in range(0, tm, EPI_RB):
        rows = slice(r0, r0 + EPI_RB)
        ffn = jnp.dot(act_scr[rows, :], wdn_ref[...], preferred_element_type=F32)
        o_ref[rows, :] = _layer_norm(DEEPNORM_ALPHA * h_ref[rows, :] + ffn, g2_ref[...], b2_ref[...])


def _ffn(h1, wu, cw, cb, wdn, g2, b2, seq):
    tm = FFN_TM
    t, d = h1.shape
    ffn_dim = wdn.shape[0]
    row = lambda i: (i, 0)
    return pl.pallas_call(
        functools.partial(_ffn_kernel, fc=FFN_FC, rb=FFN_RB, ffn_dim=ffn_dim, tiles_per_seq=seq // tm),
        grid=(t // tm,),
        in_specs=[
            pl.BlockSpec((tm, d), row),
            _resident(wu.shape), _resident(cw.shape), _resident(cb.shape), _resident(wdn.shape),
            _resident(g2.shape), _resident(b2.shape),
        ],
        out_specs=pl.BlockSpec((tm, d), row),
        out_shape=jax.ShapeDtypeStruct((t, d), F32),
        scratch_shapes=[
            pltpu.VMEM((SUBLANES, 2 * ffn_dim), F32),
            pltpu.VMEM((4, tm + SUBLANES, FFN_FC), F32),
            pltpu.VMEM((tm, ffn_dim), BF16),
        ],
        compiler_params=_cparams(("arbitrary",)),
        name="conv_ffn",
    )(h1, wu, cw, cb, wdn, g2, b2)


def kernel(x, mem, w_in, diff_lambda, diff_subln_g, ret_norm_g, w_mem_kv, w_diff_o, w_ret_o, w_mem_o,
           w_gate, b_gate, w_mix_out, ln1_g, ln1_b, w_up, conv_w, conv_b, w_down, ln2_g, ln2_b):
    batch, seq, d = x.shape
    mlen = mem.shape[1]
    assert w_in.shape[0] == DEPTH == 1
    assert seq % PROJ_TM == 0 and MERGE_TM % RET_C == 0 and seq % FFN_TM == 0 and seq % MERGE_TM == 0
    assert PROJ_TM % ATT_TK == 0 and ATT_TQ % CHUNK == 0 and ATT_TK % ATT_TQ == 0 and seq % (ATT_TQ * ATT_NS) == 0
    assert ATT_UNROLL % 2 == 0 and (ATT_NS * ATT_TQ // ATT_TK) % ATT_UNROLL == 0
    l = 0
    row = lambda v: v[l].reshape(1, -1)

    x2 = x.reshape(batch * seq, d)
    main, kp, vt, nrm = _inproj(x2, w_in[l])
    memkv = _proj(mem.reshape(batch * mlen, d), w_mem_kv[l].astype(BF16), batch * mlen, PROJ_NC)
    memkv = memkv.reshape(batch, mlen, 2 * MEM_HEADS * MEM_HEAD_DIM)

    a = _diff_attention(main, kp, vt, nrm, diff_lambda[l], row(diff_subln_g), batch, seq)
    h1 = _merge(x2, a, main, memkv, row(ret_norm_g), w_gate[l].astype(BF16), row(b_gate),
                w_diff_o[l].astype(BF16), w_ret_o[l].astype(BF16), w_mem_o[l].astype(BF16),
                w_mix_out[l].astype(BF16), row(ln1_g), row(ln1_b), batch, seq)
    out = _ffn(h1, w_up[l].astype(BF16), conv_w[l], row(conv_b), w_down[l].astype(BF16),
               row(ln2_g), row(ln2_b), seq)
    return out.reshape(batch, seq, d)
```

```python
import functools
import math

import numpy as np
import jax
import jax.numpy as jnp
from jax import lax
from jax.experimental import pallas as pl
from jax.experimental.pallas import tpu as pltpu

F32 = jnp.float32
BF16 = jnp.bfloat16

CHUNK = 64
DIFF_HEADS = 4
DIFF_QK_DIM = 64
DIFF_V_DIM = 128
RET_HEADS = 4
RET_QK_DIM = 64
RET_V_DIM = 128
MEM_HEADS = 4
MEM_HEAD_DIM = 128
N_BRANCHES = 3
CONV_WIDTH = 3
LN_EPS = 1e-5
DEPTH = 1
DEEPNORM_ALPHA = (2.0 * DEPTH) ** 0.25
LAMBDA_INIT = 0.8 - 0.6 * math.exp(-0.3 * 0)
LOG2E = math.log2(math.e)

IN_DQ, IN_DK, IN_DV = 0, 512, 1024
IN_RQ, IN_RK, IN_RV, IN_RG, IN_MQ, IN_END = 1536, 1792, 2048, 2560, 3072, 3584
OFF_DQ, OFF_RQ, OFF_RK, OFF_RV, OFF_RG, OFF_MQ, MAIN_W = 0, 512, 768, 1024, 1536, 2048, 2560

LANES = 128
SUBLANES = 8
BF16_ROWS = 16
VMEM_LIMIT = 56 * 1024 * 1024

PROJ_TM = 1024
PROJ_NC = 512
ATT_TQ = 256
ATT_TK = 512
ATT_NS = 4
ATT_UNROLL = 2
ATT_RB = 128
ATT_UNDERFLOW_LOG2 = 134.0
VT_ROWS = DIFF_V_DIM + BF16_ROWS
RET_C = 256
MERGE_TM = 512
MERGE_NC = 512
FFN_TM = 1024
FFN_FC = 256
FFN_RB = 128
EPI_RB = 256


def _cparams(sem):
    return pltpu.CompilerParams(dimension_semantics=sem, vmem_limit_bytes=VMEM_LIMIT)


def _resident(shape):
    nd = len(shape)
    return pl.BlockSpec(shape, lambda *_: (0,) * nd, pipeline_mode=pl.Buffered(1))


def _layer_norm(y, g, b):
    mu = jnp.mean(y, axis=-1, keepdims=True)
    d = y - mu
    var = jnp.mean(d * d, axis=-1, keepdims=True)
    return d * lax.rsqrt(var + LN_EPS) * g + b


def _alibi_slopes():
    return np.array([(2.0 ** (-8.0 / DIFF_HEADS)) ** (i + 1) for i in range(DIFF_HEADS)], np.float64)


def _bf16_split3(v):
    parts = []
    rem = np.float64(v)
    for _ in range(3):
        p = np.asarray(rem, np.float32).astype(jnp.bfloat16)
        parts.append(p)
        rem = rem - np.float64(p.astype(np.float32))
    return parts


def _proj_kernel(x_ref, w_ref, o_ref, *, nc):
    xb = x_ref[...].astype(BF16)
    n = w_ref.shape[1]
    for c in range(0, n, nc):
        o_ref[:, c:c + nc] = jnp.dot(xb, w_ref[:, c:c + nc],
                                     preferred_element_type=F32).astype(o_ref.dtype)


def _proj(x2, w, tm, nc):
    t, d = x2.shape
    n = w.shape[1]
    return pl.pallas_call(
        functools.partial(_proj_kernel, nc=nc),
        grid=(t // tm,),
        in_specs=[pl.BlockSpec((tm, d), lambda i: (i, 0)), _resident((d, n))],
        out_specs=pl.BlockSpec((tm, n), lambda i: (i, 0)),
        out_shape=jax.ShapeDtypeStruct((t, n), BF16),
        compiler_params=_cparams(("arbitrary",)),
        name="memkv",
    )(x2, w)


def _inproj_kernel(x_ref, wm_ref, wk_ref, wvt_ref, kaug_ref, sel_ref, main_ref, kp_ref, vt_ref, nrm_ref, *, nc):
    xb = x_ref[...].astype(BF16)
    qscale = LOG2E * DIFF_QK_DIM ** -0.5
    for c in range(0, MAIN_W, nc):
        acc = jnp.dot(xb, wm_ref[:, c:c + nc], preferred_element_type=F32)
        if c < OFF_RQ:
            acc = acc * qscale
        main_ref[:, c:c + nc] = acc.astype(main_ref.dtype)

    k = jnp.dot(xb, wk_ref[...], preferred_element_type=F32).astype(kp_ref.dtype)
    for h in range(DIFF_HEADS):
        kp_ref[:, h * 2 * LANES:h * 2 * LANES + LANES] = k[:, h * LANES:(h + 1) * LANES]
        kp_ref[:, h * 2 * LANES + LANES:(h + 1) * 2 * LANES] = kaug_ref[:, h * LANES:(h + 1) * LANES]

    def max_sq_norm(v):
        vf = v.astype(F32)
        sq = jnp.dot((vf * vf).astype(BF16), sel_ref[...], preferred_element_type=F32)
        return jnp.broadcast_to(jnp.max(sq, axis=0, keepdims=True), (SUBLANES // 2, LANES))

    nrm_ref[0] = jnp.concatenate([max_sq_norm(main_ref[:, OFF_DQ:OFF_RQ]), max_sq_norm(k)], axis=0)

    vt = lax.dot_general(wvt_ref[...], xb, (((1,), (1,)), ((), ())), preferred_element_type=F32)
    row = lax.broadcasted_iota(jnp.int32, vt.shape, 0)
    ones_row = row == DIFF_V_DIM
    for h in range(1, DIFF_HEADS):
        ones_row = ones_row | (row == h * VT_ROWS + DIFF_V_DIM)
    vt_ref[...] = jnp.where(ones_row, 1.0, vt).astype(vt_ref.dtype)


def _inproj(x2, w_in):
    t, d = x2.shape
    tm = PROJ_TM
    cols = lambda a, b: w_in[:, a:b]
    w_main = jnp.concatenate([cols(IN_DQ, IN_DK), cols(IN_RQ, IN_END)], axis=1).astype(BF16)
    w_k = cols(IN_DK, IN_DV).astype(BF16)
    wv = cols(IN_DV, IN_RQ).astype(BF16).T.reshape(DIFF_HEADS, DIFF_V_DIM, d)
    w_vt = jnp.pad(wv, ((0, 0), (0, VT_ROWS - DIFF_V_DIM), (0, 0))).reshape(DIFF_HEADS * VT_ROWS, d)

    kaug = _alibi_key_columns(tm, ATT_TK)
    sel = (np.arange(DIFF_HEADS * 2 * DIFF_QK_DIM)[:, None] // DIFF_QK_DIM == np.arange(LANES)[None, :])
    sel = jnp.asarray(sel, BF16)

    row = lambda i: (i, 0)
    return pl.pallas_call(
        functools.partial(_inproj_kernel, nc=PROJ_NC),
        grid=(t // tm,),
        in_specs=[pl.BlockSpec((tm, d), row), _resident(w_main.shape), _resident(w_k.shape),
                  _resident(w_vt.shape), _resident(kaug.shape), _resident(sel.shape)],
        out_specs=[pl.BlockSpec((tm, MAIN_W), row),
                   pl.BlockSpec((tm, DIFF_HEADS * 2 * LANES), row),
                   pl.BlockSpec((DIFF_HEADS * VT_ROWS, tm), lambda i: (0, i)),
                   pl.BlockSpec((1, SUBLANES, LANES), lambda i: (i, 0, 0))],
        out_shape=[jax.ShapeDtypeStruct((t, MAIN_W), BF16),
                   jax.ShapeDtypeStruct((t, DIFF_HEADS * 2 * LANES), BF16),
                   jax.ShapeDtypeStruct((DIFF_HEADS * VT_ROWS, t), BF16),
                   jax.ShapeDtypeStruct((t // tm, SUBLANES, LANES), F32)],
        compiler_params=_cparams(("arbitrary",)),
        name="inproj",
    )(x2, w_main, w_k, w_vt, kaug, sel)


def _attn_kernel(slope_ref, first_ref, q_ref, qaug_ref, kp_ref, vt_ref, tdiag_ref, lam_ref, g_ref, o_ref,
                 qs_scr, m_scr, acc_scr, z_scr, zmax_scr, p_scr, alpha_scr, *, tq, tk, ns):
    b, h, qi = pl.program_id(0), pl.program_id(1), pl.program_id(2)
    per_tile = tk // tq
    n0 = (ns // per_tile) * qi
    first = first_ref[b, h, qi]
    slope = slope_ref[h]

    @pl.when((b == 0) & (h == 0) & (qi == 0))
    def _():
        p_scr[...] = jnp.zeros(p_scr.shape, BF16)

    qa = qaug_ref[0]
    for j in range(ns):
        q = q_ref[j * tq:(j + 1) * tq, :]
        lane = lax.broadcasted_iota(jnp.int32, q.shape, 1)
        zero = jnp.zeros_like(q)
        q1 = jnp.concatenate([jnp.where(lane < DIFF_QK_DIM, q, zero), qa], axis=1).astype(F32)
        q2 = jnp.concatenate([jnp.where(lane >= DIFF_QK_DIM, q, zero), qa], axis=1).astype(F32)
        qs_scr[j, :, 0:tq] = q1.T.astype(BF16)
        qs_scr[j, :, tq:2 * tq] = q2.T.astype(BF16)
        m_scr[j] = jnp.full(m_scr.shape[1:], -jnp.inf, F32)
        acc_scr[j] = jnp.zeros(acc_scr.shape[1:], F32)
        alpha_scr[j, 1] = jnp.ones(alpha_scr.shape[2:], F32)

    def score_stage(j, kb, slot):
        start = pl.multiple_of(kb * tk, tk)
        z = jnp.dot(kp_ref[pl.ds(start, tk), :], qs_scr[j],
                    preferred_element_type=F32)
        zmax = None
        for r in range(0, tk, ATT_RB):
            zb = z[r:r + ATT_RB]
            z_scr[j, slot, r:r + ATT_RB] = zb
            bmax = jnp.max(zb, axis=0, keepdims=True)
            zmax = bmax if zmax is None else jnp.maximum(zmax, bmax)
        zmax_scr[j, slot] = zmax

    def softmax_stage(j, z, zmax, c, slot):
        m_prev = m_scr[j]
        m_new = jnp.maximum(m_prev, zmax + c)
        shift = m_new - c
        for r in range(0, tk, ATT_RB):
            p_scr[j, slot, r:r + ATT_RB] = jnp.exp2(z[r:r + ATT_RB] - shift).astype(BF16)
        alpha_scr[j, slot] = jnp.exp2(m_prev - m_new)
        m_scr[j] = m_new

    def value_stage(j, kb, slot):
        start = pl.multiple_of(jnp.maximum(kb, first) * tk, tk)
        v = vt_ref[:, pl.ds(start, tk)]
        v = jnp.where(kb >= first, v, jnp.zeros_like(v))
        acc_scr[j] = alpha_scr[j, slot] * acc_scr[j] + jnp.dot(v, p_scr[j, slot], preferred_element_type=F32)

    def tile_bias(j, kb):
        return -slope * ((n0 - kb) * tk + j * tq).astype(F32)

    def full_trip(kb, cur, nxt):
        for j in range(ns):
            score_stage(j, kb + 1, nxt)
            softmax_stage(j, z_scr.at[j, cur], zmax_scr[j, cur], tile_bias(j, kb), cur)
            value_stage(j, kb - 1, nxt)

    for j in range(ns):
        score_stage(j, first, 0)

    def unrolled(i, carry):
        for u in range(ATT_UNROLL):
            full_trip(ATT_UNROLL * i + u, u % 2, 1 - u % 2)
        return carry

    lax.fori_loop(first // ATT_UNROLL, n0 // ATT_UNROLL, unrolled, 0)

    for i in range(ns // per_tile + 1):
        kb = n0 + i
        cur, nxt = i % 2, 1 - i % 2
        for j in range(ns):
            last = j // per_tile
            if i < last:
                score_stage(j, kb + 1, nxt)
                softmax_stage(j, z_scr.at[j, cur], zmax_scr[j, cur], tile_bias(j, kb), cur)
            elif i == last:
                td = tdiag_ref[0, j % per_tile]
                z = z_scr[j, cur]
                z = jnp.concatenate([z[:, 0:tq] + td, z[:, tq:2 * tq] + td], axis=1)
                softmax_stage(j, z, jnp.max(z, axis=0, keepdims=True), tile_bias(j, kb), cur)
            if i <= last + 1:
                value_stage(j, kb - 1, nxt)

    lp = lam_ref[...]
    lam = (jnp.exp(jnp.sum(lp[0:1] * lp[1:2], axis=-1, keepdims=True))
           - jnp.exp(jnp.sum(lp[2:3] * lp[3:4], axis=-1, keepdims=True)) + LAMBDA_INIT)
    for j in range(ns):
        acc = acc_scr[j]
        o = acc[0:DIFF_V_DIM, :] / acc[DIFF_V_DIM:DIFF_V_DIM + 1, :]
        y = o[:, 0:tq] - lam * o[:, tq:2 * tq]
        y = y * lax.rsqrt(jnp.mean(y * y, axis=0, keepdims=True) + LN_EPS)
        o_ref[j * tq:(j + 1) * tq, :] = (y.T * g_ref[...] * (1.0 - LAMBDA_INIT)).astype(o_ref.dtype)


def _alibi_key_columns(n_rows, tk):
    l3 = _bf16_split3(LOG2E)
    off = np.arange(n_rows) % tk
    kaug = np.zeros((n_rows, DIFF_HEADS, LANES), np.float32)
    for hh, m in enumerate(_alibi_slopes()):
        for i in range(3):
            kaug[:, hh, i] = np.float32(l3[i])
            kaug[:, hh, 3 + i] = m * (off % ATT_TQ)
            kaug[:, hh, 6 + i] = m * (off - off % ATT_TQ)
    return jnp.asarray(kaug.reshape(n_rows, DIFF_HEADS * LANES), BF16)


def _attn_tables(tq, tk):
    slopes = _alibi_slopes()
    l3 = _bf16_split3(LOG2E)
    il = np.arange(tq, dtype=np.float64)
    qaug = np.zeros((DIFF_HEADS, tq, LANES), np.float32)
    for hh, m in enumerate(slopes):
        for i in range(3):
            qaug[hh, :, i] = -m * il
            qaug[hh, :, 3 + i] = np.float32(l3[i])
            qaug[hh, :, 6 + i] = np.float32(l3[i])
    j = np.arange(tk)[None, :, None]
    i = np.arange(tq)[None, None, :] + tq * np.arange(tk // tq)[:, None, None]
    allowed = (j // CHUNK) <= (i // CHUNK)
    corr = LOG2E * slopes[:, None, None, None] * ((i - j) - np.abs(i - j))[None]
    tdiag = np.where(allowed[None], corr, -np.inf)
    return (jnp.asarray(slopes * LOG2E, F32), jnp.asarray(qaug, BF16), jnp.asarray(tdiag, F32))


def _first_key_tiles(nrm, batch, seq):
    tiles_per_seq = seq // PROJ_TM
    tsup = ATT_TQ * ATT_NS
    n = nrm.reshape(batch, tiles_per_seq, 2, SUBLANES // 2, LANES)[:, :, :, 0, :DIFF_HEADS * 2]
    n = n.reshape(batch, tiles_per_seq, 2, DIFF_HEADS, 2).max(axis=-1)
    q2 = n[:, :, 0].reshape(batch, seq // tsup, tsup // PROJ_TM, DIFF_HEADS).max(axis=2)
    k2 = n[:, :, 1].max(axis=1)
    bound = jnp.sqrt(q2 * k2[:, None, :]) * 1.02 + 0.05
    slope = jnp.asarray(_alibi_slopes() * LOG2E, F32)
    reach = (2.0 * bound + ATT_UNDERFLOW_LOG2) / slope
    needed = jnp.floor(jnp.maximum(reach - 1.0, 0.0) / ATT_TK).astype(jnp.int32) + 1
    n0 = (tsup // ATT_TK) * jnp.arange(seq // tsup, dtype=jnp.int32)[None, :, None]
    first = jnp.maximum(n0 - needed, 0) // ATT_UNROLL * ATT_UNROLL
    return first.transpose(0, 2, 1)


def _diff_attention(main, kp, vt, nrm, lam_p, subln_g, batch, seq):
    tq, tk, ns = ATT_TQ, ATT_TK, ATT_NS
    tsup = tq * ns
    nq = seq // tsup
    slopes, qaug, tdiag = _attn_tables(tq, tk)
    first = _first_key_tiles(nrm, batch, seq)
    return pl.pallas_call(
        functools.partial(_attn_kernel, tq=tq, tk=tk, ns=ns),
        grid=(batch, DIFF_HEADS, nq),
        in_specs=[
            pl.BlockSpec(memory_space=pltpu.SMEM),
            pl.BlockSpec(memory_space=pltpu.SMEM),
            pl.BlockSpec((tsup, LANES), lambda b, h, qi: (b * nq + qi, OFF_DQ // LANES + h)),
            pl.BlockSpec((1, tq, LANES), lambda b, h, qi: (h, 0, 0)),
            pl.BlockSpec((seq, 2 * LANES), lambda b, h, qi: (b, h)),
            pl.BlockSpec((VT_ROWS, seq), lambda b, h, qi: (h, b)),
            pl.BlockSpec((1,) + tdiag.shape[1:], lambda b, h, qi: (h, 0, 0, 0)),
            pl.BlockSpec(lam_p.shape, lambda b, h, qi: (0, 0)),
            pl.BlockSpec((1, LANES), lambda b, h, qi: (0, h)),
        ],
        out_specs=pl.BlockSpec((tsup, LANES), lambda b, h, qi: (b * nq + qi, h)),
        out_shape=jax.ShapeDtypeStruct((batch * seq, DIFF_HEADS * DIFF_V_DIM), BF16),
        scratch_shapes=[
            pltpu.VMEM((ns, 2 * LANES, 2 * tq), BF16),
            pltpu.VMEM((ns, 1, 2 * tq), F32),
            pltpu.VMEM((ns, VT_ROWS, 2 * tq), F32),
            pltpu.VMEM((ns, 2, tk, 2 * tq), F32),
            pltpu.VMEM((ns, 2, 1, 2 * tq), F32),
            pltpu.VMEM((ns, 2, tk, 2 * tq), BF16),
            pltpu.VMEM((ns, 2, 1, 2 * tq), F32),
        ],
        compiler_params=_cparams(("arbitrary", "arbitrary", "arbitrary")),
        name="diff_attn",
    )(slopes, first, main, qaug, kp, vt, tdiag, lam_p, subln_g)


def _retention_chunk(rows, q_ref, k_ref, v_ref, g_ref, dtab_ref, qdec_ref, kdec_ref, cdec_ref, gn_ref,
                     state_scr, r_scr):
    c = RET_C
    lane = lax.broadcasted_iota(jnp.int32, (c, LANES), 1)
    for p in range(RET_HEADS // 2):
        qp = q_ref[rows, p * LANES:(p + 1) * LANES]
        kp = k_ref[rows, p * LANES:(p + 1) * LANES]
        vp = v_ref[rows, p * 2 * RET_V_DIM:(p + 1) * 2 * RET_V_DIM]
        state = state_scr[p]
        state_b = state.astype(BF16)
        qdec = qdec_ref[:, p * LANES:(p + 1) * LANES]
        for e in range(2):
            head = 2 * p + e
            keep = (lane < RET_QK_DIM) if e == 0 else (lane >= RET_QK_DIM)
            qm = jnp.where(keep, qp, jnp.zeros_like(qp))
            s = lax.dot_general(qm, kp, (((1,), (1,)), ((), ())), preferred_element_type=F32)
            s = s * dtab_ref[head]
            ve = vp[:, e * RET_V_DIM:(e + 1) * RET_V_DIM]
            intra = jnp.dot(s.astype(BF16), ve, preferred_element_type=F32)
            qd = (qm.astype(F32) * qdec).astype(BF16)
            cross = jnp.dot(qd, state_b[:, e * RET_V_DIM:(e + 1) * RET_V_DIM],
                            preferred_element_type=F32)
            o = intra + cross
            mu = jnp.mean(o, axis=-1, keepdims=True)
            d = o - mu
            var = jnp.mean(d * d, axis=-1, keepdims=True)
            cols = slice(head * RET_V_DIM, (head + 1) * RET_V_DIM)
            gate = g_ref[rows, cols].astype(F32)
            y = d * lax.rsqrt(var + LN_EPS) * gn_ref[:, cols] * (gate * jax.nn.sigmoid(gate))
            r_scr[rows, cols] = y.astype(r_scr.dtype)
        kd = (kp.astype(F32) * kdec_ref[:, p * LANES:(p + 1) * LANES]).astype(BF16)
        kv = lax.dot_general(kd, vp, (((0,), (0,)), ((), ())), preferred_element_type=F32)
        state_scr[p] = state * cdec_ref[p] + kv


def _ret_tables(c):
    hh = np.arange(RET_HEADS, dtype=np.float64)
    lg = np.log(1.0 - 2.0 ** (-5.0 - hh))
    idx = np.arange(c, dtype=np.float64)
    nm = idx[:, None] - idx[None, :]
    dtab = np.where(nm >= 0, np.exp(lg[:, None, None] * np.maximum(nm, 0.0)), 0.0) * RET_QK_DIM ** -0.5
    qdec = np.exp(lg[None, :] * (idx + 1.0)[:, None])
    kdec = np.exp(lg[None, :] * (c - 1.0 - idx)[:, None]) * RET_QK_DIM ** -0.5
    qdec = np.repeat(qdec, RET_QK_DIM, axis=1)
    kdec = np.repeat(kdec, RET_QK_DIM, axis=1)
    cdec = np.repeat(np.exp(lg * c), RET_QK_DIM).reshape(RET_HEADS // 2, 2 * RET_QK_DIM, 1)
    return tuple(jnp.asarray(a, F32) for a in (dtab, qdec, kdec, cdec))


def _merge_kernel(x_ref, a_ref, rq_ref, rk_ref, rv_ref, rg_ref, mq_ref, mk_ref, mv_ref,
                  dtab_ref, qdec_ref, kdec_ref, cdec_ref, gn_ref,
                  wg_ref, bg_ref, wd_ref, wr_ref, wm_ref, wo_ref, g1_ref, b1_ref, o_ref,
                  state_scr, r_scr, m_scr, merged_scr, *, nc, tiles_per_seq):
    tm, d = x_ref.shape
    x = x_ref[...]
    xb = x.astype(BF16)

    @pl.when(pl.program_id(0) % tiles_per_seq == 0)
    def _():
        state_scr[...] = jnp.zeros(state_scr.shape, F32)

    for r0 in range(0, tm, RET_C):
        _retention_chunk(slice(r0, r0 + RET_C), rq_ref, rk_ref, rv_ref, rg_ref, dtab_ref, qdec_ref, kdec_ref,
                         cdec_ref, gn_ref, state_scr, r_scr)

    scale = jnp.asarray(MEM_HEAD_DIM ** -0.5, BF16)
    for h in range(MEM_HEADS):
        cols = slice(h * MEM_HEAD_DIM, (h + 1) * MEM_HEAD_DIM)
        qh = mq_ref[:, cols] * scale
        s = lax.dot_general(qh, mk_ref[0, :, cols], (((1,), (1,)), ((), ())),
                            preferred_element_type=F32)
        p = jnp.exp(s - jnp.max(s, axis=-1, keepdims=True))
        l = jnp.sum(p, axis=-1, keepdims=True)
        mh = jnp.dot(p.astype(BF16), mv_ref[0, :, cols], preferred_element_type=F32) / l
        m_scr[:, cols] = mh.astype(BF16)

    a = a_ref[...]
    r = r_scr[...]
    m = m_scr[...]
    for c in range(0, d, nc):
        cs = slice(c, c + nc)
        acc = None
        for br, (val, w_ref) in enumerate(((a, wd_ref), (r, wr_ref), (m, wm_ref))):
            gs = slice(br * d + c, br * d + c + nc)
            gate = jax.nn.sigmoid(jnp.dot(xb, wg_ref[:, gs], preferred_element_type=F32) + bg_ref[:, gs])
            term = gate * jnp.dot(val, w_ref[:, cs], preferred_element_type=F32)
            acc = term if acc is None else acc + term
        merged_scr[:, cs] = acc.astype(BF16)

    for r0 in range(0, tm, EPI_RB):
        rows = slice(r0, r0 + EPI_RB)
        mix = jnp.dot(merged_scr[rows, :], wo_ref[...], preferred_element_type=F32)
        o_ref[rows, :] = _layer_norm(DEEPNORM_ALPHA * x_ref[rows, :] + mix, g1_ref[...], b1_ref[...])


def _merge(x2, a, main, memkv, ret_g, wg, bg, wd, wr, wm, wo, g1, b1, batch, seq):
    tm = MERGE_TM
    t, d = x2.shape
    nt = seq // tm
    mlen = memkv.shape[1]
    mw = MEM_HEADS * MEM_HEAD_DIM
    qkw = RET_HEADS * RET_QK_DIM
    vw = RET_HEADS * RET_V_DIM
    dtab, qdec, kdec, cdec = _ret_tables(RET_C)
    row = lambda i: (i, 0)
    return pl.pallas_call(
        functools.partial(_merge_kernel, nc=MERGE_NC, tiles_per_seq=nt),
        grid=(t // tm,),
        in_specs=[
            pl.BlockSpec((tm, d), row),
            pl.BlockSpec((tm, a.shape[1]), row),
            pl.BlockSpec((tm, qkw), lambda i: (i, OFF_RQ // qkw)),
            pl.BlockSpec((tm, qkw), lambda i: (i, OFF_RK // qkw)),
            pl.BlockSpec((tm, vw), lambda i: (i, OFF_RV // vw)),
            pl.BlockSpec((tm, vw), lambda i: (i, OFF_RG // vw)),
            pl.BlockSpec((tm, mw), lambda i: (i, OFF_MQ // mw)),
            pl.BlockSpec((1, mlen, mw), lambda i: (i // nt, 0, 0)),
            pl.BlockSpec((1, mlen, mw), lambda i: (i // nt, 0, 1)),
            _resident(dtab.shape), _resident(qdec.shape), _resident(kdec.shape), _resident(cdec.shape),
            _resident(ret_g.shape),
            _resident(wg.shape), _resident(bg.shape), _resident(wd.shape), _resident(wr.shape),
            _resident(wm.shape), _resident(wo.shape), _resident(g1.shape), _resident(b1.shape),
        ],
        out_specs=pl.BlockSpec((tm, d), row),
        out_shape=jax.ShapeDtypeStruct((t, d), F32),
        scratch_shapes=[
            pltpu.VMEM((RET_HEADS // 2, 2 * RET_QK_DIM, 2 * RET_V_DIM), F32),
            pltpu.VMEM((tm, vw), BF16),
            pltpu.VMEM((tm, mw), BF16),
            pltpu.VMEM((tm, d), BF16),
        ],
        compiler_params=_cparams(("arbitrary",)),
        name="merge",
    )(x2, a, main, main, main, main, main, memkv, memkv, dtab, qdec, kdec, cdec, ret_g,
      wg, bg, wd, wr, wm, wo, g1, b1)


def _ffn_kernel(h_ref, wu_ref, cw_ref, cb_ref, wdn_ref, g2_ref, b2_ref, o_ref,
                carry_scr, hs_scr, act_scr, *, fc, rb, ffn_dim, tiles_per_seq):
    tm = h_ref.shape[0]
    halo = SUBLANES

    @pl.when(pl.program_id(0) % tiles_per_seq == 0)
    def _():
        carry_scr[...] = jnp.zeros(carry_scr.shape, F32)

    x = h_ref[...]
    xb = x.astype(BF16)

    def up_project(slot, col):
        cs = slice(col, col + fc)
        up = jnp.dot(xb, wu_ref[:, cs], preferred_element_type=F32)
        hs_scr[slot, 0:halo, :] = carry_scr[:, cs]
        hs_scr[slot, halo:halo + tm, :] = up
        carry_scr[:, cs] = up[tm - halo:tm, :]

    def conv_rows(slot, col, r0):
        cs = slice(col, col + fc)
        blk = hs_scr[slot, r0:r0 + rb + halo, :]
        w = cw_ref[:, cs]
        y = pltpu.roll(blk, 2, 0) * w[0:1] + pltpu.roll(blk, 1, 0) * w[1:2] + blk * w[2:3] + cb_ref[:, cs]
        return y[halo:, :]

    for j, c in enumerate(range(0, ffn_dim, fc)):
        sg, sv = 2 * (j % 2), 2 * (j % 2) + 1
        up_project(sg, c)
        up_project(sv, ffn_dim + c)
        for r0 in range(0, tm, rb):
            g = conv_rows(sg, c, r0)
            val = conv_rows(sv, ffn_dim + c, r0)
            act_scr[r0:r0 + rb, c:c + fc] = (g * jax.nn.sigmoid(g) * val).astype(BF16)

    for r0 in range(0, tm, EPI_RB):
        rows = slice(r0, r0 + EPI_RB)
        ffn = jnp.dot(act_scr[rows, :], wdn_ref[...], preferred_element_type=F32)
        o_ref[rows, :] = _layer_norm(DEEPNORM_ALPHA * h_ref[rows, :] + ffn, g2_ref[...], b2_ref[...])


def _ffn(h1, wu, cw, cb, wdn, g2, b2, seq):
    tm = FFN_TM
    t, d = h1.shape
    ffn_dim = wdn.shape[0]
    row = lambda i: (i, 0)
    return pl.pallas_call(
        functools.partial(_ffn_kernel, fc=FFN_FC, rb=FFN_RB, ffn_dim=ffn_dim, tiles_per_seq=seq // tm),
        grid=(t // tm,),
        in_specs=[
            pl.BlockSpec((tm, d), row),
            _resident(wu.shape), _resident(cw.shape), _resident(cb.shape), _resident(wdn.shape),
            _resident(g2.shape), _resident(b2.shape),
        ],
        out_specs=pl.BlockSpec((tm, d), row),
        out_shape=jax.ShapeDtypeStruct((t, d), F32),
        scratch_shapes=[
            pltpu.VMEM((SUBLANES, 2 * ffn_dim), F32),
            pltpu.VMEM((4, tm + SUBLANES, FFN_FC), F32),
            pltpu.VMEM((tm, ffn_dim), BF16),
        ],
        compiler_params=_cparams(("arbitrary",)),
        name="conv_ffn",
    )(h1, wu, cw, cb, wdn, g2, b2)


def kernel(x, mem, w_in, diff_lambda, diff_subln_g, ret_norm_g, w_mem_kv, w_diff_o, w_ret_o, w_mem_o,
           w_gate, b_gate, w_mix_out, ln1_g, ln1_b, w_up, conv_w, conv_b, w_down, ln2_g, ln2_b):
    batch, seq, d = x.shape
    mlen = mem.shape[1]
    assert w_in.shape[0] == DEPTH == 1
    assert seq % PROJ_TM == 0 and MERGE_TM % RET_C == 0 and seq % FFN_TM == 0 and seq % MERGE_TM == 0
    assert PROJ_TM % ATT_TK == 0 and ATT_TQ % CHUNK == 0 and ATT_TK % ATT_TQ == 0 and seq % (ATT_TQ * ATT_NS) == 0
    assert ATT_UNROLL % 2 == 0 and (ATT_NS * ATT_TQ // ATT_TK) % ATT_UNROLL == 0
    l = 0
    row = lambda v: v[l].reshape(1, -1)

    x2 = x.reshape(batch * seq, d)
    main, kp, vt, nrm = _inproj(x2, w_in[l])
    memkv = _proj(mem.reshape(batch * mlen, d), w_mem_kv[l].astype(BF16), batch * mlen, PROJ_NC)
    memkv = memkv.reshape(batch, mlen, 2 * MEM_HEADS * MEM_HEAD_DIM)

    a = _diff_attention(main, kp, vt, nrm, diff_lambda[l], row(diff_subln_g), batch, seq)
    h1 = _merge(x2, a, main, memkv, row(ret_norm_g), w_gate[l].astype(BF16), row(b_gate),
                w_diff_o[l].astype(BF16), w_ret_o[l].astype(BF16), w_mem_o[l].astype(BF16),
                w_mix_out[l].astype(BF16), row(ln1_g), row(ln1_b), batch, seq)
    out = _ffn(h1, w_up[l].astype(BF16), conv_w[l], row(conv_b), w_down[l].astype(BF16),
               row(ln2_g), row(ln2_b), seq)
    return out.reshape(batch, seq, d)
```

```python
import functools
import math

import numpy as np
import jax
import jax.numpy as jnp
from jax import lax
from jax.experimental import pallas as pl
from jax.experimental.pallas import tpu as pltpu

F32 = jnp.float32
BF16 = jnp.bfloat16

CHUNK = 64
DIFF_HEADS = 4
DIFF_QK_DIM = 64
DIFF_V_DIM = 128
RET_HEADS = 4
RET_QK_DIM = 64
RET_V_DIM = 128
MEM_HEADS = 4
MEM_HEAD_DIM = 128
CONV_WIDTH = 3
LN_EPS = 1e-5
DEPTH = 1
DEEPNORM_ALPHA = (2.0 * DEPTH) ** 0.25
LAMBDA_INIT = 0.8 - 0.6 * math.exp(-0.3 * 0)
LOG2E = math.log2(math.e)

IN_DQ, IN_DK, IN_DV = 0, 512, 1024
IN_RQ, IN_END = 1536, 3584
OFF_DQ, OFF_RQ, OFF_RK, OFF_RV, OFF_RG, OFF_MQ, MAIN_W = 0, 512, 768, 1024, 1536, 2048, 2560

LANES = 128
SUBLANES = 8
BF16_ROWS = 16
VMEM_LIMIT = 56 * 1024 * 1024

PROJ_TM = 512
PROJ_NC = 512
ATT_TQ = 256
ATT_TK = 512
ATT_NS = 4
ATT_UNROLL = 2
ATT_RB = 128
ATT_UNDERFLOW_LOG2 = 134.0
ATT_BOUND_SCALE = 1.02
ATT_BOUND_SLACK = 0.05
VT_ROWS = DIFF_V_DIM + BF16_ROWS
RET_C = 256
MERGE_TM = 512
MERGE_NC = 512
FFN_TM = 512
FFN_FC = 256
FFN_RB = 128
EPI_RB = 256


def _cparams(sem):
    return pltpu.CompilerParams(dimension_semantics=sem, vmem_limit_bytes=VMEM_LIMIT)


def _resident(shape):
    nd = len(shape)
    return pl.BlockSpec(shape, lambda *_: (0,) * nd, pipeline_mode=pl.Buffered(1))


def _layer_norm(y, g, b):
    mu = jnp.mean(y, axis=-1, keepdims=True)
    d = y - mu
    var = jnp.mean(d * d, axis=-1, keepdims=True)
    return d * lax.rsqrt(var + LN_EPS) * g + b


def _alibi_slopes():
    return np.array([(2.0 ** (-8.0 / DIFF_HEADS)) ** (i + 1) for i in range(DIFF_HEADS)], np.float64)


def _bf16_split3(v):
    parts = []
    rem = np.float64(v)
    for _ in range(3):
        p = np.asarray(rem, np.float32).astype(jnp.bfloat16)
        parts.append(p)
        rem = rem - np.float64(p.astype(np.float32))
    return parts


def _proj_kernel(x_ref, w_ref, o_ref, *, nc):
    xb = x_ref[...].astype(BF16)
    n = w_ref.shape[1]
    for c in range(0, n, nc):
        o_ref[:, c:c + nc] = jnp.dot(xb, w_ref[:, c:c + nc],
                                     preferred_element_type=F32).astype(o_ref.dtype)


def _proj(x2, w, tm, nc):
    t, d = x2.shape
    n = w.shape[1]
    return pl.pallas_call(
        functools.partial(_proj_kernel, nc=nc),
        grid=(t // tm,),
        in_specs=[pl.BlockSpec((tm, d), lambda i: (i, 0)), _resident((d, n))],
        out_specs=pl.BlockSpec((tm, n), lambda i: (i, 0)),
        out_shape=jax.ShapeDtypeStruct((t, n), BF16),
        compiler_params=_cparams(("arbitrary",)),
        name="memkv",
    )(x2, w)


def _inproj_kernel(x_ref, wm_ref, wk_ref, wvt_ref, kaug_ref, sel_ref, main_ref, kp_ref, vt_ref, nrm_ref, *, nc):
    xb = x_ref[...].astype(BF16)
    qscale = LOG2E * DIFF_QK_DIM ** -0.5
    for c in range(0, MAIN_W, nc):
        acc = jnp.dot(xb, wm_ref[:, c:c + nc], preferred_element_type=F32)
        if c < OFF_RQ:
            acc = acc * qscale
        main_ref[:, c:c + nc] = acc.astype(main_ref.dtype)

    k = jnp.dot(xb, wk_ref[...], preferred_element_type=F32).astype(kp_ref.dtype)
    for h in range(DIFF_HEADS):
        kp_ref[:, h * 2 * LANES:h * 2 * LANES + LANES] = k[:, h * LANES:(h + 1) * LANES]
        kp_ref[:, h * 2 * LANES + LANES:(h + 1) * 2 * LANES] = kaug_ref[:, h * LANES:(h + 1) * LANES]

    def max_sq_norm(v):
        vf = v.astype(F32)
        sq = jnp.dot((vf * vf).astype(BF16), sel_ref[...], preferred_element_type=F32)
        return jnp.broadcast_to(jnp.max(sq, axis=0, keepdims=True), (SUBLANES // 2, LANES))

    nrm_ref[0] = jnp.concatenate([max_sq_norm(main_ref[:, OFF_DQ:OFF_RQ]), max_sq_norm(k)], axis=0)

    vt = lax.dot_general(wvt_ref[...], xb, (((1,), (1,)), ((), ())), preferred_element_type=F32)
    row = lax.broadcasted_iota(jnp.int32, vt.shape, 0)
    ones_row = row == DIFF_V_DIM
    for h in range(1, DIFF_HEADS):
        ones_row = ones_row | (row == h * VT_ROWS + DIFF_V_DIM)
    vt_ref[...] = jnp.where(ones_row, 1.0, vt).astype(vt_ref.dtype)


def _inproj(x2, w_in):
    t, d = x2.shape
    tm = PROJ_TM
    cols = lambda a, b: w_in[:, a:b]
    w_main = jnp.concatenate([cols(IN_DQ, IN_DK), cols(IN_RQ, IN_END)], axis=1).astype(BF16)
    w_k = cols(IN_DK, IN_DV).astype(BF16)
    wv = cols(IN_DV, IN_RQ).astype(BF16).T.reshape(DIFF_HEADS, DIFF_V_DIM, d)
    w_vt = jnp.pad(wv, ((0, 0), (0, VT_ROWS - DIFF_V_DIM), (0, 0))).reshape(DIFF_HEADS * VT_ROWS, d)

    kaug = _alibi_key_columns(tm, ATT_TK)
    sel = (np.arange(DIFF_HEADS * 2 * DIFF_QK_DIM)[:, None] // DIFF_QK_DIM == np.arange(LANES)[None, :])
    sel = jnp.asarray(sel, BF16)

    row = lambda i: (i, 0)
    return pl.pallas_call(
        functools.partial(_inproj_kernel, nc=PROJ_NC),
        grid=(t // tm,),
        in_specs=[pl.BlockSpec((tm, d), row), _resident(w_main.shape), _resident(w_k.shape),
                  _resident(w_vt.shape), _resident(kaug.shape), _resident(sel.shape)],
        out_specs=[pl.BlockSpec((tm, MAIN_W), row),
                   pl.BlockSpec((tm, DIFF_HEADS * 2 * LANES), row),
                   pl.BlockSpec((DIFF_HEADS * VT_ROWS, tm), lambda i: (0, i)),
                   pl.BlockSpec((1, SUBLANES, LANES), lambda i: (i, 0, 0))],
        out_shape=[jax.ShapeDtypeStruct((t, MAIN_W), BF16),
                   jax.ShapeDtypeStruct((t, DIFF_HEADS * 2 * LANES), BF16),
                   jax.ShapeDtypeStruct((DIFF_HEADS * VT_ROWS, t), BF16),
                   jax.ShapeDtypeStruct((t // tm, SUBLANES, LANES), F32)],
        compiler_params=_cparams(("arbitrary",)),
        name="inproj",
    )(x2, w_main, w_k, w_vt, kaug, sel)


def _attn_kernel(slope_ref, first_ref, q_ref, qaug_ref, kp_ref, vt_ref, tdiag_ref, lam_ref, g_ref, o_ref,
                 qs_scr, m_scr, acc_scr, z_scr, zmax_scr, p_scr, alpha_scr, *, tq, tk, ns):
    b, h, qi = pl.program_id(0), pl.program_id(1), pl.program_id(2)
    per_tile = tk // tq
    n0 = (ns // per_tile) * qi
    first = first_ref[b, h, qi]
    slope = slope_ref[h]

    @pl.when((b == 0) & (h == 0) & (qi == 0))
    def _():
        p_scr[...] = jnp.zeros(p_scr.shape, BF16)

    qa = qaug_ref[0]
    for j in range(ns):
        q = q_ref[j * tq:(j + 1) * tq, :]
        lane = lax.broadcasted_iota(jnp.int32, q.shape, 1)
        zero = jnp.zeros_like(q)
        q1 = jnp.concatenate([jnp.where(lane < DIFF_QK_DIM, q, zero), qa], axis=1).astype(F32)
        q2 = jnp.concatenate([jnp.where(lane >= DIFF_QK_DIM, q, zero), qa], axis=1).astype(F32)
        qs_scr[j, :, 0:tq] = q1.T.astype(BF16)
        qs_scr[j, :, tq:2 * tq] = q2.T.astype(BF16)
        m_scr[j] = jnp.full(m_scr.shape[1:], -jnp.inf, F32)
        acc_scr[j] = jnp.zeros(acc_scr.shape[1:], F32)
        alpha_scr[j, 1] = jnp.ones(alpha_scr.shape[2:], F32)

    def score_stage(j, kb, slot, rows=tk):
        start = pl.multiple_of(kb * tk, tk)
        z = jnp.dot(kp_ref[pl.ds(start, rows), :], qs_scr[j],
                    preferred_element_type=F32)
        zmax = None
        for r in range(0, rows, ATT_RB):
            zb = z[r:r + ATT_RB]
            z_scr[j, slot, r:r + ATT_RB] = zb
            bmax = jnp.max(zb, axis=0, keepdims=True)
            zmax = bmax if zmax is None else jnp.maximum(zmax, bmax)
        zmax_scr[j, slot] = zmax

    def softmax_stage(j, z, zmax, c, slot, rows=tk):
        m_prev = m_scr[j]
        m_new = jnp.maximum(m_prev, zmax + c)
        shift = m_new - c
        for r in range(0, rows, ATT_RB):
            p_scr[j, slot, r:r + ATT_RB] = jnp.exp2(z[r:r + ATT_RB] - shift).astype(BF16)
        alpha_scr[j, slot] = jnp.exp2(m_prev - m_new)
        m_scr[j] = m_new

    def value_stage(j, kb, slot, rows=tk):
        start = pl.multiple_of(jnp.maximum(kb, first) * tk, tk)
        v = vt_ref[:, pl.ds(start, rows)]
        v = jnp.where(kb >= first, v, jnp.zeros_like(v))
        acc_scr[j] = alpha_scr[j, slot] * acc_scr[j] + jnp.dot(v, p_scr[j, slot, 0:rows],
                                                               preferred_element_type=F32)

    def tile_bias(j, kb):
        return -slope * ((n0 - kb) * tk + j * tq).astype(F32)

    def full_trip(kb, cur, nxt):
        for j in range(ns):
            score_stage(j, kb + 1, nxt)
            softmax_stage(j, z_scr.at[j, cur], zmax_scr[j, cur], tile_bias(j, kb), cur)
            value_stage(j, kb - 1, nxt)

    for j in range(ns):
        score_stage(j, first, 0)

    def unrolled(i, carry):
        for u in range(ATT_UNROLL):
            full_trip(ATT_UNROLL * i + u, u % 2, 1 - u % 2)
        return carry

    lax.fori_loop(first // ATT_UNROLL, n0 // ATT_UNROLL, unrolled, 0)

    for i in range(ns // per_tile + 1):
        kb = n0 + i
        cur, nxt = i % 2, 1 - i % 2
        for j in range(ns):
            last = j // per_tile
            seen = (j % per_tile + 1) * tq
            if i < last:
                score_stage(j, kb + 1, nxt, rows=seen if i + 1 == last else tk)
                softmax_stage(j, z_scr.at[j, cur], zmax_scr[j, cur], tile_bias(j, kb), cur)
            elif i == last:
                td = tdiag_ref[0, j % per_tile, 0:seen, :]
                z = z_scr[j, cur, 0:seen]
                z = jnp.concatenate([z[:, 0:tq] + td, z[:, tq:2 * tq] + td], axis=1)
                softmax_stage(j, z, jnp.max(z, axis=0, keepdims=True), tile_bias(j, kb), cur, rows=seen)
            if i <= last + 1:
                value_stage(j, kb - 1, nxt, rows=seen if i == last + 1 else tk)

    lp = lam_ref[...]
    lam = (jnp.exp(jnp.sum(lp[0:1] * lp[1:2], axis=-1, keepdims=True))
           - jnp.exp(jnp.sum(lp[2:3] * lp[3:4], axis=-1, keepdims=True)) + LAMBDA_INIT)
    for j in range(ns):
        acc = acc_scr[j]
        o = acc[0:DIFF_V_DIM, :] / acc[DIFF_V_DIM:DIFF_V_DIM + 1, :]
        y = o[:, 0:tq] - lam * o[:, tq:2 * tq]
        y = y * lax.rsqrt(jnp.mean(y * y, axis=0, keepdims=True) + LN_EPS)
        o_ref[j * tq:(j + 1) * tq, :] = (y.T * g_ref[...] * (1.0 - LAMBDA_INIT)).astype(o_ref.dtype)


def _alibi_key_columns(n_rows, tk):
    l3 = _bf16_split3(LOG2E)
    off = np.arange(n_rows) % tk
    kaug = np.zeros((n_rows, DIFF_HEADS, LANES), np.float32)
    for hh, m in enumerate(_alibi_slopes()):
        for i in range(3):
            kaug[:, hh, i] = np.float32(l3[i])
            kaug[:, hh, 3 + i] = m * (off % ATT_TQ)
            kaug[:, hh, 6 + i] = m * (off - off % ATT_TQ)
    return jnp.asarray(kaug.reshape(n_rows, DIFF_HEADS * LANES), BF16)


def _attn_tables(tq, tk):
    slopes = _alibi_slopes()
    l3 = _bf16_split3(LOG2E)
    il = np.arange(tq, dtype=np.float64)
    qaug = np.zeros((DIFF_HEADS, tq, LANES), np.float32)
    for hh, m in enumerate(slopes):
        for i in range(3):
            qaug[hh, :, i] = -m * il
            qaug[hh, :, 3 + i] = np.float32(l3[i])
            qaug[hh, :, 6 + i] = np.float32(l3[i])
    j = np.arange(tk)[None, :, None]
    i = np.arange(tq)[None, None, :] + tq * np.arange(tk // tq)[:, None, None]
    allowed = (j // CHUNK) <= (i // CHUNK)
    corr = LOG2E * slopes[:, None, None, None] * ((i - j) - np.abs(i - j))[None]
    tdiag = np.where(allowed[None], corr, -np.inf)
    return (jnp.asarray(slopes * LOG2E, F32), jnp.asarray(qaug, BF16), jnp.asarray(tdiag, F32))


def _first_key_tiles(nrm, batch, seq):
    tiles_per_seq = seq // PROJ_TM
    tsup = ATT_TQ * ATT_NS
    n = nrm.reshape(batch, tiles_per_seq, 2, SUBLANES // 2, LANES)[:, :, :, 0, :DIFF_HEADS * 2]
    n = n.reshape(batch, tiles_per_seq, 2, DIFF_HEADS, 2).max(axis=-1)
    q2 = n[:, :, 0].reshape(batch, seq // tsup, tsup // PROJ_TM, DIFF_HEADS).max(axis=2)
    k2 = n[:, :, 1].max(axis=1)
    bound = jnp.sqrt(q2 * k2[:, None, :]) * ATT_BOUND_SCALE + ATT_BOUND_SLACK
    slope = jnp.asarray(_alibi_slopes() * LOG2E, F32)
    reach = (2.0 * bound + ATT_UNDERFLOW_LOG2) / slope
    needed = jnp.floor(jnp.maximum(reach - 1.0, 0.0) / ATT_TK).astype(jnp.int32) + 1
    n0 = (tsup // ATT_TK) * jnp.arange(seq // tsup, dtype=jnp.int32)[None, :, None]
    first = jnp.maximum(n0 - needed, 0) // ATT_UNROLL * ATT_UNROLL
    return first.transpose(0, 2, 1)


def _diff_attention(main, kp, vt, nrm, lam_p, subln_g, batch, seq):
    tq, tk, ns = ATT_TQ, ATT_TK, ATT_NS
    tsup = tq * ns
    nq = seq // tsup
    slopes, qaug, tdiag = _attn_tables(tq, tk)
    first = _first_key_tiles(nrm, batch, seq)
    return pl.pallas_call(
        functools.partial(_attn_kernel, tq=tq, tk=tk, ns=ns),
        grid=(batch, DIFF_HEADS, nq),
        in_specs=[
            pl.BlockSpec(memory_space=pltpu.SMEM),
            pl.BlockSpec(memory_space=pltpu.SMEM),
            pl.BlockSpec((tsup, LANES), lambda b, h, qi: (b * nq + qi, OFF_DQ // LANES + h)),
            pl.BlockSpec((1, tq, LANES), lambda b, h, qi: (h, 0, 0)),
            pl.BlockSpec((seq, 2 * LANES), lambda b, h, qi: (b, h)),
            pl.BlockSpec((VT_ROWS, seq), lambda b, h, qi: (h, b)),
            pl.BlockSpec((1,) + tdiag.shape[1:], lambda b, h, qi: (h, 0, 0, 0)),
            pl.BlockSpec(lam_p.shape, lambda b, h, qi: (0, 0)),
            pl.BlockSpec((1, LANES), lambda b, h, qi: (0, h)),
        ],
        out_specs=pl.BlockSpec((tsup, LANES), lambda b, h, qi: (b * nq + qi, h)),
        out_shape=jax.ShapeDtypeStruct((batch * seq, DIFF_HEADS * DIFF_V_DIM), BF16),
        scratch_shapes=[
            pltpu.VMEM((ns, 2 * LANES, 2 * tq), BF16),
            pltpu.VMEM((ns, 1, 2 * tq), F32),
            pltpu.VMEM((ns, VT_ROWS, 2 * tq), F32),
            pltpu.VMEM((ns, 2, tk, 2 * tq), F32),
            pltpu.VMEM((ns, 2, 1, 2 * tq), F32),
            pltpu.VMEM((ns, 2, tk, 2 * tq), BF16),
            pltpu.VMEM((ns, 2, 1, 2 * tq), F32),
        ],
        compiler_params=_cparams(("arbitrary", "arbitrary", "arbitrary")),
        name="diff_attn",
    )(slopes, first, main, qaug, kp, vt, tdiag, lam_p, subln_g)


def _retention_chunk(rows, q_ref, k_ref, v_ref, g_ref, dtab_ref, qdec_ref, kdec_ref, cdec_ref, gn_ref,
                     state_scr, r_scr):
    c = RET_C
    lane = lax.broadcasted_iota(jnp.int32, (c, LANES), 1)
    for p in range(RET_HEADS // 2):
        qp = q_ref[rows, p * LANES:(p + 1) * LANES]
        kp = k_ref[rows, p * LANES:(p + 1) * LANES]
        vp = v_ref[rows, p * 2 * RET_V_DIM:(p + 1) * 2 * RET_V_DIM]
        state = state_scr[p]
        state_b = state.astype(BF16)
        qdec = qdec_ref[:, p * LANES:(p + 1) * LANES]
        for e in range(2):
            head = 2 * p + e
            keep = (lane < RET_QK_DIM) if e == 0 else (lane >= RET_QK_DIM)
            qm = jnp.where(keep, qp, jnp.zeros_like(qp))
            s = lax.dot_general(qm, kp, (((1,), (1,)), ((), ())), preferred_element_type=F32)
            s = s * dtab_ref[head]
            ve = vp[:, e * RET_V_DIM:(e + 1) * RET_V_DIM]
            intra = jnp.dot(s.astype(BF16), ve, preferred_element_type=F32)
            qd = (qm.astype(F32) * qdec).astype(BF16)
            cross = jnp.dot(qd, state_b[:, e * RET_V_DIM:(e + 1) * RET_V_DIM],
                            preferred_element_type=F32)
            o = intra + cross
            mu = jnp.mean(o, axis=-1, keepdims=True)
            d = o - mu
            var = jnp.mean(d * d, axis=-1, keepdims=True)
            cols = slice(head * RET_V_DIM, (head + 1) * RET_V_DIM)
            gate = g_ref[rows, cols].astype(F32)
            y = d * lax.rsqrt(var + LN_EPS) * gn_ref[:, cols] * (gate * jax.nn.sigmoid(gate))
            r_scr[rows, cols] = y.astype(r_scr.dtype)
        kd = (kp.astype(F32) * kdec_ref[:, p * LANES:(p + 1) * LANES]).astype(BF16)
        kv = lax.dot_general(kd, vp, (((0,), (0,)), ((), ())), preferred_element_type=F32)
        state_scr[p] = state * cdec_ref[p] + kv


def _ret_tables(c):
    hh = np.arange(RET_HEADS, dtype=np.float64)
    lg = np.log(1.0 - 2.0 ** (-5.0 - hh))
    idx = np.arange(c, dtype=np.float64)
    nm = idx[:, None] - idx[None, :]
    dtab = np.where(nm >= 0, np.exp(lg[:, None, None] * np.maximum(nm, 0.0)), 0.0) * RET_QK_DIM ** -0.5
    qdec = np.exp(lg[None, :] * (idx + 1.0)[:, None])
    kdec = np.exp(lg[None, :] * (c - 1.0 - idx)[:, None]) * RET_QK_DIM ** -0.5
    qdec = np.repeat(qdec, RET_QK_DIM, axis=1)
    kdec = np.repeat(kdec, RET_QK_DIM, axis=1)
    cdec = np.repeat(np.exp(lg * c), RET_QK_DIM).reshape(RET_HEADS // 2, 2 * RET_QK_DIM, 1)
    return tuple(jnp.asarray(a, F32) for a in (dtab, qdec, kdec, cdec))


def _merge_kernel(x_ref, a_ref, rq_ref, rk_ref, rv_ref, rg_ref, mq_ref, mk_ref, mv_ref,
                  dtab_ref, qdec_ref, kdec_ref, cdec_ref, gn_ref,
                  wg_ref, bg_ref, wd_ref, wr_ref, wm_ref, wo_ref, g1_ref, b1_ref, o_ref,
                  state_scr, r_scr, m_scr, merged_scr, *, nc, tiles_per_seq):
    tm, d = x_ref.shape
    x = x_ref[...]
    xb = x.astype(BF16)

    @pl.when(pl.program_id(0) % tiles_per_seq == 0)
    def _():
        state_scr[...] = jnp.zeros(state_scr.shape, F32)

    for r0 in range(0, tm, RET_C):
        _retention_chunk(slice(r0, r0 + RET_C), rq_ref, rk_ref, rv_ref, rg_ref, dtab_ref, qdec_ref, kdec_ref,
                         cdec_ref, gn_ref, state_scr, r_scr)

    scale = jnp.asarray(MEM_HEAD_DIM ** -0.5, BF16)
    for h in range(MEM_HEADS):
        cols = slice(h * MEM_HEAD_DIM, (h + 1) * MEM_HEAD_DIM)
        qh = mq_ref[:, cols] * scale
        s = lax.dot_general(qh, mk_ref[0, :, cols], (((1,), (1,)), ((), ())),
                            preferred_element_type=F32)
        p = jnp.exp(s - jnp.max(s, axis=-1, keepdims=True))
        l = jnp.sum(p, axis=-1, keepdims=True)
        mh = jnp.dot(p.astype(BF16), mv_ref[0, :, cols], preferred_element_type=F32) / l
        m_scr[:, cols] = mh.astype(BF16)

    a = a_ref[...]
    r = r_scr[...]
    m = m_scr[...]
    for c in range(0, d, nc):
        cs = slice(c, c + nc)
        acc = None
        for br, (val, w_ref) in enumerate(((a, wd_ref), (r, wr_ref), (m, wm_ref))):
            gs = slice(br * d + c, br * d + c + nc)
            gate = jax.nn.sigmoid(jnp.dot(xb, wg_ref[:, gs], preferred_element_type=F32) + bg_ref[:, gs])
            term = gate * jnp.dot(val, w_ref[:, cs], preferred_element_type=F32)
            acc = term if acc is None else acc + term
        merged_scr[:, cs] = acc.astype(BF16)

    for r0 in range(0, tm, EPI_RB):
        rows = slice(r0, r0 + EPI_RB)
        mix = jnp.dot(merged_scr[rows, :], wo_ref[...], preferred_element_type=F32)
        o_ref[rows, :] = _layer_norm(DEEPNORM_ALPHA * x_ref[rows, :] + mix, g1_ref[...], b1_ref[...])


def _merge(x2, a, main, memkv, ret_g, wg, bg, wd, wr, wm, wo, g1, b1, batch, seq):
    tm = MERGE_TM
    t, d = x2.shape
    nt = seq // tm
    mlen = memkv.shape[1]
    mw = MEM_HEADS * MEM_HEAD_DIM
    qkw = RET_HEADS * RET_QK_DIM
    vw = RET_HEADS * RET_V_DIM
    dtab, qdec, kdec, cdec = _ret_tables(RET_C)
    row = lambda i: (i, 0)
    return pl.pallas_call(
        functools.partial(_merge_kernel, nc=MERGE_NC, tiles_per_seq=nt),
        grid=(t // tm,),
        in_specs=[
            pl.BlockSpec((tm, d), row),
            pl.BlockSpec((tm, a.shape[1]), row),
            pl.BlockSpec((tm, qkw), lambda i: (i, OFF_RQ // qkw)),
            pl.BlockSpec((tm, qkw), lambda i: (i, OFF_RK // qkw)),
            pl.BlockSpec((tm, vw), lambda i: (i, OFF_RV // vw)),
            pl.BlockSpec((tm, vw), lambda i: (i, OFF_RG // vw)),
            pl.BlockSpec((tm, mw), lambda i: (i, OFF_MQ // mw)),
            pl.BlockSpec((1, mlen, mw), lambda i: (i // nt, 0, 0)),
            pl.BlockSpec((1, mlen, mw), lambda i: (i // nt, 0, 1)),
            _resident(dtab.shape), _resident(qdec.shape), _resident(kdec.shape), _resident(cdec.shape),
            _resident(ret_g.shape),
            _resident(wg.shape), _resident(bg.shape), _resident(wd.shape), _resident(wr.shape),
            _resident(wm.shape), _resident(wo.shape), _resident(g1.shape), _resident(b1.shape),
        ],
        out_specs=pl.BlockSpec((tm, d), row),
        out_shape=jax.ShapeDtypeStruct((t, d), F32),
        scratch_shapes=[
            pltpu.VMEM((RET_HEADS // 2, 2 * RET_QK_DIM, 2 * RET_V_DIM), F32),
            pltpu.VMEM((tm, vw), BF16),
            pltpu.VMEM((tm, mw), BF16),
            pltpu.VMEM((tm, d), BF16),
        ],
        compiler_params=_cparams(("arbitrary",)),
        name="merge",
    )(x2, a, main, main, main, main, main, memkv, memkv, dtab, qdec, kdec, cdec, ret_g,
      wg, bg, wd, wr, wm, wo, g1, b1)


def _ffn_kernel(h_ref, wu_ref, cw_ref, cb_ref, wdn_ref, g2_ref, b2_ref, o_ref,
                carry_scr, hs_scr, act_scr, *, fc, rb, ffn_dim, tiles_per_seq):
    tm = h_ref.shape[0]
    halo = SUBLANES

    @pl.when(pl.program_id(0) % tiles_per_seq == 0)
    def _():
        carry_scr[...] = jnp.zeros(carry_scr.shape, F32)

    x = h_ref[...]
    xb = x.astype(BF16)

    def up_project(slot, col):
        cs = slice(col, col + fc)
        up = jnp.dot(xb, wu_ref[:, cs], preferred_element_type=F32)
        hs_scr[slot, 0:halo, :] = carry_scr[:, cs]
        hs_scr[slot, halo:halo + tm, :] = up
        carry_scr[:, cs] = up[tm - halo:tm, :]

    def conv_rows(slot, col, r0):
        cs = slice(col, col + fc)
        blk = hs_scr[slot, r0:r0 + rb + halo, :]
        w = cw_ref[:, cs]
        y = pltpu.roll(blk, 2, 0) * w[0:1] + pltpu.roll(blk, 1, 0) * w[1:2] + blk * w[2:3] + cb_ref[:, cs]
        return y[halo:, :]

    for j, c in enumerate(range(0, ffn_dim, fc)):
        sg, sv = 2 * (j % 2), 2 * (j % 2) + 1
        up_project(sg, c)
        up_project(sv, ffn_dim + c)
        for r0 in range(0, tm, rb):
            g = conv_rows(sg, c, r0)
            val = conv_rows(sv, ffn_dim + c, r0)
            act_scr[r0:r0 + rb, c:c + fc] = (g * jax.nn.sigmoid(g) * val).astype(BF16)

    for r0 in range(0, tm, EPI_RB):
        rows = slice(r0, r0 + EPI_RB)
        ffn = jnp.dot(act_scr[rows, :], wdn_ref[...], preferred_element_type=F32)
        o_ref[rows, :] = _layer_norm(DEEPNORM_ALPHA * h_ref[rows, :] + ffn, g2_ref[...], b2_ref[...])


def _ffn(h1, wu, cw, cb, wdn, g2, b2, seq):
    tm = FFN_TM
    t, d = h1.shape
    ffn_dim = wdn.shape[0]
    row = lambda i: (i, 0)
    return pl.pallas_call(
        functools.partial(_ffn_kernel, fc=FFN_FC, rb=FFN_RB, ffn_dim=ffn_dim, tiles_per_seq=seq // tm),
        grid=(t // tm,),
        in_specs=[
            pl.BlockSpec((tm, d), row),
            _resident(wu.shape), _resident(cw.shape), _resident(cb.shape), _resident(wdn.shape),
            _resident(g2.shape), _resident(b2.shape),
        ],
        out_specs=pl.BlockSpec((tm, d), row),
        out_shape=jax.ShapeDtypeStruct((t, d), F32),
        scratch_shapes=[
            pltpu.VMEM((SUBLANES, 2 * ffn_dim), F32),
            pltpu.VMEM((4, tm + SUBLANES, FFN_FC), F32),
            pltpu.VMEM((tm, ffn_dim), BF16),
        ],
        compiler_params=_cparams(("arbitrary",)),
        name="conv_ffn",
    )(h1, wu, cw, cb, wdn, g2, b2)


def kernel(x, mem, w_in, diff_lambda, diff_subln_g, ret_norm_g, w_mem_kv, w_diff_o, w_ret_o, w_mem_o,
           w_gate, b_gate, w_mix_out, ln1_g, ln1_b, w_up, conv_w, conv_b, w_down, ln2_g, ln2_b):
    batch, seq, d = x.shape
    mlen = mem.shape[1]
    assert w_in.shape[0] == DEPTH == 1 and conv_w.shape[1] == CONV_WIDTH == 3
    assert seq % PROJ_TM == 0 and MERGE_TM % RET_C == 0 and seq % FFN_TM == 0 and seq % MERGE_TM == 0
    assert PROJ_TM % ATT_TK == 0 and ATT_TQ % CHUNK == 0 and ATT_TK % ATT_TQ == 0 and seq % (ATT_TQ * ATT_NS) == 0
    assert ATT_UNROLL % 2 == 0 and (ATT_NS * ATT_TQ // ATT_TK) % ATT_UNROLL == 0
    l = 0
    row = lambda v: v[l].reshape(1, -1)

    x2 = x.reshape(batch * seq, d)
    main, kp, vt, nrm = _inproj(x2, w_in[l])
    memkv = _proj(mem.reshape(batch * mlen, d), w_mem_kv[l].astype(BF16), batch * mlen, PROJ_NC)
    memkv = memkv.reshape(batch, mlen, 2 * MEM_HEADS * MEM_HEAD_DIM)

    a = _diff_attention(main, kp, vt, nrm, diff_lambda[l], row(diff_subln_g), batch, seq)
    h1 = _merge(x2, a, main, memkv, row(ret_norm_g), w_gate[l].astype(BF16), row(b_gate),
                w_diff_o[l].astype(BF16), w_ret_o[l].astype(BF16), w_mem_o[l].astype(BF16),
                w_mix_out[l].astype(BF16), row(ln1_g), row(ln1_b), batch, seq)
    out = _ffn(h1, w_up[l].astype(BF16), conv_w[l], row(conv_b), w_down[l].astype(BF16),
               row(ln2_g), row(ln2_b), seq)
    return out.reshape(batch, seq, d)
```

```python
import functools
import math

import numpy as np
import jax
import jax.numpy as jnp
from jax import lax
from jax.experimental import pallas as pl
from jax.experimental.pallas import tpu as pltpu

F32 = jnp.float32
BF16 = jnp.bfloat16

CHUNK = 64
DIFF_HEADS = 4
DIFF_QK_DIM = 64
DIFF_V_DIM = 128
RET_HEADS = 4
RET_QK_DIM = 64
RET_V_DIM = 128
MEM_HEADS = 4
MEM_HEAD_DIM = 128
CONV_WIDTH = 3
LN_EPS = 1e-5
DEPTH = 1
DEEPNORM_ALPHA = (2.0 * DEPTH) ** 0.25
LAMBDA_INIT = 0.8 - 0.6 * math.exp(-0.3 * 0)
LOG2E = math.log2(math.e)

IN_DQ, IN_DK, IN_DV = 0, 512, 1024
IN_RQ, IN_END = 1536, 3584
OFF_DQ, OFF_RQ, OFF_RK, OFF_RV, OFF_RG, OFF_MQ, MAIN_W = 0, 512, 768, 1024, 1536, 2048, 2560

LANES = 128
SUBLANES = 8
BF16_ROWS = 16
VMEM_LIMIT = 56 * 1024 * 1024

PROJ_TM = 512
PROJ_NC = 512
ATT_TQ = 256
ATT_TK = 512
ATT_NS = 4
ATT_UNROLL = 2
ATT_RB = 256
ATT_UNDERFLOW_LOG2 = 134.0
ATT_BOUND_SCALE = 1.02
ATT_BOUND_SLACK = 0.05
VT_ROWS = DIFF_V_DIM + BF16_ROWS
RET_C = 256
MERGE_TM = 512
MERGE_NC = 512
FFN_TM = 512
FFN_FC = 256
FFN_RB = 128
EPI_RB = 256


def _cparams(sem):
    return pltpu.CompilerParams(dimension_semantics=sem, vmem_limit_bytes=VMEM_LIMIT)


def _resident(shape):
    nd = len(shape)
    return pl.BlockSpec(shape, lambda *_: (0,) * nd, pipeline_mode=pl.Buffered(1))


def _layer_norm(y, g, b):
    mu = jnp.mean(y, axis=-1, keepdims=True)
    d = y - mu
    var = jnp.mean(d * d, axis=-1, keepdims=True)
    return d * lax.rsqrt(var + LN_EPS) * g + b


def _alibi_slopes():
    return np.array([(2.0 ** (-8.0 / DIFF_HEADS)) ** (i + 1) for i in range(DIFF_HEADS)], np.float64)


def _bf16_split3(v):
    parts = []
    rem = np.float64(v)
    for _ in range(3):
        p = np.asarray(rem, np.float32).astype(jnp.bfloat16)
        parts.append(p)
        rem = rem - np.float64(p.astype(np.float32))
    return parts


def _proj_kernel(x_ref, w_ref, o_ref, *, nc):
    xb = x_ref[...].astype(BF16)
    n = w_ref.shape[1]
    for c in range(0, n, nc):
        o_ref[:, c:c + nc] = jnp.dot(xb, w_ref[:, c:c + nc],
                                     preferred_element_type=F32).astype(o_ref.dtype)


def _proj(x2, w, tm, nc):
    t, d = x2.shape
    n = w.shape[1]
    return pl.pallas_call(
        functools.partial(_proj_kernel, nc=nc),
        grid=(t // tm,),
        in_specs=[pl.BlockSpec((tm, d), lambda i: (i, 0)), _resident((d, n))],
        out_specs=pl.BlockSpec((tm, n), lambda i: (i, 0)),
        out_shape=jax.ShapeDtypeStruct((t, n), BF16),
        compiler_params=_cparams(("arbitrary",)),
        name="memkv",
    )(x2, w)


def _inproj_kernel(x_ref, wm_ref, wk_ref, wvt_ref, kaug_ref, sel_ref, main_ref, kp_ref, vt_ref, nrm_ref, *, nc):
    xb = x_ref[...].astype(BF16)
    qscale = LOG2E * DIFF_QK_DIM ** -0.5
    for c in range(0, MAIN_W, nc):
        acc = jnp.dot(xb, wm_ref[:, c:c + nc], preferred_element_type=F32)
        if c < OFF_RQ:
            acc = acc * qscale
        main_ref[:, c:c + nc] = acc.astype(main_ref.dtype)

    k = jnp.dot(xb, wk_ref[...], preferred_element_type=F32).astype(kp_ref.dtype)
    for h in range(DIFF_HEADS):
        kp_ref[:, h * 2 * LANES:h * 2 * LANES + LANES] = k[:, h * LANES:(h + 1) * LANES]
        kp_ref[:, h * 2 * LANES + LANES:(h + 1) * 2 * LANES] = kaug_ref[:, h * LANES:(h + 1) * LANES]

    def max_sq_norm(v):
        vf = v.astype(F32)
        sq = jnp.dot((vf * vf).astype(BF16), sel_ref[...], preferred_element_type=F32)
        return jnp.broadcast_to(jnp.max(sq, axis=0, keepdims=True), (SUBLANES // 2, LANES))

    nrm_ref[0] = jnp.concatenate([max_sq_norm(main_ref[:, OFF_DQ:OFF_RQ]), max_sq_norm(k)], axis=0)

    vt = lax.dot_general(wvt_ref[...], xb, (((1,), (1,)), ((), ())), preferred_element_type=F32)
    row = lax.broadcasted_iota(jnp.int32, vt.shape, 0)
    ones_row = row == DIFF_V_DIM
    for h in range(1, DIFF_HEADS):
        ones_row = ones_row | (row == h * VT_ROWS + DIFF_V_DIM)
    vt_ref[...] = jnp.where(ones_row, 1.0, vt).astype(vt_ref.dtype)


def _inproj(x2, w_in):
    t, d = x2.shape
    tm = PROJ_TM
    cols = lambda a, b: w_in[:, a:b]
    w_main = jnp.concatenate([cols(IN_DQ, IN_DK), cols(IN_RQ, IN_END)], axis=1).astype(BF16)
    w_k = cols(IN_DK, IN_DV).astype(BF16)
    wv = cols(IN_DV, IN_RQ).astype(BF16).T.reshape(DIFF_HEADS, DIFF_V_DIM, d)
    w_vt = jnp.pad(wv, ((0, 0), (0, VT_ROWS - DIFF_V_DIM), (0, 0))).reshape(DIFF_HEADS * VT_ROWS, d)

    kaug = _alibi_key_columns(tm, ATT_TK)
    sel = (np.arange(DIFF_HEADS * 2 * DIFF_QK_DIM)[:, None] // DIFF_QK_DIM == np.arange(LANES)[None, :])
    sel = jnp.asarray(sel, BF16)

    row = lambda i: (i, 0)
    return pl.pallas_call(
        functools.partial(_inproj_kernel, nc=PROJ_NC),
        grid=(t // tm,),
        in_specs=[pl.BlockSpec((tm, d), row), _resident(w_main.shape), _resident(w_k.shape),
                  _resident(w_vt.shape), _resident(kaug.shape), _resident(sel.shape)],
        out_specs=[pl.BlockSpec((tm, MAIN_W), row),
                   pl.BlockSpec((tm, DIFF_HEADS * 2 * LANES), row),
                   pl.BlockSpec((DIFF_HEADS * VT_ROWS, tm), lambda i: (0, i)),
                   pl.BlockSpec((1, SUBLANES, LANES), lambda i: (i, 0, 0))],
        out_shape=[jax.ShapeDtypeStruct((t, MAIN_W), BF16),
                   jax.ShapeDtypeStruct((t, DIFF_HEADS * 2 * LANES), BF16),
                   jax.ShapeDtypeStruct((DIFF_HEADS * VT_ROWS, t), BF16),
                   jax.ShapeDtypeStruct((t // tm, SUBLANES, LANES), F32)],
        compiler_params=_cparams(("arbitrary",)),
        name="inproj",
    )(x2, w_main, w_k, w_vt, kaug, sel)


def _attn_kernel(slope_ref, first_ref, q_ref, qaug_ref, kp_ref, vt_ref, tdiag_ref, lam_ref, g_ref, o_ref,
                 qs_scr, m_scr, acc_scr, z_scr, zmax_scr, p_scr, alpha_scr, *, tq, tk, ns):
    b, h, qi = pl.program_id(0), pl.program_id(1), pl.program_id(2)
    per_tile = tk // tq
    n0 = (ns // per_tile) * qi
    first = first_ref[b, h, qi]
    slope = slope_ref[h]

    @pl.when((b == 0) & (h == 0) & (qi == 0))
    def _():
        p_scr[...] = jnp.zeros(p_scr.shape, BF16)

    qa = qaug_ref[0]
    for j in range(ns):
        q = q_ref[j * tq:(j + 1) * tq, :]
        lane = lax.broadcasted_iota(jnp.int32, q.shape, 1)
        zero = jnp.zeros_like(q)
        q1 = jnp.concatenate([jnp.where(lane < DIFF_QK_DIM, q, zero), qa], axis=1).astype(F32)
        q2 = jnp.concatenate([jnp.where(lane >= DIFF_QK_DIM, q, zero), qa], axis=1).astype(F32)
        qs_scr[j, :, 0:tq] = q1.T.astype(BF16)
        qs_scr[j, :, tq:2 * tq] = q2.T.astype(BF16)
        m_scr[j] = jnp.full(m_scr.shape[1:], -jnp.inf, F32)
        acc_scr[j] = jnp.zeros(acc_scr.shape[1:], F32)
        alpha_scr[j, 1] = jnp.ones(alpha_scr.shape[2:], F32)

    def score_stage(j, kb, slot, rows=tk):
        start = pl.multiple_of(kb * tk, tk)
        z = jnp.dot(kp_ref[pl.ds(start, rows), :], qs_scr[j],
                    preferred_element_type=F32)
        zmax = None
        for r in range(0, rows, ATT_RB):
            zb = z[r:r + ATT_RB]
            z_scr[j, slot, r:r + ATT_RB] = zb
            bmax = jnp.max(zb, axis=0, keepdims=True)
            zmax = bmax if zmax is None else jnp.maximum(zmax, bmax)
        zmax_scr[j, slot] = zmax

    def softmax_stage(j, z, zmax, c, slot, rows=tk):
        m_prev = m_scr[j]
        m_new = jnp.maximum(m_prev, zmax + c)
        shift = m_new - c
        for r in range(0, rows, ATT_RB):
            p_scr[j, slot, r:r + ATT_RB] = jnp.exp2(z[r:r + ATT_RB] - shift).astype(BF16)
        alpha_scr[j, slot] = jnp.exp2(m_prev - m_new)
        m_scr[j] = m_new

    def value_stage(j, kb, slot, rows=tk):
        start = pl.multiple_of(jnp.maximum(kb, first) * tk, tk)
        v = vt_ref[:, pl.ds(start, rows)]
        v = jnp.where(kb >= first, v, jnp.zeros_like(v))
        acc_scr[j] = alpha_scr[j, slot] * acc_scr[j] + jnp.dot(v, p_scr[j, slot, 0:rows],
                                                               preferred_element_type=F32)

    def tile_bias(j, kb):
        return -slope * ((n0 - kb) * tk + j * tq).astype(F32)

    def full_trip(kb, cur, nxt):
        for j in range(ns):
            score_stage(j, kb + 1, nxt)
            softmax_stage(j, z_scr.at[j, cur], zmax_scr[j, cur], tile_bias(j, kb), cur)
            value_stage(j, kb - 1, nxt)

    for j in range(ns):
        score_stage(j, first, 0)

    def unrolled(i, carry):
        for u in range(ATT_UNROLL):
            full_trip(ATT_UNROLL * i + u, u % 2, 1 - u % 2)
        return carry

    lax.fori_loop(first // ATT_UNROLL, n0 // ATT_UNROLL, unrolled, 0)

    for i in range(ns // per_tile + 1):
        kb = n0 + i
        cur, nxt = i % 2, 1 - i % 2
        for j in range(ns):
            last = j // per_tile
            seen = (j % per_tile + 1) * tq
            if i < last:
                score_stage(j, kb + 1, nxt, rows=seen if i + 1 == last else tk)
                softmax_stage(j, z_scr.at[j, cur], zmax_scr[j, cur], tile_bias(j, kb), cur)
            elif i == last:
                td = tdiag_ref[0, j % per_tile, 0:seen, :]
                z = z_scr[j, cur, 0:seen]
                z = jnp.concatenate([z[:, 0:tq] + td, z[:, tq:2 * tq] + td], axis=1)
                softmax_stage(j, z, jnp.max(z, axis=0, keepdims=True), tile_bias(j, kb), cur, rows=seen)
            if i <= last + 1:
                value_stage(j, kb - 1, nxt, rows=seen if i == last + 1 else tk)

    lp = lam_ref[...]
    lam = (jnp.exp(jnp.sum(lp[0:1] * lp[1:2], axis=-1, keepdims=True))
           - jnp.exp(jnp.sum(lp[2:3] * lp[3:4], axis=-1, keepdims=True)) + LAMBDA_INIT)
    for j in range(ns):
        acc = acc_scr[j]
        o = acc[0:DIFF_V_DIM, :] / acc[DIFF_V_DIM:DIFF_V_DIM + 1, :]
        y = o[:, 0:tq] - lam * o[:, tq:2 * tq]
        y = y * lax.rsqrt(jnp.mean(y * y, axis=0, keepdims=True) + LN_EPS)
        o_ref[j * tq:(j + 1) * tq, :] = (y.T * g_ref[...] * (1.0 - LAMBDA_INIT)).astype(o_ref.dtype)


def _alibi_key_columns(n_rows, tk):
    l3 = _bf16_split3(LOG2E)
    off = np.arange(n_rows) % tk
    kaug = np.zeros((n_rows, DIFF_HEADS, LANES), np.float32)
    for hh, m in enumerate(_alibi_slopes()):
        for i in range(3):
            kaug[:, hh, i] = np.float32(l3[i])
            kaug[:, hh, 3 + i] = m * (off % ATT_TQ)
            kaug[:, hh, 6 + i] = m * (off - off % ATT_TQ)
    return jnp.asarray(kaug.reshape(n_rows, DIFF_HEADS * LANES), BF16)


def _attn_tables(tq, tk):
    slopes = _alibi_slopes()
    l3 = _bf16_split3(LOG2E)
    il = np.arange(tq, dtype=np.float64)
    qaug = np.zeros((DIFF_HEADS, tq, LANES), np.float32)
    for hh, m in enumerate(slopes):
        for i in range(3):
            qaug[hh, :, i] = -m * il
            qaug[hh, :, 3 + i] = np.float32(l3[i])
            qaug[hh, :, 6 + i] = np.float32(l3[i])
    j = np.arange(tk)[None, :, None]
    i = np.arange(tq)[None, None, :] + tq * np.arange(tk // tq)[:, None, None]
    allowed = (j // CHUNK) <= (i // CHUNK)
    corr = LOG2E * slopes[:, None, None, None] * ((i - j) - np.abs(i - j))[None]
    tdiag = np.where(allowed[None], corr, -np.inf)
    return (jnp.asarray(slopes * LOG2E, F32), jnp.asarray(qaug, BF16), jnp.asarray(tdiag, F32))


def _first_key_tiles(nrm, batch, seq):
    tiles_per_seq = seq // PROJ_TM
    tsup = ATT_TQ * ATT_NS
    n = nrm.reshape(batch, tiles_per_seq, 2, SUBLANES // 2, LANES)[:, :, :, 0, :DIFF_HEADS * 2]
    n = n.reshape(batch, tiles_per_seq, 2, DIFF_HEADS, 2).max(axis=-1)
    q2 = n[:, :, 0].reshape(batch, seq // tsup, tsup // PROJ_TM, DIFF_HEADS).max(axis=2)
    k2 = n[:, :, 1].max(axis=1)
    bound = jnp.sqrt(q2 * k2[:, None, :]) * ATT_BOUND_SCALE + ATT_BOUND_SLACK
    slope = jnp.asarray(_alibi_slopes() * LOG2E, F32)
    reach = (2.0 * bound + ATT_UNDERFLOW_LOG2) / slope
    needed = jnp.floor(jnp.maximum(reach - 1.0, 0.0) / ATT_TK).astype(jnp.int32) + 1
    n0 = (tsup // ATT_TK) * jnp.arange(seq // tsup, dtype=jnp.int32)[None, :, None]
    first = jnp.maximum(n0 - needed, 0) // ATT_UNROLL * ATT_UNROLL
    return first.transpose(0, 2, 1)


def _diff_attention(main, kp, vt, nrm, lam_p, subln_g, batch, seq):
    tq, tk, ns = ATT_TQ, ATT_TK, ATT_NS
    tsup = tq * ns
    nq = seq // tsup
    slopes, qaug, tdiag = _attn_tables(tq, tk)
    first = _first_key_tiles(nrm, batch, seq)
    return pl.pallas_call(
        functools.partial(_attn_kernel, tq=tq, tk=tk, ns=ns),
        grid=(batch, DIFF_HEADS, nq),
        in_specs=[
            pl.BlockSpec(memory_space=pltpu.SMEM),
            pl.BlockSpec(memory_space=pltpu.SMEM),
            pl.BlockSpec((tsup, LANES), lambda b, h, qi: (b * nq + qi, OFF_DQ // LANES + h)),
            pl.BlockSpec((1, tq, LANES), lambda b, h, qi: (h, 0, 0)),
            pl.BlockSpec((seq, 2 * LANES), lambda b, h, qi: (b, h)),
            pl.BlockSpec((VT_ROWS, seq), lambda b, h, qi: (h, b)),
            pl.BlockSpec((1,) + tdiag.shape[1:], lambda b, h, qi: (h, 0, 0, 0)),
            pl.BlockSpec(lam_p.shape, lambda b, h, qi: (0, 0)),
            pl.BlockSpec((1, LANES), lambda b, h, qi: (0, h)),
        ],
        out_specs=pl.BlockSpec((tsup, LANES), lambda b, h, qi: (b * nq + qi, h)),
        out_shape=jax.ShapeDtypeStruct((batch * seq, DIFF_HEADS * DIFF_V_DIM), BF16),
        scratch_shapes=[
            pltpu.VMEM((ns, 2 * LANES, 2 * tq), BF16),
            pltpu.VMEM((ns, 1, 2 * tq), F32),
            pltpu.VMEM((ns, VT_ROWS, 2 * tq), F32),
            pltpu.VMEM((ns, 2, tk, 2 * tq), F32),
            pltpu.VMEM((ns, 2, 1, 2 * tq), F32),
            pltpu.VMEM((ns, 2, tk, 2 * tq), BF16),
            pltpu.VMEM((ns, 2, 1, 2 * tq), F32),
        ],
        compiler_params=_cparams(("arbitrary", "arbitrary", "arbitrary")),
        name="diff_attn",
    )(slopes, first, main, qaug, kp, vt, tdiag, lam_p, subln_g)


def _retention_chunk(rows, q_ref, k_ref, v_ref, g_ref, dtab_ref, qdec_ref, kdec_ref, cdec_ref, gn_ref,
                     state_scr, r_scr):
    c = RET_C
    lane = lax.broadcasted_iota(jnp.int32, (c, LANES), 1)
    for p in range(RET_HEADS // 2):
        qp = q_ref[rows, p * LANES:(p + 1) * LANES]
        kp = k_ref[rows, p * LANES:(p + 1) * LANES]
        vp = v_ref[rows, p * 2 * RET_V_DIM:(p + 1) * 2 * RET_V_DIM]
        state = state_scr[p]
        state_b = state.astype(BF16)
        qdec = qdec_ref[:, p * LANES:(p + 1) * LANES]
        for e in range(2):
            head = 2 * p + e
            keep = (lane < RET_QK_DIM) if e == 0 else (lane >= RET_QK_DIM)
            qm = jnp.where(keep, qp, jnp.zeros_like(qp))
            s = lax.dot_general(qm, kp, (((1,), (1,)), ((), ())), preferred_element_type=F32)
            s = s * dtab_ref[head]
            ve = vp[:, e * RET_V_DIM:(e + 1) * RET_V_DIM]
            intra = jnp.dot(s.astype(BF16), ve, preferred_element_type=F32)
            qd = (qm.astype(F32) * qdec).astype(BF16)
            cross = jnp.dot(qd, state_b[:, e * RET_V_DIM:(e + 1) * RET_V_DIM],
                            preferred_element_type=F32)
            o = intra + cross
            mu = jnp.mean(o, axis=-1, keepdims=True)
            d = o - mu
            var = jnp.mean(d * d, axis=-1, keepdims=True)
            cols = slice(head * RET_V_DIM, (head + 1) * RET_V_DIM)
            gate = g_ref[rows, cols].astype(F32)
            y = d * lax.rsqrt(var + LN_EPS) * gn_ref[:, cols] * (gate * jax.nn.sigmoid(gate))
            r_scr[rows, cols] = y.astype(r_scr.dtype)
        kd = (kp.astype(F32) * kdec_ref[:, p * LANES:(p + 1) * LANES]).astype(BF16)
        kv = lax.dot_general(kd, vp, (((0,), (0,)), ((), ())), preferred_element_type=F32)
        state_scr[p] = state * cdec_ref[p] + kv


def _ret_tables(c):
    hh = np.arange(RET_HEADS, dtype=np.float64)
    lg = np.log(1.0 - 2.0 ** (-5.0 - hh))
    idx = np.arange(c, dtype=np.float64)
    nm = idx[:, None] - idx[None, :]
    dtab = np.where(nm >= 0, np.exp(lg[:, None, None] * np.maximum(nm, 0.0)), 0.0) * RET_QK_DIM ** -0.5
    qdec = np.exp(lg[None, :] * (idx + 1.0)[:, None])
    kdec = np.exp(lg[None, :] * (c - 1.0 - idx)[:, None]) * RET_QK_DIM ** -0.5
    qdec = np.repeat(qdec, RET_QK_DIM, axis=1)
    kdec = np.repeat(kdec, RET_QK_DIM, axis=1)
    cdec = np.repeat(np.exp(lg * c), RET_QK_DIM).reshape(RET_HEADS // 2, 2 * RET_QK_DIM, 1)
    return tuple(jnp.asarray(a, F32) for a in (dtab, qdec, kdec, cdec))


def _merge_kernel(x_ref, a_ref, rq_ref, rk_ref, rv_ref, rg_ref, mq_ref, mk_ref, mv_ref,
                  dtab_ref, qdec_ref, kdec_ref, cdec_ref, gn_ref,
                  wg_ref, bg_ref, wd_ref, wr_ref, wm_ref, wo_ref, g1_ref, b1_ref, o_ref,
                  state_scr, r_scr, m_scr, merged_scr, *, nc, tiles_per_seq):
    tm, d = x_ref.shape
    x = x_ref[...]
    xb = x.astype(BF16)

    @pl.when(pl.program_id(0) % tiles_per_seq == 0)
    def _():
        state_scr[...] = jnp.zeros(state_scr.shape, F32)

    for r0 in range(0, tm, RET_C):
        _retention_chunk(slice(r0, r0 + RET_C), rq_ref, rk_ref, rv_ref, rg_ref, dtab_ref, qdec_ref, kdec_ref,
                         cdec_ref, gn_ref, state_scr, r_scr)

    scale = jnp.asarray(MEM_HEAD_DIM ** -0.5, BF16)
    for h in range(MEM_HEADS):
        cols = slice(h * MEM_HEAD_DIM, (h + 1) * MEM_HEAD_DIM)
        qh = mq_ref[:, cols] * scale
        s = lax.dot_general(qh, mk_ref[0, :, cols], (((1,), (1,)), ((), ())),
                            preferred_element_type=F32)
        p = jnp.exp(s - jnp.max(s, axis=-1, keepdims=True))
        l = jnp.sum(p, axis=-1, keepdims=True)
        mh = jnp.dot(p.astype(BF16), mv_ref[0, :, cols], preferred_element_type=F32) / l
        m_scr[:, cols] = mh.astype(BF16)

    a = a_ref[...]
    r = r_scr[...]
    m = m_scr[...]
    for c in range(0, d, nc):
        cs = slice(c, c + nc)
        acc = None
        for br, (val, w_ref) in enumerate(((a, wd_ref), (r, wr_ref), (m, wm_ref))):
            gs = slice(br * d + c, br * d + c + nc)
            gate = jax.nn.sigmoid(jnp.dot(xb, wg_ref[:, gs], preferred_element_type=F32) + bg_ref[:, gs])
            term = gate * jnp.dot(val, w_ref[:, cs], preferred_element_type=F32)
            acc = term if acc is None else acc + term
        merged_scr[:, cs] = acc.astype(BF16)

    for r0 in range(0, tm, EPI_RB):
        rows = slice(r0, r0 + EPI_RB)
        mix = jnp.dot(merged_scr[rows, :], wo_ref[...], preferred_element_type=F32)
        o_ref[rows, :] = _layer_norm(DEEPNORM_ALPHA * x_ref[rows, :] + mix, g1_ref[...], b1_ref[...])


def _merge(x2, a, main, memkv, ret_g, wg, bg, wd, wr, wm, wo, g1, b1, batch, seq):
    tm = MERGE_TM
    t, d = x2.shape
    nt = seq // tm
    mlen = memkv.shape[1]
    mw = MEM_HEADS * MEM_HEAD_DIM
    qkw = RET_HEADS * RET_QK_DIM
    vw = RET_HEADS * RET_V_DIM
    dtab, qdec, kdec, cdec = _ret_tables(RET_C)
    row = lambda i: (i, 0)
    return pl.pallas_call(
        functools.partial(_merge_kernel, nc=MERGE_NC, tiles_per_seq=nt),
        grid=(t // tm,),
        in_specs=[
            pl.BlockSpec((tm, d), row),
            pl.BlockSpec((tm, a.shape[1]), row),
            pl.BlockSpec((tm, qkw), lambda i: (i, OFF_RQ // qkw)),
            pl.BlockSpec((tm, qkw), lambda i: (i, OFF_RK // qkw)),
            pl.BlockSpec((tm, vw), lambda i: (i, OFF_RV // vw)),
            pl.BlockSpec((tm, vw), lambda i: (i, OFF_RG // vw)),
            pl.BlockSpec((tm, mw), lambda i: (i, OFF_MQ // mw)),
            pl.BlockSpec((1, mlen, mw), lambda i: (i // nt, 0, 0)),
            pl.BlockSpec((1, mlen, mw), lambda i: (i // nt, 0, 1)),
            _resident(dtab.shape), _resident(qdec.shape), _resident(kdec.shape), _resident(cdec.shape),
            _resident(ret_g.shape),
            _resident(wg.shape), _resident(bg.shape), _resident(wd.shape), _resident(wr.shape),
            _resident(wm.shape), _resident(wo.shape), _resident(g1.shape), _resident(b1.shape),
        ],
        out_specs=pl.BlockSpec((tm, d), row),
        out_shape=jax.ShapeDtypeStruct((t, d), F32),
        scratch_shapes=[
            pltpu.VMEM((RET_HEADS // 2, 2 * RET_QK_DIM, 2 * RET_V_DIM), F32),
            pltpu.VMEM((tm, vw), BF16),
            pltpu.VMEM((tm, mw), BF16),
            pltpu.VMEM((tm, d), BF16),
        ],
        compiler_params=_cparams(("arbitrary",)),
        name="merge",
    )(x2, a, main, main, main, main, main, memkv, memkv, dtab, qdec, kdec, cdec, ret_g,
      wg, bg, wd, wr, wm, wo, g1, b1)


def _ffn_kernel(h_ref, wu_ref, cw_ref, cb_ref, wdn_ref, g2_ref, b2_ref, o_ref,
                carry_scr, hs_scr, act_scr, *, fc, rb, ffn_dim, tiles_per_seq):
    tm = h_ref.shape[0]
    halo = SUBLANES

    @pl.when(pl.program_id(0) % tiles_per_seq == 0)
    def _():
        carry_scr[...] = jnp.zeros(carry_scr.shape, F32)

    x = h_ref[...]
    xb = x.astype(BF16)

    def up_project(slot, col):
        cs = slice(col, col + fc)
        up = jnp.dot(xb, wu_ref[:, cs], preferred_element_type=F32)
        hs_scr[slot, 0:halo, :] = carry_scr[:, cs]
        hs_scr[slot, halo:halo + tm, :] = up
        carry_scr[:, cs] = up[tm - halo:tm, :]

    def conv_rows(slot, col, r0):
        cs = slice(col, col + fc)
        blk = hs_scr[slot, r0:r0 + rb + halo, :]
        w = cw_ref[:, cs]
        y = pltpu.roll(blk, 2, 0) * w[0:1] + pltpu.roll(blk, 1, 0) * w[1:2] + blk * w[2:3] + cb_ref[:, cs]
        return y[halo:, :]

    for j, c in enumerate(range(0, ffn_dim, fc)):
        sg, sv = 2 * (j % 2), 2 * (j % 2) + 1
        up_project(sg, c)
        up_project(sv, ffn_dim + c)
        for r0 in range(0, tm, rb):
            g = conv_rows(sg, c, r0)
            val = conv_rows(sv, ffn_dim + c, r0)
            act_scr[r0:r0 + rb, c:c + fc] = (g * jax.nn.sigmoid(g) * val).astype(BF16)

    for r0 in range(0, tm, EPI_RB):
        rows = slice(r0, r0 + EPI_RB)
        ffn = jnp.dot(act_scr[rows, :], wdn_ref[...], preferred_element_type=F32)
        o_ref[rows, :] = _layer_norm(DEEPNORM_ALPHA * h_ref[rows, :] + ffn, g2_ref[...], b2_ref[...])


def _ffn(h1, wu, cw, cb, wdn, g2, b2, seq):
    tm = FFN_TM
    t, d = h1.shape
    ffn_dim = wdn.shape[0]
    row = lambda i: (i, 0)
    return pl.pallas_call(
        functools.partial(_ffn_kernel, fc=FFN_FC, rb=FFN_RB, ffn_dim=ffn_dim, tiles_per_seq=seq // tm),
        grid=(t // tm,),
        in_specs=[
            pl.BlockSpec((tm, d), row),
            _resident(wu.shape), _resident(cw.shape), _resident(cb.shape), _resident(wdn.shape),
            _resident(g2.shape), _resident(b2.shape),
        ],
        out_specs=pl.BlockSpec((tm, d), row),
        out_shape=jax.ShapeDtypeStruct((t, d), F32),
        scratch_shapes=[
            pltpu.VMEM((SUBLANES, 2 * ffn_dim), F32),
            pltpu.VMEM((4, tm + SUBLANES, FFN_FC), F32),
            pltpu.VMEM((tm, ffn_dim), BF16),
        ],
        compiler_params=_cparams(("arbitrary",)),
        name="conv_ffn",
    )(h1, wu, cw, cb, wdn, g2, b2)


def kernel(x, mem, w_in, diff_lambda, diff_subln_g, ret_norm_g, w_mem_kv, w_diff_o, w_ret_o, w_mem_o,
           w_gate, b_gate, w_mix_out, ln1_g, ln1_b, w_up, conv_w, conv_b, w_down, ln2_g, ln2_b):
    batch, seq, d = x.shape
    mlen = mem.shape[1]
    assert w_in.shape[0] == DEPTH == 1 and conv_w.shape[1] == CONV_WIDTH == 3
    assert seq % PROJ_TM == 0 and MERGE_TM % RET_C == 0 and seq % FFN_TM == 0 and seq % MERGE_TM == 0
    assert PROJ_TM % ATT_TK == 0 and ATT_TQ % CHUNK == 0 and ATT_TK % ATT_TQ == 0 and seq % (ATT_TQ * ATT_NS) == 0
    assert ATT_UNROLL % 2 == 0 and (ATT_NS * ATT_TQ // ATT_TK) % ATT_UNROLL == 0
    l = 0
    row = lambda v: v[l].reshape(1, -1)

    x2 = x.reshape(batch * seq, d)
    main, kp, vt, nrm = _inproj(x2, w_in[l])
    memkv = _proj(mem.reshape(batch * mlen, d), w_mem_kv[l].astype(BF16), batch * mlen, PROJ_NC)
    memkv = memkv.reshape(batch, mlen, 2 * MEM_HEADS * MEM_HEAD_DIM)

    a = _diff_attention(main, kp, vt, nrm, diff_lambda[l], row(diff_subln_g), batch, seq)
    h1 = _merge(x2, a, main, memkv, row(ret_norm_g), w_gate[l].astype(BF16), row(b_gate),
                w_diff_o[l].astype(BF16), w_ret_o[l].astype(BF16), w_mem_o[l].astype(BF16),
                w_mix_out[l].astype(BF16), row(ln1_g), row(ln1_b), batch, seq)
    out = _ffn(h1, w_up[l].astype(BF16), conv_w[l], row(conv_b), w_down[l].astype(BF16),
               row(ln2_g), row(ln2_b), seq)
    return out.reshape(batch, seq, d)
```

```python
import functools
import math

import numpy as np
import jax
import jax.numpy as jnp
from jax import lax
from jax.experimental import pallas as pl
from jax.experimental.pallas import tpu as pltpu

F32 = jnp.float32
BF16 = jnp.bfloat16

CHUNK = 64
DIFF_HEADS = 4
DIFF_QK_DIM = 64
DIFF_V_DIM = 128
RET_HEADS = 4
RET_QK_DIM = 64
RET_V_DIM = 128
MEM_HEADS = 4
MEM_HEAD_DIM = 128
CONV_WIDTH = 3
LN_EPS = 1e-5
DEPTH = 1
DEEPNORM_ALPHA = (2.0 * DEPTH) ** 0.25
LAMBDA_INIT = 0.8 - 0.6 * math.exp(-0.3 * 0)
LOG2E = math.log2(math.e)

IN_DQ, IN_DK, IN_DV = 0, 512, 1024
IN_RQ, IN_END = 1536, 3584
OFF_DQ, OFF_RQ, OFF_RK, OFF_RV, OFF_RG, OFF_MQ, MAIN_W = 0, 512, 768, 1024, 1536, 2048, 2560

LANES = 128
SUBLANES = 8
BF16_ROWS = 16
VMEM_LIMIT = 56 * 1024 * 1024

PROJ_TM = 512
PROJ_NC = 512
ATT_TQ = 256
ATT_TK = 512
ATT_NS = 4
ATT_UNROLL = 2
ATT_RB = 256
ATT_UNDERFLOW_LOG2 = 134.0
ATT_BOUND_SCALE = 1.02
ATT_BOUND_SLACK = 0.05
VT_ROWS = DIFF_V_DIM + BF16_ROWS
RET_C = 256
MERGE_TM = 512
MERGE_NC = 512
FFN_TM = 512
FFN_FC = 256
FFN_RB = 128
EPI_RB = 256


def _cparams(sem):
    return pltpu.CompilerParams(dimension_semantics=sem, vmem_limit_bytes=VMEM_LIMIT)


def _resident(shape):
    nd = len(shape)
    return pl.BlockSpec(shape, lambda *_: (0,) * nd, pipeline_mode=pl.Buffered(1))


def _layer_norm(y, g, b):
    mu = jnp.mean(y, axis=-1, keepdims=True)
    d = y - mu
    var = jnp.mean(d * d, axis=-1, keepdims=True)
    return d * lax.rsqrt(var + LN_EPS) * g + b


def _alibi_slopes():
    return np.array([(2.0 ** (-8.0 / DIFF_HEADS)) ** (i + 1) for i in range(DIFF_HEADS)], np.float64)


def _bf16_split3(v):
    parts = []
    rem = np.float64(v)
    for _ in range(3):
        p = np.asarray(rem, np.float32).astype(jnp.bfloat16)
        parts.append(p)
        rem = rem - np.float64(p.astype(np.float32))
    return parts


def _proj_kernel(x_ref, w_ref, o_ref, *, nc):
    xb = x_ref[...].astype(BF16)
    n = w_ref.shape[1]
    for c in range(0, n, nc):
        o_ref[:, c:c + nc] = jnp.dot(xb, w_ref[:, c:c + nc],
                                     preferred_element_type=F32).astype(o_ref.dtype)


def _proj(x2, w, tm, nc):
    t, d = x2.shape
    n = w.shape[1]
    return pl.pallas_call(
        functools.partial(_proj_kernel, nc=nc),
        grid=(t // tm,),
        in_specs=[pl.BlockSpec((tm, d), lambda i: (i, 0)), _resident((d, n))],
        out_specs=pl.BlockSpec((tm, n), lambda i: (i, 0)),
        out_shape=jax.ShapeDtypeStruct((t, n), BF16),
        compiler_params=_cparams(("arbitrary",)),
        name="memkv",
    )(x2, w)


def _inproj_kernel(x_ref, wm_ref, wk_ref, wvt_ref, kaug_ref, sel_ref, main_ref, kp_ref, vt_ref, nrm_ref, *, nc):
    xb = x_ref[...].astype(BF16)
    qscale = LOG2E * DIFF_QK_DIM ** -0.5
    for c in range(0, MAIN_W, nc):
        acc = jnp.dot(xb, wm_ref[:, c:c + nc], preferred_element_type=F32)
        if c < OFF_RQ:
            acc = acc * qscale
        main_ref[:, c:c + nc] = acc.astype(main_ref.dtype)

    k = jnp.dot(xb, wk_ref[...], preferred_element_type=F32).astype(kp_ref.dtype)
    for h in range(DIFF_HEADS):
        kp_ref[:, h * 2 * LANES:h * 2 * LANES + LANES] = k[:, h * LANES:(h + 1) * LANES]
        kp_ref[:, h * 2 * LANES + LANES:(h + 1) * 2 * LANES] = kaug_ref[:, h * LANES:(h + 1) * LANES]

    def max_sq_norm(v):
        vf = v.astype(F32)
        sq = jnp.dot((vf * vf).astype(BF16), sel_ref[...], preferred_element_type=F32)
        return jnp.broadcast_to(jnp.max(sq, axis=0, keepdims=True), (SUBLANES // 2, LANES))

    nrm_ref[0] = jnp.concatenate([max_sq_norm(main_ref[:, OFF_DQ:OFF_RQ]), max_sq_norm(k)], axis=0)

    vt = lax.dot_general(wvt_ref[...], xb, (((1,), (1,)), ((), ())), preferred_element_type=F32)
    row = lax.broadcasted_iota(jnp.int32, vt.shape, 0)
    ones_row = row == DIFF_V_DIM
    for h in range(1, DIFF_HEADS):
        ones_row = ones_row | (row == h * VT_ROWS + DIFF_V_DIM)
    vt_ref[...] = jnp.where(ones_row, 1.0, vt).astype(vt_ref.dtype)


def _inproj(x2, w_in):
    t, d = x2.shape
    tm = PROJ_TM
    cols = lambda a, b: w_in[:, a:b]
    w_main = jnp.concatenate([cols(IN_DQ, IN_DK), cols(IN_RQ, IN_END)], axis=1).astype(BF16)
    w_k = cols(IN_DK, IN_DV).astype(BF16)
    wv = cols(IN_DV, IN_RQ).astype(BF16).T.reshape(DIFF_HEADS, DIFF_V_DIM, d)
    w_vt = jnp.pad(wv, ((0, 0), (0, VT_ROWS - DIFF_V_DIM), (0, 0))).reshape(DIFF_HEADS * VT_ROWS, d)

    kaug = _alibi_key_columns(tm, ATT_TK)
    sel = (np.arange(DIFF_HEADS * 2 * DIFF_QK_DIM)[:, None] // DIFF_QK_DIM == np.arange(LANES)[None, :])
    sel = jnp.asarray(sel, BF16)

    row = lambda i: (i, 0)
    return pl.pallas_call(
        functools.partial(_inproj_kernel, nc=PROJ_NC),
        grid=(t // tm,),
        in_specs=[pl.BlockSpec((tm, d), row), _resident(w_main.shape), _resident(w_k.shape),
                  _resident(w_vt.shape), _resident(kaug.shape), _resident(sel.shape)],
        out_specs=[pl.BlockSpec((tm, MAIN_W), row),
                   pl.BlockSpec((tm, DIFF_HEADS * 2 * LANES), row),
                   pl.BlockSpec((DIFF_HEADS * VT_ROWS, tm), lambda i: (0, i)),
                   pl.BlockSpec((1, SUBLANES, LANES), lambda i: (i, 0, 0))],
        out_shape=[jax.ShapeDtypeStruct((t, MAIN_W), BF16),
                   jax.ShapeDtypeStruct((t, DIFF_HEADS * 2 * LANES), BF16),
                   jax.ShapeDtypeStruct((DIFF_HEADS * VT_ROWS, t), BF16),
                   jax.ShapeDtypeStruct((t // tm, SUBLANES, LANES), F32)],
        compiler_params=_cparams(("arbitrary",)),
        name="inproj",
    )(x2, w_main, w_k, w_vt, kaug, sel)


def _attn_kernel(slope_ref, first_ref, q_ref, qaug_ref, kp_ref, vt_ref, tdiag_ref, lam_ref, g_ref, o_ref,
                 qs_scr, m_scr, acc_scr, z_scr, zmax_scr, p_scr, alpha_scr, *, tq, tk, ns):
    b, h, qi = pl.program_id(0), pl.program_id(1), pl.program_id(2)
    per_tile = tk // tq
    n0 = (ns // per_tile) * qi
    first = first_ref[b, h, qi]
    slope = slope_ref[h]

    @pl.when((b == 0) & (h == 0) & (qi == 0))
    def _():
        p_scr[...] = jnp.zeros(p_scr.shape, BF16)

    qa = qaug_ref[0]
    for j in range(ns):
        q = q_ref[j * tq:(j + 1) * tq, :]
        lane = lax.broadcasted_iota(jnp.int32, q.shape, 1)
        zero = jnp.zeros_like(q)
        q1 = jnp.concatenate([jnp.where(lane < DIFF_QK_DIM, q, zero), qa], axis=1).astype(F32)
        q2 = jnp.concatenate([jnp.where(lane >= DIFF_QK_DIM, q, zero), qa], axis=1).astype(F32)
        qs_scr[j, :, 0:tq] = q1.T.astype(BF16)
        qs_scr[j, :, tq:2 * tq] = q2.T.astype(BF16)
        m_scr[j] = jnp.full(m_scr.shape[1:], -jnp.inf, F32)
        acc_scr[j] = jnp.zeros(acc_scr.shape[1:], F32)
        alpha_scr[j] = jnp.ones(alpha_scr.shape[1:], F32)

    def score_stage(j, kb, slot, rows=tk):
        start = pl.multiple_of(kb * tk, tk)
        z = jnp.dot(kp_ref[pl.ds(start, rows), :], qs_scr[j],
                    preferred_element_type=F32)
        zmax = None
        for r in range(0, rows, ATT_RB):
            zb = z[r:r + ATT_RB]
            z_scr[j, slot, r:r + ATT_RB] = zb
            bmax = jnp.max(zb, axis=0, keepdims=True)
            zmax = bmax if zmax is None else jnp.maximum(zmax, bmax)
        zmax_scr[j, slot] = zmax

    def softmax_stage(j, z, zmax, c, slot, rows=tk):
        m_prev = m_scr[j]
        m_new = jnp.maximum(m_prev, zmax + c)
        shift = m_new - c
        for r in range(0, rows, ATT_RB):
            p_scr[j, slot, r:r + ATT_RB] = jnp.exp2(z[r:r + ATT_RB] - shift).astype(BF16)
        alpha_scr[j, slot] = jnp.exp2(m_prev - m_new)
        m_scr[j] = m_new

    def value_stage(j, kb, slot, rows=tk):
        start = pl.multiple_of(jnp.maximum(kb, first) * tk, tk)
        v = vt_ref[:, pl.ds(start, rows)]
        v = jnp.where(kb >= first, v, jnp.zeros_like(v))
        acc_scr[j] = alpha_scr[j, slot] * acc_scr[j] + jnp.dot(v, p_scr[j, slot, 0:rows],
                                                               preferred_element_type=F32)

    def tile_bias(j, kb):
        return -slope * ((n0 - kb) * tk + j * tq).astype(F32)

    def full_trip(kb, cur, nxt):
        for j in range(ns):
            score_stage(j, kb + 1, nxt)
            softmax_stage(j, z_scr.at[j, cur], zmax_scr[j, cur], tile_bias(j, kb), cur)
            value_stage(j, kb - 1, nxt)

    lead = (n0 - first) % ATT_UNROLL
    base = first + lead

    @pl.when(lead == 0)
    def _():
        for j in range(ns):
            score_stage(j, first, 0)

    @pl.when(lead == 1)
    def _():
        for j in range(ns):
            score_stage(j, first, 1)
        full_trip(first, 1, 0)

    def unrolled(i, carry):
        for u in range(ATT_UNROLL):
            full_trip(base + ATT_UNROLL * i + u, u % 2, 1 - u % 2)
        return carry

    lax.fori_loop(0, (n0 - base) // ATT_UNROLL, unrolled, 0)

    for i in range(ns // per_tile + 1):
        kb = n0 + i
        cur, nxt = i % 2, 1 - i % 2
        for j in range(ns):
            last = j // per_tile
            seen = (j % per_tile + 1) * tq
            if i < last:
                score_stage(j, kb + 1, nxt, rows=seen if i + 1 == last else tk)
                softmax_stage(j, z_scr.at[j, cur], zmax_scr[j, cur], tile_bias(j, kb), cur)
            elif i == last:
                td = tdiag_ref[0, j % per_tile, 0:seen, :]
                z = z_scr[j, cur, 0:seen]
                z = jnp.concatenate([z[:, 0:tq] + td, z[:, tq:2 * tq] + td], axis=1)
                softmax_stage(j, z, jnp.max(z, axis=0, keepdims=True), tile_bias(j, kb), cur, rows=seen)
            if i <= last + 1:
                value_stage(j, kb - 1, nxt, rows=seen if i == last + 1 else tk)

    lp = lam_ref[...]
    lam = (jnp.exp(jnp.sum(lp[0:1] * lp[1:2], axis=-1, keepdims=True))
           - jnp.exp(jnp.sum(lp[2:3] * lp[3:4], axis=-1, keepdims=True)) + LAMBDA_INIT)
    for j in range(ns):
        acc = acc_scr[j]
        o = acc[0:DIFF_V_DIM, :] / acc[DIFF_V_DIM:DIFF_V_DIM + 1, :]
        y = o[:, 0:tq] - lam * o[:, tq:2 * tq]
        y = y * lax.rsqrt(jnp.mean(y * y, axis=0, keepdims=True) + LN_EPS)
        o_ref[j * tq:(j + 1) * tq, :] = (y.T * g_ref[...] * (1.0 - LAMBDA_INIT)).astype(o_ref.dtype)


def _alibi_key_columns(n_rows, tk):
    l3 = _bf16_split3(LOG2E)
    off = np.arange(n_rows) % tk
    kaug = np.zeros((n_rows, DIFF_HEADS, LANES), np.float32)
    for hh, m in enumerate(_alibi_slopes()):
        for i in range(3):
            kaug[:, hh, i] = np.float32(l3[i])
            kaug[:, hh, 3 + i] = m * (off % ATT_TQ)
            kaug[:, hh, 6 + i] = m * (off - off % ATT_TQ)
    return jnp.asarray(kaug.reshape(n_rows, DIFF_HEADS * LANES), BF16)


def _attn_tables(tq, tk):
    slopes = _alibi_slopes()
    l3 = _bf16_split3(LOG2E)
    il = np.arange(tq, dtype=np.float64)
    qaug = np.zeros((DIFF_HEADS, tq, LANES), np.float32)
    for hh, m in enumerate(slopes):
        for i in range(3):
            qaug[hh, :, i] = -m * il
            qaug[hh, :, 3 + i] = np.float32(l3[i])
            qaug[hh, :, 6 + i] = np.float32(l3[i])
    j = np.arange(tk)[None, :, None]
    i = np.arange(tq)[None, None, :] + tq * np.arange(tk // tq)[:, None, None]
    allowed = (j // CHUNK) <= (i // CHUNK)
    corr = LOG2E * slopes[:, None, None, None] * ((i - j) - np.abs(i - j))[None]
    tdiag = np.where(allowed[None], corr, -np.inf)
    return (jnp.asarray(slopes * LOG2E, F32), jnp.asarray(qaug, BF16), jnp.asarray(tdiag, F32))


def _first_key_tiles(nrm, batch, seq):
    tiles_per_seq = seq // PROJ_TM
    tsup = ATT_TQ * ATT_NS
    n = nrm.reshape(batch, tiles_per_seq, 2, SUBLANES // 2, LANES)[:, :, :, 0, :DIFF_HEADS * 2]
    n = n.reshape(batch, tiles_per_seq, 2, DIFF_HEADS, 2).max(axis=-1)
    q2 = n[:, :, 0].reshape(batch, seq // tsup, tsup // PROJ_TM, DIFF_HEADS).max(axis=2)
    k2 = n[:, :, 1].max(axis=1)
    bound = jnp.sqrt(q2 * k2[:, None, :]) * ATT_BOUND_SCALE + ATT_BOUND_SLACK
    slope = jnp.asarray(_alibi_slopes() * LOG2E, F32)
    reach = (2.0 * bound + ATT_UNDERFLOW_LOG2) / slope
    needed = jnp.floor(jnp.maximum(reach - 1.0, 0.0) / ATT_TK).astype(jnp.int32) + 1
    n0 = (tsup // ATT_TK) * jnp.arange(seq // tsup, dtype=jnp.int32)[None, :, None]
    first = jnp.maximum(n0 - needed, 0)
    return first.transpose(0, 2, 1)


def _diff_attention(main, kp, vt, nrm, lam_p, subln_g, batch, seq):
    tq, tk, ns = ATT_TQ, ATT_TK, ATT_NS
    tsup = tq * ns
    nq = seq // tsup
    slopes, qaug, tdiag = _attn_tables(tq, tk)
    first = _first_key_tiles(nrm, batch, seq)
    return pl.pallas_call(
        functools.partial(_attn_kernel, tq=tq, tk=tk, ns=ns),
        grid=(batch, DIFF_HEADS, nq),
        in_specs=[
            pl.BlockSpec(memory_space=pltpu.SMEM),
            pl.BlockSpec(memory_space=pltpu.SMEM),
            pl.BlockSpec((tsup, LANES), lambda b, h, qi: (b * nq + qi, OFF_DQ // LANES + h)),
            pl.BlockSpec((1, tq, LANES), lambda b, h, qi: (h, 0, 0)),
            pl.BlockSpec((seq, 2 * LANES), lambda b, h, qi: (b, h)),
            pl.BlockSpec((VT_ROWS, seq), lambda b, h, qi: (h, b)),
            pl.BlockSpec((1,) + tdiag.shape[1:], lambda b, h, qi: (h, 0, 0, 0)),
            pl.BlockSpec(lam_p.shape, lambda b, h, qi: (0, 0)),
            pl.BlockSpec((1, LANES), lambda b, h, qi: (0, h)),
        ],
        out_specs=pl.BlockSpec((tsup, LANES), lambda b, h, qi: (b * nq + qi, h)),
        out_shape=jax.ShapeDtypeStruct((batch * seq, DIFF_HEADS * DIFF_V_DIM), BF16),
        scratch_shapes=[
            pltpu.VMEM((ns, 2 * LANES, 2 * tq), BF16),
            pltpu.VMEM((ns, 1, 2 * tq), F32),
            pltpu.VMEM((ns, VT_ROWS, 2 * tq), F32),
            pltpu.VMEM((ns, 2, tk, 2 * tq), F32),
            pltpu.VMEM((ns, 2, 1, 2 * tq), F32),
            pltpu.VMEM((ns, 2, tk, 2 * tq), BF16),
            pltpu.VMEM((ns, 2, 1, 2 * tq), F32),
        ],
        compiler_params=_cparams(("arbitrary", "arbitrary", "arbitrary")),
        name="diff_attn",
    )(slopes, first, main, qaug, kp, vt, tdiag, lam_p, subln_g)


def _retention_chunk(rows, q_ref, k_ref, v_ref, g_ref, dtab_ref, qdec_ref, kdec_ref, cdec_ref, gn_ref,
                     state_scr, r_scr):
    c = RET_C
    lane = lax.broadcasted_iota(jnp.int32, (c, LANES), 1)
    for p in range(RET_HEADS // 2):
        qp = q_ref[rows, p * LANES:(p + 1) * LANES]
        kp = k_ref[rows, p * LANES:(p + 1) * LANES]
        vp = v_ref[rows, p * 2 * RET_V_DIM:(p + 1) * 2 * RET_V_DIM]
        state = state_scr[p]
        state_b = state.astype(BF16)
        qdec = qdec_ref[:, p * LANES:(p + 1) * LANES]
        for e in range(2):
            head = 2 * p + e
            keep = (lane < RET_QK_DIM) if e == 0 else (lane >= RET_QK_DIM)
            qm = jnp.where(keep, qp, jnp.zeros_like(qp))
            s = lax.dot_general(qm, kp, (((1,), (1,)), ((), ())), preferred_element_type=F32)
            s = s * dtab_ref[head]
            ve = vp[:, e * RET_V_DIM:(e + 1) * RET_V_DIM]
            intra = jnp.dot(s.astype(BF16), ve, preferred_element_type=F32)
            qd = (qm.astype(F32) * qdec).astype(BF16)
            cross = jnp.dot(qd, state_b[:, e * RET_V_DIM:(e + 1) * RET_V_DIM],
                            preferred_element_type=F32)
            o = intra + cross
            mu = jnp.mean(o, axis=-1, keepdims=True)
            d = o - mu
            var = jnp.mean(d * d, axis=-1, keepdims=True)
            cols = slice(head * RET_V_DIM, (head + 1) * RET_V_DIM)
            gate = g_ref[rows, cols].astype(F32)
            y = d * lax.rsqrt(var + LN_EPS) * gn_ref[:, cols] * (gate * jax.nn.sigmoid(gate))
            r_scr[rows, cols] = y.astype(r_scr.dtype)
        kd = (kp.astype(F32) * kdec_ref[:, p * LANES:(p + 1) * LANES]).astype(BF16)
        kv = lax.dot_general(kd, vp, (((0,), (0,)), ((), ())), preferred_element_type=F32)
        state_scr[p] = state * cdec_ref[p] + kv


def _ret_tables(c):
    hh = np.arange(RET_HEADS, dtype=np.float64)
    lg = np.log(1.0 - 2.0 ** (-5.0 - hh))
    idx = np.arange(c, dtype=np.float64)
    nm = idx[:, None] - idx[None, :]
    dtab = np.where(nm >= 0, np.exp(lg[:, None, None] * np.maximum(nm, 0.0)), 0.0) * RET_QK_DIM ** -0.5
    qdec = np.exp(lg[None, :] * (idx + 1.0)[:, None])
    kdec = np.exp(lg[None, :] * (c - 1.0 - idx)[:, None]) * RET_QK_DIM ** -0.5
    qdec = np.repeat(qdec, RET_QK_DIM, axis=1)
    kdec = np.repeat(kdec, RET_QK_DIM, axis=1)
    cdec = np.repeat(np.exp(lg * c), RET_QK_DIM).reshape(RET_HEADS // 2, 2 * RET_QK_DIM, 1)
    return tuple(jnp.asarray(a, F32) for a in (dtab, qdec, kdec, cdec))


def _merge_kernel(x_ref, a_ref, rq_ref, rk_ref, rv_ref, rg_ref, mq_ref, mk_ref, mv_ref,
                  dtab_ref, qdec_ref, kdec_ref, cdec_ref, gn_ref,
                  wg_ref, bg_ref, wd_ref, wr_ref, wm_ref, wo_ref, g1_ref, b1_ref, o_ref,
                  state_scr, r_scr, m_scr, merged_scr, *, nc, tiles_per_seq):
    tm, d = x_ref.shape
    x = x_ref[...]
    xb = x.astype(BF16)

    @pl.when(pl.program_id(0) % tiles_per_seq == 0)
    def _():
        state_scr[...] = jnp.zeros(state_scr.shape, F32)

    for r0 in range(0, tm, RET_C):
        _retention_chunk(slice(r0, r0 + RET_C), rq_ref, rk_ref, rv_ref, rg_ref, dtab_ref, qdec_ref, kdec_ref,
                         cdec_ref, gn_ref, state_scr, r_scr)

    scale = jnp.asarray(MEM_HEAD_DIM ** -0.5, BF16)
    for h in range(MEM_HEADS):
        cols = slice(h * MEM_HEAD_DIM, (h + 1) * MEM_HEAD_DIM)
        qh = mq_ref[:, cols] * scale
        s = lax.dot_general(qh, mk_ref[0, :, cols], (((1,), (1,)), ((), ())),
                            preferred_element_type=F32)
        p = jnp.exp(s - jnp.max(s, axis=-1, keepdims=True))
        l = jnp.sum(p, axis=-1, keepdims=True)
        mh = jnp.dot(p.astype(BF16), mv_ref[0, :, cols], preferred_element_type=F32) / l
        m_scr[:, cols] = mh.astype(BF16)

    a = a_ref[...]
    r = r_scr[...]
    m = m_scr[...]
    for c in range(0, d, nc):
        cs = slice(c, c + nc)
        acc = None
        for br, (val, w_ref) in enumerate(((a, wd_ref), (r, wr_ref), (m, wm_ref))):
            gs = slice(br * d + c, br * d + c + nc)
            gate = jax.nn.sigmoid(jnp.dot(xb, wg_ref[:, gs], preferred_element_type=F32) + bg_ref[:, gs])
            term = gate * jnp.dot(val, w_ref[:, cs], preferred_element_type=F32)
            acc = term if acc is None else acc + term
        merged_scr[:, cs] = acc.astype(BF16)

    for r0 in range(0, tm, EPI_RB):
        rows = slice(r0, r0 + EPI_RB)
        mix = jnp.dot(merged_scr[rows, :], wo_ref[...], preferred_element_type=F32)
        o_ref[rows, :] = _layer_norm(DEEPNORM_ALPHA * x_ref[rows, :] + mix, g1_ref[...], b1_ref[...])


def _merge(x2, a, main, memkv, ret_g, wg, bg, wd, wr, wm, wo, g1, b1, batch, seq):
    tm = MERGE_TM
    t, d = x2.shape
    nt = seq // tm
    mlen = memkv.shape[1]
    mw = MEM_HEADS * MEM_HEAD_DIM
    qkw = RET_HEADS * RET_QK_DIM
    vw = RET_HEADS * RET_V_DIM
    dtab, qdec, kdec, cdec = _ret_tables(RET_C)
    row = lambda i: (i, 0)
    return pl.pallas_call(
        functools.partial(_merge_kernel, nc=MERGE_NC, tiles_per_seq=nt),
        grid=(t // tm,),
        in_specs=[
            pl.BlockSpec((tm, d), row),
            pl.BlockSpec((tm, a.shape[1]), row),
            pl.BlockSpec((tm, qkw), lambda i: (i, OFF_RQ // qkw)),
            pl.BlockSpec((tm, qkw), lambda i: (i, OFF_RK // qkw)),
            pl.BlockSpec((tm, vw), lambda i: (i, OFF_RV // vw)),
            pl.BlockSpec((tm, vw), lambda i: (i, OFF_RG // vw)),
            pl.BlockSpec((tm, mw), lambda i: (i, OFF_MQ // mw)),
            pl.BlockSpec((1, mlen, mw), lambda i: (i // nt, 0, 0)),
            pl.BlockSpec((1, mlen, mw), lambda i: (i // nt, 0, 1)),
            _resident(dtab.shape), _resident(qdec.shape), _resident(kdec.shape), _resident(cdec.shape),
            _resident(ret_g.shape),
            _resident(wg.shape), _resident(bg.shape), _resident(wd.shape), _resident(wr.shape),
            _resident(wm.shape), _resident(wo.shape), _resident(g1.shape), _resident(b1.shape),
        ],
        out_specs=pl.BlockSpec((tm, d), row),
        out_shape=jax.ShapeDtypeStruct((t, d), F32),
        scratch_shapes=[
            pltpu.VMEM((RET_HEADS // 2, 2 * RET_QK_DIM, 2 * RET_V_DIM), F32),
            pltpu.VMEM((tm, vw), BF16),
            pltpu.VMEM((tm, mw), BF16),
            pltpu.VMEM((tm, d), BF16),
        ],
        compiler_params=_cparams(("arbitrary",)),
        name="merge",
    )(x2, a, main, main, main, main, main, memkv, memkv, dtab, qdec, kdec, cdec, ret_g,
      wg, bg, wd, wr, wm, wo, g1, b1)


def _ffn_kernel(h_ref, wu_ref, cw_ref, cb_ref, wdn_ref, g2_ref, b2_ref, o_ref,
                carry_scr, hs_scr, act_scr, *, fc, rb, ffn_dim, tiles_per_seq):
    tm = h_ref.shape[0]
    halo = SUBLANES

    @pl.when(pl.program_id(0) % tiles_per_seq == 0)
    def _():
        carry_scr[...] = jnp.zeros(carry_scr.shape, F32)

    x = h_ref[...]
    xb = x.astype(BF16)

    def up_project(slot, col):
        cs = slice(col, col + fc)
        up = jnp.dot(xb, wu_ref[:, cs], preferred_element_type=F32)
        hs_scr[slot, 0:halo, :] = carry_scr[:, cs]
        hs_scr[slot, halo:halo + tm, :] = up
        carry_scr[:, cs] = up[tm - halo:tm, :]

    def conv_rows(slot, col, r0):
        cs = slice(col, col + fc)
        blk = hs_scr[slot, r0:r0 + rb + halo, :]
        w = cw_ref[:, cs]
        y = pltpu.roll(blk, 2, 0) * w[0:1] + pltpu.roll(blk, 1, 0) * w[1:2] + blk * w[2:3] + cb_ref[:, cs]
        return y[halo:, :]

    for j, c in enumerate(range(0, ffn_dim, fc)):
        sg, sv = 2 * (j % 2), 2 * (j % 2) + 1
        up_project(sg, c)
        up_project(sv, ffn_dim + c)
        for r0 in range(0, tm, rb):
            g = conv_rows(sg, c, r0)
            val = conv_rows(sv, ffn_dim + c, r0)
            act_scr[r0:r0 + rb, c:c + fc] = (g * jax.nn.sigmoid(g) * val).astype(BF16)

    for r0 in range(0, tm, EPI_RB):
        rows = slice(r0, r0 + EPI_RB)
        ffn = jnp.dot(act_scr[rows, :], wdn_ref[...], preferred_element_type=F32)
        o_ref[rows, :] = _layer_norm(DEEPNORM_ALPHA * h_ref[rows, :] + ffn, g2_ref[...], b2_ref[...])


def _ffn(h1, wu, cw, cb, wdn, g2, b2, seq):
    tm = FFN_TM
    t, d = h1.shape
    ffn_dim = wdn.shape[0]
    row = lambda i: (i, 0)
    return pl.pallas_call(
        functools.partial(_ffn_kernel, fc=FFN_FC, rb=FFN_RB, ffn_dim=ffn_dim, tiles_per_seq=seq // tm),
        grid=(t // tm,),
        in_specs=[
            pl.BlockSpec((tm, d), row),
            _resident(wu.shape), _resident(cw.shape), _resident(cb.shape), _resident(wdn.shape),
            _resident(g2.shape), _resident(b2.shape),
        ],
        out_specs=pl.BlockSpec((tm, d), row),
        out_shape=jax.ShapeDtypeStruct((t, d), F32),
        scratch_shapes=[
            pltpu.VMEM((SUBLANES, 2 * ffn_dim), F32),
            pltpu.VMEM((4, tm + SUBLANES, FFN_FC), F32),
            pltpu.VMEM((tm, ffn_dim), BF16),
        ],
        compiler_params=_cparams(("arbitrary",)),
        name="conv_ffn",
    )(h1, wu, cw, cb, wdn, g2, b2)


def kernel(x, mem, w_in, diff_lambda, diff_subln_g, ret_norm_g, w_mem_kv, w_diff_o, w_ret_o, w_mem_o,
           w_gate, b_gate, w_mix_out, ln1_g, ln1_b, w_up, conv_w, conv_b, w_down, ln2_g, ln2_b):
    batch, seq, d = x.shape
    mlen = mem.shape[1]
    assert w_in.shape[0] == DEPTH == 1 and conv_w.shape[1] == CONV_WIDTH == 3
    assert seq % PROJ_TM == 0 and MERGE_TM % RET_C == 0 and seq % FFN_TM == 0 and seq % MERGE_TM == 0
    assert PROJ_TM % ATT_TK == 0 and ATT_TQ % CHUNK == 0 and ATT_TK % ATT_TQ == 0 and seq % (ATT_TQ * ATT_NS) == 0
    assert ATT_UNROLL == 2 and (ATT_NS * ATT_TQ // ATT_TK) % ATT_UNROLL == 0
    l = 0
    row = lambda v: v[l].reshape(1, -1)

    x2 = x.reshape(batch * seq, d)
    main, kp, vt, nrm = _inproj(x2, w_in[l])
    memkv = _proj(mem.reshape(batch * mlen, d), w_mem_kv[l].astype(BF16), batch * mlen, PROJ_NC)
    memkv = memkv.reshape(batch, mlen, 2 * MEM_HEADS * MEM_HEAD_DIM)

    a = _diff_attention(main, kp, vt, nrm, diff_lambda[l], row(diff_subln_g), batch, seq)
    h1 = _merge(x2, a, main, memkv, row(ret_norm_g), w_gate[l].astype(BF16), row(b_gate),
                w_diff_o[l].astype(BF16), w_ret_o[l].astype(BF16), w_mem_o[l].astype(BF16),
                w_mix_out[l].astype(BF16), row(ln1_g), row(ln1_b), batch, seq)
    out = _ffn(h1, w_up[l].astype(BF16), conv_w[l], row(conv_b), w_down[l].astype(BF16),
               row(ln2_g), row(ln2_b), seq)
    return out.reshape(batch, seq, d)
```

```python
import functools
import math

import numpy as np
import jax
import jax.numpy as jnp
from jax import lax
from jax.experimental import pallas as pl
from jax.experimental.pallas import tpu as pltpu

F32 = jnp.float32
BF16 = jnp.bfloat16

CHUNK = 64
DIFF_HEADS = 4
DIFF_QK_DIM = 64
DIFF_V_DIM = 128
RET_HEADS = 4
RET_QK_DIM = 64
RET_V_DIM = 128
MEM_HEADS = 4
MEM_HEAD_DIM = 128
CONV_WIDTH = 3
LN_EPS = 1e-5
DEPTH = 1
DEEPNORM_ALPHA = (2.0 * DEPTH) ** 0.25
LAMBDA_INIT = 0.8 - 0.6 * math.exp(-0.3 * 0)
LOG2E = math.log2(math.e)

IN_DQ, IN_DK, IN_DV = 0, 512, 1024
IN_RQ, IN_END = 1536, 3584
OFF_DQ, OFF_RQ, OFF_RK, OFF_RV, OFF_RG, OFF_MQ, MAIN_W = 0, 512, 768, 1024, 1536, 2048, 2560

LANES = 128
SUBLANES = 8
BF16_ROWS = 16
VMEM_LIMIT = 56 * 1024 * 1024

PROJ_TM = 512
PROJ_NC = 512
ATT_TQ = 256
ATT_TK = 512
ATT_NS = 4
ATT_UNROLL = 2
ATT_RB = 256
ATT_UNDERFLOW_LOG2 = 134.0
ATT_BOUND_SCALE = 1.02
ATT_BOUND_SLACK = 0.05
VT_ROWS = DIFF_V_DIM + BF16_ROWS
RET_C = 256
MERGE_TM = 512
MERGE_NC = 512
FFN_TM = 512
FFN_FC = 256
FFN_RB = 128
EPI_RB = 256


def _cparams(sem):
    return pltpu.CompilerParams(dimension_semantics=sem, vmem_limit_bytes=VMEM_LIMIT)


def _resident(shape):
    nd = len(shape)
    return pl.BlockSpec(shape, lambda *_: (0,) * nd, pipeline_mode=pl.Buffered(1))


def _layer_norm(y, g, b):
    mu = jnp.mean(y, axis=-1, keepdims=True)
    d = y - mu
    var = jnp.mean(d * d, axis=-1, keepdims=True)
    return d * lax.rsqrt(var + LN_EPS) * g + b


def _alibi_slopes():
    return np.array([(2.0 ** (-8.0 / DIFF_HEADS)) ** (i + 1) for i in range(DIFF_HEADS)], np.float64)


def _bf16_split3(v):
    parts = []
    rem = np.float64(v)
    for _ in range(3):
        p = np.asarray(rem, np.float32).astype(jnp.bfloat16)
        parts.append(p)
        rem = rem - np.float64(p.astype(np.float32))
    return parts


def _proj_kernel(x_ref, w_ref, o_ref, *, nc):
    xb = x_ref[...].astype(BF16)
    n = w_ref.shape[1]
    for c in range(0, n, nc):
        o_ref[:, c:c + nc] = jnp.dot(xb, w_ref[:, c:c + nc],
                                     preferred_element_type=F32).astype(o_ref.dtype)


def _proj(x2, w, tm, nc):
    t, d = x2.shape
    n = w.shape[1]
    return pl.pallas_call(
        functools.partial(_proj_kernel, nc=nc),
        grid=(t // tm,),
        in_specs=[pl.BlockSpec((tm, d), lambda i: (i, 0)), _resident((d, n))],
        out_specs=pl.BlockSpec((tm, n), lambda i: (i, 0)),
        out_shape=jax.ShapeDtypeStruct((t, n), BF16),
        compiler_params=_cparams(("arbitrary",)),
        name="memkv",
    )(x2, w)


def _inproj_kernel(x_ref, wm_ref, wk_ref, wvt_ref, kaug_ref, sel_ref, main_ref, kp_ref, vt_ref, nrm_ref, *, nc):
    xb = x_ref[...].astype(BF16)
    qscale = LOG2E * DIFF_QK_DIM ** -0.5
    for c in range(0, MAIN_W, nc):
        acc = jnp.dot(xb, wm_ref[:, c:c + nc], preferred_element_type=F32)
        if c < OFF_RQ:
            acc = acc * qscale
        main_ref[:, c:c + nc] = acc.astype(main_ref.dtype)

    k = jnp.dot(xb, wk_ref[...], preferred_element_type=F32).astype(kp_ref.dtype)
    for h in range(DIFF_HEADS):
        kp_ref[:, h * 2 * LANES:h * 2 * LANES + LANES] = k[:, h * LANES:(h + 1) * LANES]
        kp_ref[:, h * 2 * LANES + LANES:(h + 1) * 2 * LANES] = kaug_ref[:, h * LANES:(h + 1) * LANES]

    def max_sq_norm(v):
        vf = v.astype(F32)
        sq = jnp.dot((vf * vf).astype(BF16), sel_ref[...], preferred_element_type=F32)
        return jnp.broadcast_to(jnp.max(sq, axis=0, keepdims=True), (SUBLANES // 2, LANES))

    nrm_ref[0] = jnp.concatenate([max_sq_norm(main_ref[:, OFF_DQ:OFF_RQ]), max_sq_norm(k)], axis=0)

    vt = lax.dot_general(wvt_ref[...], xb, (((1,), (1,)), ((), ())), preferred_element_type=F32)
    row = lax.broadcasted_iota(jnp.int32, vt.shape, 0)
    ones_row = row == DIFF_V_DIM
    for h in range(1, DIFF_HEADS):
        ones_row = ones_row | (row == h * VT_ROWS + DIFF_V_DIM)
    vt_ref[...] = jnp.where(ones_row, 1.0, vt).astype(vt_ref.dtype)


def _inproj(x2, w_in):
    t, d = x2.shape
    tm = PROJ_TM
    cols = lambda a, b: w_in[:, a:b]
    w_main = jnp.concatenate([cols(IN_DQ, IN_DK), cols(IN_RQ, IN_END)], axis=1).astype(BF16)
    w_k = cols(IN_DK, IN_DV).astype(BF16)
    wv = cols(IN_DV, IN_RQ).astype(BF16).T.reshape(DIFF_HEADS, DIFF_V_DIM, d)
    w_vt = jnp.pad(wv, ((0, 0), (0, VT_ROWS - DIFF_V_DIM), (0, 0))).reshape(DIFF_HEADS * VT_ROWS, d)

    kaug = _alibi_key_columns(tm, ATT_TK)
    sel = (np.arange(DIFF_HEADS * 2 * DIFF_QK_DIM)[:, None] // DIFF_QK_DIM == np.arange(LANES)[None, :])
    sel = jnp.asarray(sel, BF16)

    row = lambda i: (i, 0)
    return pl.pallas_call(
        functools.partial(_inproj_kernel, nc=PROJ_NC),
        grid=(t // tm,),
        in_specs=[pl.BlockSpec((tm, d), row), _resident(w_main.shape), _resident(w_k.shape),
                  _resident(w_vt.shape), _resident(kaug.shape), _resident(sel.shape)],
        out_specs=[pl.BlockSpec((tm, MAIN_W), row),
                   pl.BlockSpec((tm, DIFF_HEADS * 2 * LANES), row),
                   pl.BlockSpec((DIFF_HEADS * VT_ROWS, tm), lambda i: (0, i)),
                   pl.BlockSpec((1, SUBLANES, LANES), lambda i: (i, 0, 0))],
        out_shape=[jax.ShapeDtypeStruct((t, MAIN_W), BF16),
                   jax.ShapeDtypeStruct((t, DIFF_HEADS * 2 * LANES), BF16),
                   jax.ShapeDtypeStruct((DIFF_HEADS * VT_ROWS, t), BF16),
                   jax.ShapeDtypeStruct((t // tm, SUBLANES, LANES), F32)],
        compiler_params=_cparams(("arbitrary",)),
        name="inproj",
    )(x2, w_main, w_k, w_vt, kaug, sel)


def _attn_kernel(slope_ref, first_ref, q_ref, qaug_ref, kp_ref, vt_ref, tdiag_ref, lam_ref, g_ref, o_ref,
                 qs_scr, m_scr, acc_scr, z_scr, zmax_scr, p_scr, alpha_scr, *, tq, tk, ns):
    b, h, qi = pl.program_id(0), pl.program_id(1), pl.program_id(2)
    per_tile = tk // tq
    n0 = (ns // per_tile) * qi
    first = first_ref[b, h, qi]
    slope = slope_ref[h]

    @pl.when((b == 0) & (h == 0) & (qi == 0))
    def _():
        p_scr[...] = jnp.zeros(p_scr.shape, BF16)

    qa = qaug_ref[0]
    for j in range(ns):
        q = q_ref[j * tq:(j + 1) * tq, :]
        lane = lax.broadcasted_iota(jnp.int32, q.shape, 1)
        zero = jnp.zeros_like(q)
        q1 = jnp.concatenate([jnp.where(lane < DIFF_QK_DIM, q, zero), qa], axis=1).astype(F32)
        q2 = jnp.concatenate([jnp.where(lane >= DIFF_QK_DIM, q, zero), qa], axis=1).astype(F32)
        qs_scr[j, :, 0:tq] = q1.T.astype(BF16)
        qs_scr[j, :, tq:2 * tq] = q2.T.astype(BF16)
        m_scr[j] = jnp.full(m_scr.shape[1:], -jnp.inf, F32)
        acc_scr[j] = jnp.zeros(acc_scr.shape[1:], F32)
        alpha_scr[j] = jnp.ones(alpha_scr.shape[1:], F32)

    def score_stage(j, kb, slot, rows=tk):
        start = pl.multiple_of(kb * tk, tk)
        z = jnp.dot(kp_ref[pl.ds(start, rows), :], qs_scr[j],
                    preferred_element_type=F32)
        zmax = None
        for r in range(0, rows, ATT_RB):
            zb = z[r:r + ATT_RB]
            z_scr[j, slot, r:r + ATT_RB] = zb
            bmax = jnp.max(zb, axis=0, keepdims=True)
            zmax = bmax if zmax is None else jnp.maximum(zmax, bmax)
        zmax_scr[j, slot] = zmax

    def softmax_stage(j, z, zmax, c, slot, rows=tk):
        m_prev = m_scr[j]
        m_new = jnp.maximum(m_prev, zmax + c)
        shift = m_new - c
        for r in range(0, rows, ATT_RB):
            p_scr[j, slot, r:r + ATT_RB] = jnp.exp2(z[r:r + ATT_RB] - shift).astype(BF16)
        alpha_scr[j, slot] = jnp.exp2(m_prev - m_new)
        m_scr[j] = m_new

    def value_stage(j, kb, slot, rows=tk):
        start = pl.multiple_of(jnp.maximum(kb, first) * tk, tk)
        v = vt_ref[:, pl.ds(start, rows)]
        v = jnp.where(kb >= first, v, jnp.zeros_like(v))
        acc_scr[j] = alpha_scr[j, slot] * acc_scr[j] + jnp.dot(v, p_scr[j, slot, 0:rows],
                                                               preferred_element_type=F32)

    def tile_bias(j, kb):
        return -slope * ((n0 - kb) * tk + j * tq).astype(F32)

    def full_trip(kb, cur, nxt):
        for j in range(ns):
            score_stage(j, kb + 1, nxt)
            softmax_stage(j, z_scr.at[j, cur], zmax_scr[j, cur], tile_bias(j, kb), cur)
            value_stage(j, kb - 1, nxt)

    lead = (n0 - first) % ATT_UNROLL
    base = first + lead

    @pl.when(lead == 0)
    def _():
        for j in range(ns):
            score_stage(j, first, 0)

    @pl.when(lead == 1)
    def _():
        for j in range(ns):
            score_stage(j, first, 1)
        full_trip(first, 1, 0)

    def unrolled(i, carry):
        for u in range(ATT_UNROLL):
            full_trip(base + ATT_UNROLL * i + u, u % 2, 1 - u % 2)
        return carry

    lax.fori_loop(0, (n0 - base) // ATT_UNROLL, unrolled, 0)

    for i in range(ns // per_tile + 1):
        kb = n0 + i
        cur, nxt = i % 2, 1 - i % 2
        for j in range(ns):
            last = j // per_tile
            seen = (j % per_tile + 1) * tq
            if i < last:
                score_stage(j, kb + 1, nxt, rows=seen if i + 1 == last else tk)
                softmax_stage(j, z_scr.at[j, cur], zmax_scr[j, cur], tile_bias(j, kb), cur)
            elif i == last:
                td = tdiag_ref[0, j % per_tile, 0:seen, :]
                z = z_scr[j, cur, 0:seen]
                z = jnp.concatenate([z[:, 0:tq] + td, z[:, tq:2 * tq] + td], axis=1)
                softmax_stage(j, z, jnp.max(z, axis=0, keepdims=True), tile_bias(j, kb), cur, rows=seen)
            if i <= last + 1:
                value_stage(j, kb - 1, nxt, rows=seen if i == last + 1 else tk)

    lp = lam_ref[...]
    lam = (jnp.exp(jnp.sum(lp[0:1] * lp[1:2], axis=-1, keepdims=True))
           - jnp.exp(jnp.sum(lp[2:3] * lp[3:4], axis=-1, keepdims=True)) + LAMBDA_INIT)
    for j in range(ns):
        acc = acc_scr[j]
        o = acc[0:DIFF_V_DIM, :] / acc[DIFF_V_DIM:DIFF_V_DIM + 1, :]
        y = o[:, 0:tq] - lam * o[:, tq:2 * tq]
        y = y * lax.rsqrt(jnp.mean(y * y, axis=0, keepdims=True) + LN_EPS)
        o_ref[j * tq:(j + 1) * tq, :] = (y.T * g_ref[...] * (1.0 - LAMBDA_INIT)).astype(o_ref.dtype)


def _alibi_key_columns(n_rows, tk):
    l3 = _bf16_split3(LOG2E)
    off = np.arange(n_rows) % tk
    kaug = np.zeros((n_rows, DIFF_HEADS, LANES), np.float32)
    for hh, m in enumerate(_alibi_slopes()):
        for i in range(3):
            kaug[:, hh, i] = np.float32(l3[i])
            kaug[:, hh, 3 + i] = m * (off % ATT_TQ)
            kaug[:, hh, 6 + i] = m * (off - off % ATT_TQ)
    return jnp.asarray(kaug.reshape(n_rows, DIFF_HEADS * LANES), BF16)


def _attn_tables(tq, tk):
    slopes = _alibi_slopes()
    l3 = _bf16_split3(LOG2E)
    il = np.arange(tq, dtype=np.float64)
    qaug = np.zeros((DIFF_HEADS, tq, LANES), np.float32)
    for hh, m in enumerate(slopes):
        for i in range(3):
            qaug[hh, :, i] = -m * il
            qaug[hh, :, 3 + i] = np.float32(l3[i])
            qaug[hh, :, 6 + i] = np.float32(l3[i])
    j = np.arange(tk)[None, :, None]
    i = np.arange(tq)[None, None, :] + tq * np.arange(tk // tq)[:, None, None]
    allowed = (j // CHUNK) <= (i // CHUNK)
    corr = LOG2E * slopes[:, None, None, None] * ((i - j) - np.abs(i - j))[None]
    tdiag = np.where(allowed[None], corr, -np.inf)
    return (jnp.asarray(slopes * LOG2E, F32), jnp.asarray(qaug, BF16), jnp.asarray(tdiag, F32))


def _first_key_tiles(nrm, batch, seq):
    tiles_per_seq = seq // PROJ_TM
    tsup = ATT_TQ * ATT_NS
    n = nrm.reshape(batch, tiles_per_seq, 2, SUBLANES // 2, LANES)[:, :, :, 0, :DIFF_HEADS * 2]
    n = n.reshape(batch, tiles_per_seq, 2, DIFF_HEADS, 2).max(axis=-1)
    q2 = n[:, :, 0].reshape(batch, seq // tsup, tsup // PROJ_TM, DIFF_HEADS).max(axis=2)
    k2 = n[:, :, 1].max(axis=1)
    bound = jnp.sqrt(q2 * k2[:, None, :]) * ATT_BOUND_SCALE + ATT_BOUND_SLACK
    slope = jnp.asarray(_alibi_slopes() * LOG2E, F32)
    reach = jnp.minimum((2.0 * bound + ATT_UNDERFLOW_LOG2) / slope, float(seq))
    needed = jnp.floor(jnp.maximum(reach - 1.0, 0.0) / ATT_TK).astype(jnp.int32) + 1
    n0 = (tsup // ATT_TK) * jnp.arange(seq // tsup, dtype=jnp.int32)[None, :, None]
    first = jnp.maximum(n0 - needed, 0)
    return first.transpose(0, 2, 1)


def _diff_attention(main, kp, vt, nrm, lam_p, subln_g, batch, seq):
    tq, tk, ns = ATT_TQ, ATT_TK, ATT_NS
    tsup = tq * ns
    nq = seq // tsup
    slopes, qaug, tdiag = _attn_tables(tq, tk)
    first = _first_key_tiles(nrm, batch, seq)
    return pl.pallas_call(
        functools.partial(_attn_kernel, tq=tq, tk=tk, ns=ns),
        grid=(batch, DIFF_HEADS, nq),
        in_specs=[
            pl.BlockSpec(memory_space=pltpu.SMEM),
            pl.BlockSpec(memory_space=pltpu.SMEM),
            pl.BlockSpec((tsup, LANES), lambda b, h, qi: (b * nq + qi, OFF_DQ // LANES + h)),
            pl.BlockSpec((1, tq, LANES), lambda b, h, qi: (h, 0, 0)),
            pl.BlockSpec((seq, 2 * LANES), lambda b, h, qi: (b, h)),
            pl.BlockSpec((VT_ROWS, seq), lambda b, h, qi: (h, b)),
            pl.BlockSpec((1,) + tdiag.shape[1:], lambda b, h, qi: (h, 0, 0, 0)),
            pl.BlockSpec(lam_p.shape, lambda b, h, qi: (0, 0)),
            pl.BlockSpec((1, LANES), lambda b, h, qi: (0, h)),
        ],
        out_specs=pl.BlockSpec((tsup, LANES), lambda b, h, qi: (b * nq + qi, h)),
        out_shape=jax.ShapeDtypeStruct((batch * seq, DIFF_HEADS * DIFF_V_DIM), BF16),
        scratch_shapes=[
            pltpu.VMEM((ns, 2 * LANES, 2 * tq), BF16),
            pltpu.VMEM((ns, 1, 2 * tq), F32),
            pltpu.VMEM((ns, VT_ROWS, 2 * tq), F32),
            pltpu.VMEM((ns, 2, tk, 2 * tq), F32),
            pltpu.VMEM((ns, 2, 1, 2 * tq), F32),
            pltpu.VMEM((ns, 2, tk, 2 * tq), BF16),
            pltpu.VMEM((ns, 2, 1, 2 * tq), F32),
        ],
        compiler_params=_cparams(("arbitrary", "arbitrary", "arbitrary")),
        name="diff_attn",
    )(slopes, first, main, qaug, kp, vt, tdiag, lam_p, subln_g)


def _retention_chunk(rows, q_ref, k_ref, v_ref, g_ref, dtab_ref, qdec_ref, kdec_ref, cdec_ref, gn_ref,
                     state_scr, r_scr):
    c = RET_C
    lane = lax.broadcasted_iota(jnp.int32, (c, LANES), 1)
    for p in range(RET_HEADS // 2):
        qp = q_ref[rows, p * LANES:(p + 1) * LANES]
        kp = k_ref[rows, p * LANES:(p + 1) * LANES]
        vp = v_ref[rows, p * 2 * RET_V_DIM:(p + 1) * 2 * RET_V_DIM]
        state = state_scr[p]
        state_b = state.astype(BF16)
        qdec = qdec_ref[:, p * LANES:(p + 1) * LANES]
        for e in range(2):
            head = 2 * p + e
            keep = (lane < RET_QK_DIM) if e == 0 else (lane >= RET_QK_DIM)
            qm = jnp.where(keep, qp, jnp.zeros_like(qp))
            s = lax.dot_general(qm, kp, (((1,), (1,)), ((), ())), preferred_element_type=F32)
            s = s * dtab_ref[head]
            ve = vp[:, e * RET_V_DIM:(e + 1) * RET_V_DIM]
            intra = jnp.dot(s.astype(BF16), ve, preferred_element_type=F32)
            qd = (qm.astype(F32) * qdec).astype(BF16)
            cross = jnp.dot(qd, state_b[:, e * RET_V_DIM:(e + 1) * RET_V_DIM],
                            preferred_element_type=F32)
            o = intra + cross
            mu = jnp.mean(o, axis=-1, keepdims=True)
            d = o - mu
            var = jnp.mean(d * d, axis=-1, keepdims=True)
            cols = slice(head * RET_V_DIM, (head + 1) * RET_V_DIM)
            gate = g_ref[rows, cols].astype(F32)
            y = d * lax.rsqrt(var + LN_EPS) * gn_ref[:, cols] * (gate * jax.nn.sigmoid(gate))
            r_scr[rows, cols] = y.astype(r_scr.dtype)
        kd = (kp.astype(F32) * kdec_ref[:, p * LANES:(p + 1) * LANES]).astype(BF16)
        kv = lax.dot_general(kd, vp, (((0,), (0,)), ((), ())), preferred_element_type=F32)
        state_scr[p] = state * cdec_ref[p] + kv


def _ret_tables(c):
    hh = np.arange(RET_HEADS, dtype=np.float64)
    lg = np.log(1.0 - 2.0 ** (-5.0 - hh))
    idx = np.arange(c, dtype=np.float64)
    nm = idx[:, None] - idx[None, :]
    dtab = np.where(nm >= 0, np.exp(lg[:, None, None] * np.maximum(nm, 0.0)), 0.0) * RET_QK_DIM ** -0.5
    qdec = np.exp(lg[None, :] * (idx + 1.0)[:, None])
    kdec = np.exp(lg[None, :] * (c - 1.0 - idx)[:, None]) * RET_QK_DIM ** -0.5
    qdec = np.repeat(qdec, RET_QK_DIM, axis=1)
    kdec = np.repeat(kdec, RET_QK_DIM, axis=1)
    cdec = np.repeat(np.exp(lg * c), RET_QK_DIM).reshape(RET_HEADS // 2, 2 * RET_QK_DIM, 1)
    return tuple(jnp.asarray(a, F32) for a in (dtab, qdec, kdec, cdec))


def _merge_kernel(x_ref, a_ref, rq_ref, rk_ref, rv_ref, rg_ref, mq_ref, mk_ref, mv_ref,
                  dtab_ref, qdec_ref, kdec_ref, cdec_ref, gn_ref,
                  wg_ref, bg_ref, wd_ref, wr_ref, wm_ref, wo_ref, g1_ref, b1_ref, o_ref,
                  state_scr, r_scr, m_scr, merged_scr, *, nc, tiles_per_seq):
    tm, d = x_ref.shape
    x = x_ref[...]
    xb = x.astype(BF16)

    @pl.when(pl.program_id(0) % tiles_per_seq == 0)
    def _():
        state_scr[...] = jnp.zeros(state_scr.shape, F32)

    for r0 in range(0, tm, RET_C):
        _retention_chunk(slice(r0, r0 + RET_C), rq_ref, rk_ref, rv_ref, rg_ref, dtab_ref, qdec_ref, kdec_ref,
                         cdec_ref, gn_ref, state_scr, r_scr)

    scale = jnp.asarray(MEM_HEAD_DIM ** -0.5, BF16)
    for h in range(MEM_HEADS):
        cols = slice(h * MEM_HEAD_DIM, (h + 1) * MEM_HEAD_DIM)
        qh = mq_ref[:, cols] * scale
        s = lax.dot_general(qh, mk_ref[0, :, cols], (((1,), (1,)), ((), ())),
                            preferred_element_type=F32)
        p = jnp.exp(s - jnp.max(s, axis=-1, keepdims=True))
        l = jnp.sum(p, axis=-1, keepdims=True)
        mh = jnp.dot(p.astype(BF16), mv_ref[0, :, cols], preferred_element_type=F32) / l
        m_scr[:, cols] = mh.astype(BF16)

    a = a_ref[...]
    r = r_scr[...]
    m = m_scr[...]
    for c in range(0, d, nc):
        cs = slice(c, c + nc)
        acc = None
        for br, (val, w_ref) in enumerate(((a, wd_ref), (r, wr_ref), (m, wm_ref))):
            gs = slice(br * d + c, br * d + c + nc)
            gate = jax.nn.sigmoid(jnp.dot(xb, wg_ref[:, gs], preferred_element_type=F32) + bg_ref[:, gs])
            term = gate * jnp.dot(val, w_ref[:, cs], preferred_element_type=F32)
            acc = term if acc is None else acc + term
        merged_scr[:, cs] = acc.astype(BF16)

    for r0 in range(0, tm, EPI_RB):
        rows = slice(r0, r0 + EPI_RB)
        mix = jnp.dot(merged_scr[rows, :], wo_ref[...], preferred_element_type=F32)
        o_ref[rows, :] = _layer_norm(DEEPNORM_ALPHA * x_ref[rows, :] + mix, g1_ref[...], b1_ref[...])


def _merge(x2, a, main, memkv, ret_g, wg, bg, wd, wr, wm, wo, g1, b1, batch, seq):
    tm = MERGE_TM
    t, d = x2.shape
    nt = seq // tm
    mlen = memkv.shape[1]
    mw = MEM_HEADS * MEM_HEAD_DIM
    qkw = RET_HEADS * RET_QK_DIM
    vw = RET_HEADS * RET_V_DIM
    dtab, qdec, kdec, cdec = _ret_tables(RET_C)
    row = lambda i: (i, 0)
    return pl.pallas_call(
        functools.partial(_merge_kernel, nc=MERGE_NC, tiles_per_seq=nt),
        grid=(t // tm,),
        in_specs=[
            pl.BlockSpec((tm, d), row),
            pl.BlockSpec((tm, a.shape[1]), row),
            pl.BlockSpec((tm, qkw), lambda i: (i, OFF_RQ // qkw)),
            pl.BlockSpec((tm, qkw), lambda i: (i, OFF_RK // qkw)),
            pl.BlockSpec((tm, vw), lambda i: (i, OFF_RV // vw)),
            pl.BlockSpec((tm, vw), lambda i: (i, OFF_RG // vw)),
            pl.BlockSpec((tm, mw), lambda i: (i, OFF_MQ // mw)),
            pl.BlockSpec((1, mlen, mw), lambda i: (i // nt, 0, 0)),
            pl.BlockSpec((1, mlen, mw), lambda i: (i // nt, 0, 1)),
            _resident(dtab.shape), _resident(qdec.shape), _resident(kdec.shape), _resident(cdec.shape),
            _resident(ret_g.shape),
            _resident(wg.shape), _resident(bg.shape), _resident(wd.shape), _resident(wr.shape),
            _resident(wm.shape), _resident(wo.shape), _resident(g1.shape), _resident(b1.shape),
        ],
        out_specs=pl.BlockSpec((tm, d), row),
        out_shape=jax.ShapeDtypeStruct((t, d), F32),
        scratch_shapes=[
            pltpu.VMEM((RET_HEADS // 2, 2 * RET_QK_DIM, 2 * RET_V_DIM), F32),
            pltpu.VMEM((tm, vw), BF16),
            pltpu.VMEM((tm, mw), BF16),
            pltpu.VMEM((tm, d), BF16),
        ],
        compiler_params=_cparams(("arbitrary",)),
        name="merge",
    )(x2, a, main, main, main, main, main, memkv, memkv, dtab, qdec, kdec, cdec, ret_g,
      wg, bg, wd, wr, wm, wo, g1, b1)


def _ffn_kernel(h_ref, wu_ref, cw_ref, cb_ref, wdn_ref, g2_ref, b2_ref, o_ref,
                carry_scr, hs_scr, act_scr, *, fc, rb, ffn_dim, tiles_per_seq):
    tm = h_ref.shape[0]
    halo = SUBLANES

    @pl.when(pl.program_id(0) % tiles_per_seq == 0)
    def _():
        carry_scr[...] = jnp.zeros(carry_scr.shape, F32)

    x = h_ref[...]
    xb = x.astype(BF16)

    def up_project(slot, col):
        cs = slice(col, col + fc)
        up = jnp.dot(xb, wu_ref[:, cs], preferred_element_type=F32)
        hs_scr[slot, 0:halo, :] = carry_scr[:, cs]
        hs_scr[slot, halo:halo + tm, :] = up
        carry_scr[:, cs] = up[tm - halo:tm, :]

    def conv_rows(slot, col, r0):
        cs = slice(col, col + fc)
        blk = hs_scr[slot, r0:r0 + rb + halo, :]
        w = cw_ref[:, cs]
        y = pltpu.roll(blk, 2, 0) * w[0:1] + pltpu.roll(blk, 1, 0) * w[1:2] + blk * w[2:3] + cb_ref[:, cs]
        return y[halo:, :]

    for j, c in enumerate(range(0, ffn_dim, fc)):
        sg, sv = 2 * (j % 2), 2 * (j % 2) + 1
        up_project(sg, c)
        up_project(sv, ffn_dim + c)
        for r0 in range(0, tm, rb):
            g = conv_rows(sg, c, r0)
            val = conv_rows(sv, ffn_dim + c, r0)
            act_scr[r0:r0 + rb, c:c + fc] = (g * jax.nn.sigmoid(g) * val).astype(BF16)

    for r0 in range(0, tm, EPI_RB):
        rows = slice(r0, r0 + EPI_RB)
        ffn = jnp.dot(act_scr[rows, :], wdn_ref[...], preferred_element_type=F32)
        o_ref[rows, :] = _layer_norm(DEEPNORM_ALPHA * h_ref[rows, :] + ffn, g2_ref[...], b2_ref[...])


def _ffn(h1, wu, cw, cb, wdn, g2, b2, seq):
    tm = FFN_TM
    t, d = h1.shape
    ffn_dim = wdn.shape[0]
    row = lambda i: (i, 0)
    return pl.pallas_call(
        functools.partial(_ffn_kernel, fc=FFN_FC, rb=FFN_RB, ffn_dim=ffn_dim, tiles_per_seq=seq // tm),
        grid=(t // tm,),
        in_specs=[
            pl.BlockSpec((tm, d), row),
            _resident(wu.shape), _resident(cw.shape), _resident(cb.shape), _resident(wdn.shape),
            _resident(g2.shape), _resident(b2.shape),
        ],
        out_specs=pl.BlockSpec((tm, d), row),
        out_shape=jax.ShapeDtypeStruct((t, d), F32),
        scratch_shapes=[
            pltpu.VMEM((SUBLANES, 2 * ffn_dim), F32),
            pltpu.VMEM((4, tm + SUBLANES, FFN_FC), F32),
            pltpu.VMEM((tm, ffn_dim), BF16),
        ],
        compiler_params=_cparams(("arbitrary",)),
        name="conv_ffn",
    )(h1, wu, cw, cb, wdn, g2, b2)


def kernel(x, mem, w_in, diff_lambda, diff_subln_g, ret_norm_g, w_mem_kv, w_diff_o, w_ret_o, w_mem_o,
           w_gate, b_gate, w_mix_out, ln1_g, ln1_b, w_up, conv_w, conv_b, w_down, ln2_g, ln2_b):
    batch, seq, d = x.shape
    mlen = mem.shape[1]
    assert w_in.shape[0] == DEPTH == 1 and conv_w.shape[1] == CONV_WIDTH == 3
    assert seq % PROJ_TM == 0 and MERGE_TM % RET_C == 0 and seq % FFN_TM == 0 and seq % MERGE_TM == 0
    assert PROJ_TM % ATT_TK == 0 and ATT_TQ % CHUNK == 0 and ATT_TK % ATT_TQ == 0 and seq % (ATT_TQ * ATT_NS) == 0
    assert ATT_UNROLL == 2 and (ATT_NS * ATT_TQ // ATT_TK) % ATT_UNROLL == 0
    l = 0
    row = lambda v: v[l].reshape(1, -1)

    x2 = x.reshape(batch * seq, d)
    main, kp, vt, nrm = _inproj(x2, w_in[l])
    memkv = _proj(mem.reshape(batch * mlen, d), w_mem_kv[l].astype(BF16), batch * mlen, PROJ_NC)
    memkv = memkv.reshape(batch, mlen, 2 * MEM_HEADS * MEM_HEAD_DIM)

    a = _diff_attention(main, kp, vt, nrm, diff_lambda[l], row(diff_subln_g), batch, seq)
    h1 = _merge(x2, a, main, memkv, row(ret_norm_g), w_gate[l].astype(BF16), row(b_gate),
                w_diff_o[l].astype(BF16), w_ret_o[l].astype(BF16), w_mem_o[l].astype(BF16),
                w_mix_out[l].astype(BF16), row(ln1_g), row(ln1_b), batch, seq)
    out = _ffn(h1, w_up[l].astype(BF16), conv_w[l], row(conv_b), w_down[l].astype(BF16),
               row(ln2_g), row(ln2_b), seq)
    return out.reshape(batch, seq, d)
```

```python
import functools
import math

import numpy as np
import jax
import jax.numpy as jnp
from jax import lax
from jax.experimental import pallas as pl
from jax.experimental.pallas import tpu as pltpu

F32 = jnp.float32
BF16 = jnp.bfloat16

CHUNK = 64
DIFF_HEADS = 4
DIFF_QK_DIM = 64
DIFF_V_DIM = 128
RET_HEADS = 4
RET_QK_DIM = 64
RET_V_DIM = 128
MEM_HEADS = 4
MEM_HEAD_DIM = 128
CONV_WIDTH = 3
LN_EPS = 1e-5
DEPTH = 1
DEEPNORM_ALPHA = (2.0 * DEPTH) ** 0.25
LAMBDA_INIT = 0.8 - 0.6 * math.exp(-0.3 * 0)
LOG2E = math.log2(math.e)

IN_DQ, IN_DK, IN_DV = 0, 512, 1024
IN_RQ, IN_END = 1536, 3584
OFF_DQ, OFF_RQ, OFF_RK, OFF_RV, OFF_RG, OFF_MQ, MAIN_W = 0, 512, 768, 1024, 1536, 2048, 2560

LANES = 128
SUBLANES = 8
BF16_ROWS = 16
VMEM_LIMIT = 56 * 1024 * 1024

PROJ_TM = 512
PROJ_NC = 512
ATT_TQ = 256
ATT_TK = 512
ATT_NS = 4
ATT_UNROLL = 2
ATT_RB = 256
ATT_UNDERFLOW_LOG2 = 134.0
ATT_BOUND_SCALE = 1.02
ATT_BOUND_SLACK = 0.05
VT_ROWS = DIFF_V_DIM + BF16_ROWS
RET_C = 256
MERGE_TM = 512
MERGE_NC = 512
FFN_TM = 512
FFN_FC = 256
FFN_RB = 128
EPI_RB = 256


def _cparams(sem):
    return pltpu.CompilerParams(dimension_semantics=sem, vmem_limit_bytes=VMEM_LIMIT)


def _resident(shape):
    nd = len(shape)
    return pl.BlockSpec(shape, lambda *_: (0,) * nd, pipeline_mode=pl.Buffered(1))


def _layer_norm(y, g, b):
    mu = jnp.mean(y, axis=-1, keepdims=True)
    d = y - mu
    var = jnp.mean(d * d, axis=-1, keepdims=True)
    return d * lax.rsqrt(var + LN_EPS) * g + b


def _alibi_slopes():
    return np.array([(2.0 ** (-8.0 / DIFF_HEADS)) ** (i + 1) for i in range(DIFF_HEADS)], np.float64)


def _bf16_split3(v):
    parts = []
    rem = np.float64(v)
    for _ in range(3):
        p = np.asarray(rem, np.float32).astype(jnp.bfloat16)
        parts.append(p)
        rem = rem - np.float64(p.astype(np.float32))
    return parts


def _proj_kernel(x_ref, w_ref, o_ref, *, nc):
    xb = x_ref[...].astype(BF16)
    n = w_ref.shape[1]
    for c in range(0, n, nc):
        o_ref[:, c:c + nc] = jnp.dot(xb, w_ref[:, c:c + nc],
                                     preferred_element_type=F32).astype(o_ref.dtype)


def _proj(x2, w, tm, nc):
    t, d = x2.shape
    n = w.shape[1]
    return pl.pallas_call(
        functools.partial(_proj_kernel, nc=nc),
        grid=(t // tm,),
        in_specs=[pl.BlockSpec((tm, d), lambda i: (i, 0)), _resident((d, n))],
        out_specs=pl.BlockSpec((tm, n), lambda i: (i, 0)),
        out_shape=jax.ShapeDtypeStruct((t, n), BF16),
        compiler_params=_cparams(("arbitrary",)),
        name="memkv",
    )(x2, w)


def _inproj_kernel(x_ref, wm_ref, wk_ref, wvt_ref, kaug_ref, sel_ref, main_ref, kp_ref, vt_ref, nrm_ref, *, nc):
    xb = x_ref[...].astype(BF16)
    qscale = LOG2E * DIFF_QK_DIM ** -0.5
    for c in range(0, MAIN_W, nc):
        acc = jnp.dot(xb, wm_ref[:, c:c + nc], preferred_element_type=F32)
        if c < OFF_RQ:
            acc = acc * qscale
        main_ref[:, c:c + nc] = acc.astype(main_ref.dtype)

    k = jnp.dot(xb, wk_ref[...], preferred_element_type=F32).astype(kp_ref.dtype)
    for h in range(DIFF_HEADS):
        kp_ref[:, h * 2 * LANES:h * 2 * LANES + LANES] = k[:, h * LANES:(h + 1) * LANES]
        kp_ref[:, h * 2 * LANES + LANES:(h + 1) * 2 * LANES] = kaug_ref[:, h * LANES:(h + 1) * LANES]

    def max_sq_norm(v):
        vf = v.astype(F32)
        sq = jnp.dot((vf * vf).astype(BF16), sel_ref[...], preferred_element_type=F32)
        return jnp.broadcast_to(jnp.max(sq, axis=0, keepdims=True), (SUBLANES // 2, LANES))

    nrm_ref[0] = jnp.concatenate([max_sq_norm(main_ref[:, OFF_DQ:OFF_RQ]), max_sq_norm(k)], axis=0)

    vt = lax.dot_general(wvt_ref[...], xb, (((1,), (1,)), ((), ())), preferred_element_type=F32)
    row = lax.broadcasted_iota(jnp.int32, vt.shape, 0)
    ones_row = row == DIFF_V_DIM
    for h in range(1, DIFF_HEADS):
        ones_row = ones_row | (row == h * VT_ROWS + DIFF_V_DIM)
    vt_ref[...] = jnp.where(ones_row, 1.0, vt).astype(vt_ref.dtype)


def _inproj(x2, w_in):
    t, d = x2.shape
    tm = PROJ_TM
    cols = lambda a, b: w_in[:, a:b]
    w_main = jnp.concatenate([cols(IN_DQ, IN_DK), cols(IN_RQ, IN_END)], axis=1).astype(BF16)
    w_k = cols(IN_DK, IN_DV).astype(BF16)
    wv = cols(IN_DV, IN_RQ).astype(BF16).T.reshape(DIFF_HEADS, DIFF_V_DIM, d)
    w_vt = jnp.pad(wv, ((0, 0), (0, VT_ROWS - DIFF_V_DIM), (0, 0))).reshape(DIFF_HEADS * VT_ROWS, d)

    kaug = _alibi_key_columns(tm, ATT_TK)
    sel = (np.arange(DIFF_HEADS * 2 * DIFF_QK_DIM)[:, None] // DIFF_QK_DIM == np.arange(LANES)[None, :])
    sel = jnp.asarray(sel, BF16)

    row = lambda i: (i, 0)
    return pl.pallas_call(
        functools.partial(_inproj_kernel, nc=PROJ_NC),
        grid=(t // tm,),
        in_specs=[pl.BlockSpec((tm, d), row), _resident(w_main.shape), _resident(w_k.shape),
                  _resident(w_vt.shape), _resident(kaug.shape), _resident(sel.shape)],
        out_specs=[pl.BlockSpec((tm, MAIN_W), row),
                   pl.BlockSpec((tm, DIFF_HEADS * 2 * LANES), row),
                   pl.BlockSpec((DIFF_HEADS * VT_ROWS, tm), lambda i: (0, i)),
                   pl.BlockSpec((1, SUBLANES, LANES), lambda i: (i, 0, 0))],
        out_shape=[jax.ShapeDtypeStruct((t, MAIN_W), BF16),
                   jax.ShapeDtypeStruct((t, DIFF_HEADS * 2 * LANES), BF16),
                   jax.ShapeDtypeStruct((DIFF_HEADS * VT_ROWS, t), BF16),
                   jax.ShapeDtypeStruct((t // tm, SUBLANES, LANES), F32)],
        compiler_params=_cparams(("arbitrary",)),
        name="inproj",
    )(x2, w_main, w_k, w_vt, kaug, sel)


def _attn_kernel(slope_ref, first_ref, q_ref, qaug_ref, kp_ref, vt_ref, tdiag_ref, lam_ref, g_ref, o_ref,
                 qs_scr, m_scr, acc_scr, z_scr, zmax_scr, p_scr, alpha_scr, *, tq, tk, ns):
    b, h, qi = pl.program_id(0), pl.program_id(1), pl.program_id(2)
    per_tile = tk // tq
    n0 = (ns // per_tile) * qi
    first = first_ref[b, h, qi]
    slope = slope_ref[h]

    @pl.when((b == 0) & (h == 0) & (qi == 0))
    def _():
        p_scr[...] = jnp.zeros(p_scr.shape, BF16)

    qa = qaug_ref[0]
    for j in range(ns):
        q = q_ref[j * tq:(j + 1) * tq, :]
        lane = lax.broadcasted_iota(jnp.int32, q.shape, 1)
        zero = jnp.zeros_like(q)
        q1 = jnp.concatenate([jnp.where(lane < DIFF_QK_DIM, q, zero), qa], axis=1).astype(F32)
        q2 = jnp.concatenate([jnp.where(lane >= DIFF_QK_DIM, q, zero), qa], axis=1).astype(F32)
        qs_scr[j, :, 0:tq] = q1.T.astype(BF16)
        qs_scr[j, :, tq:2 * tq] = q2.T.astype(BF16)
        m_scr[j] = jnp.full(m_scr.shape[1:], -jnp.inf, F32)
        acc_scr[j] = jnp.zeros(acc_scr.shape[1:], F32)
        alpha_scr[j] = jnp.ones(alpha_scr.shape[1:], F32)

    def score_stage(j, kb, slot, rows=tk):
        start = pl.multiple_of(kb * tk, tk)
        z = jnp.dot(kp_ref[pl.ds(start, rows), :], qs_scr[j],
                    preferred_element_type=F32)
        zmax = None
        for r in range(0, rows, ATT_RB):
            zb = z[r:r + ATT_RB]
            z_scr[j, slot, r:r + ATT_RB] = zb
            bmax = jnp.max(zb, axis=0, keepdims=True)
            zmax = bmax if zmax is None else jnp.maximum(zmax, bmax)
        zmax_scr[j, slot] = zmax

    def softmax_stage(j, z, zmax, c, slot, rows=tk):
        m_prev = m_scr[j]
        m_new = jnp.maximum(m_prev, zmax + c)
        shift = m_new - c
        for r in range(0, rows, ATT_RB):
            p_scr[j, slot, r:r + ATT_RB] = jnp.exp2(z[r:r + ATT_RB] - shift).astype(BF16)
        alpha_scr[j, slot] = jnp.exp2(m_prev - m_new)
        m_scr[j] = m_new

    def value_stage(j, kb, slot, rows=tk):
        start = pl.multiple_of(jnp.maximum(kb, first) * tk, tk)
        v = vt_ref[:, pl.ds(start, rows)]
        v = jnp.where(kb >= first, v, jnp.zeros_like(v))
        acc_scr[j] = alpha_scr[j, slot] * acc_scr[j] + jnp.dot(v, p_scr[j, slot, 0:rows],
                                                               preferred_element_type=F32)

    def tile_bias(j, kb):
        return -slope * ((n0 - kb) * tk + j * tq).astype(F32)

    def full_trip(kb, cur, nxt):
        for j in range(ns):
            score_stage(j, kb + 1, nxt)
            softmax_stage(j, z_scr.at[j, cur], zmax_scr[j, cur], tile_bias(j, kb), cur)
            value_stage(j, kb - 1, nxt)

    lead = (n0 - first) % ATT_UNROLL
    base = first + lead

    @pl.when(lead == 0)
    def _():
        for j in range(ns):
            score_stage(j, first, 0)

    @pl.when(lead == 1)
    def _():
        for j in range(ns):
            score_stage(j, first, 1)
        full_trip(first, 1, 0)

    def unrolled(i, carry):
        for u in range(ATT_UNROLL):
            full_trip(base + ATT_UNROLL * i + u, u % 2, 1 - u % 2)
        return carry

    lax.fori_loop(0, (n0 - base) // ATT_UNROLL, unrolled, 0)

    for i in range(ns // per_tile + 1):
        kb = n0 + i
        cur, nxt = i % 2, 1 - i % 2
        for j in range(ns):
            last = j // per_tile
            seen = (j % per_tile + 1) * tq
            if i < last:
                score_stage(j, kb + 1, nxt, rows=seen if i + 1 == last else tk)
                softmax_stage(j, z_scr.at[j, cur], zmax_scr[j, cur], tile_bias(j, kb), cur)
            elif i == last:
                td = tdiag_ref[0, j % per_tile, 0:seen, :]
                z = z_scr[j, cur, 0:seen]
                z = jnp.concatenate([z[:, 0:tq] + td, z[:, tq:2 * tq] + td], axis=1)
                softmax_stage(j, z, jnp.max(z, axis=0, keepdims=True), tile_bias(j, kb), cur, rows=seen)
            if i <= last + 1:
                value_stage(j, kb - 1, nxt, rows=seen if i == last + 1 else tk)

    lp = lam_ref[...]
    lam = (jnp.exp(jnp.sum(lp[0:1] * lp[1:2], axis=-1, keepdims=True))
           - jnp.exp(jnp.sum(lp[2:3] * lp[3:4], axis=-1, keepdims=True)) + LAMBDA_INIT)
    for j in range(ns):
        acc = acc_scr[j]
        o = acc[0:DIFF_V_DIM, :] / acc[DIFF_V_DIM:DIFF_V_DIM + 1, :]
        y = o[:, 0:tq] - lam * o[:, tq:2 * tq]
        y = y * lax.rsqrt(jnp.mean(y * y, axis=0, keepdims=True) + LN_EPS)
        o_ref[j * tq:(j + 1) * tq, :] = (y.T * g_ref[...] * (1.0 - LAMBDA_INIT)).astype(o_ref.dtype)


def _alibi_key_columns(n_rows, tk):
    l3 = _bf16_split3(LOG2E)
    off = np.arange(n_rows) % tk
    kaug = np.zeros((n_rows, DIFF_HEADS, LANES), np.float32)
    for hh, m in enumerate(_alibi_slopes()):
        for i in range(3):
            kaug[:, hh, i] = np.float32(l3[i])
            kaug[:, hh, 3 + i] = m * (off % ATT_TQ)
            kaug[:, hh, 6 + i] = m * (off - off % ATT_TQ)
    return jnp.asarray(kaug.reshape(n_rows, DIFF_HEADS * LANES), BF16)


def _attn_tables(tq, tk):
    slopes = _alibi_slopes()
    l3 = _bf16_split3(LOG2E)
    il = np.arange(tq, dtype=np.float64)
    qaug = np.zeros((DIFF_HEADS, tq, LANES), np.float32)
    for hh, m in enumerate(slopes):
        for i in range(3):
            qaug[hh, :, i] = -m * il
            qaug[hh, :, 3 + i] = np.float32(l3[i])
            qaug[hh, :, 6 + i] = np.float32(l3[i])
    j = np.arange(tk)[None, :, None]
    i = np.arange(tq)[None, None, :] + tq * np.arange(tk // tq)[:, None, None]
    allowed = (j // CHUNK) <= (i // CHUNK)
    corr = LOG2E * slopes[:, None, None, None] * ((i - j) - np.abs(i - j))[None]
    tdiag = np.where(allowed[None], corr, -np.inf)
    return (jnp.asarray(slopes * LOG2E, F32), jnp.asarray(qaug, BF16), jnp.asarray(tdiag, F32))


def _first_key_tiles(nrm, batch, seq):
    tiles_per_seq = seq // PROJ_TM
    tsup = ATT_TQ * ATT_NS
    n = nrm.reshape(batch, tiles_per_seq, 2, SUBLANES // 2, LANES)[:, :, :, 0, :DIFF_HEADS * 2]
    n = n.reshape(batch, tiles_per_seq, 2, DIFF_HEADS, 2).max(axis=-1)
    q2 = n[:, :, 0].reshape(batch, seq // tsup, tsup // PROJ_TM, DIFF_HEADS).max(axis=2)
    k2 = n[:, :, 1].max(axis=1)
    bound = jnp.sqrt(q2 * k2[:, None, :]) * ATT_BOUND_SCALE + ATT_BOUND_SLACK
    slope = jnp.asarray(_alibi_slopes() * LOG2E, F32)
    reach = jnp.minimum((2.0 * bound + ATT_UNDERFLOW_LOG2) / slope, float(seq))
    needed = jnp.floor(jnp.maximum(reach - 1.0, 0.0) / ATT_TK).astype(jnp.int32) + 1
    n0 = (tsup // ATT_TK) * jnp.arange(seq // tsup, dtype=jnp.int32)[None, :, None]
    first = jnp.maximum(n0 - needed, 0)
    return first.transpose(0, 2, 1)


def _diff_attention(main, kp, vt, nrm, lam_p, subln_g, batch, seq):
    tq, tk, ns = ATT_TQ, ATT_TK, ATT_NS
    tsup = tq * ns
    nq = seq // tsup
    slopes, qaug, tdiag = _attn_tables(tq, tk)
    first = _first_key_tiles(nrm, batch, seq)
    return pl.pallas_call(
        functools.partial(_attn_kernel, tq=tq, tk=tk, ns=ns),
        grid=(batch, DIFF_HEADS, nq),
        in_specs=[
            pl.BlockSpec(memory_space=pltpu.SMEM),
            pl.BlockSpec(memory_space=pltpu.SMEM),
            pl.BlockSpec((tsup, LANES), lambda b, h, qi: (b * nq + qi, OFF_DQ // LANES + h)),
            pl.BlockSpec((1, tq, LANES), lambda b, h, qi: (h, 0, 0)),
            pl.BlockSpec((seq, 2 * LANES), lambda b, h, qi: (b, h)),
            pl.BlockSpec((VT_ROWS, seq), lambda b, h, qi: (h, b)),
            pl.BlockSpec((1,) + tdiag.shape[1:], lambda b, h, qi: (h, 0, 0, 0)),
            pl.BlockSpec(lam_p.shape, lambda b, h, qi: (0, 0)),
            pl.BlockSpec((1, LANES), lambda b, h, qi: (0, h)),
        ],
        out_specs=pl.BlockSpec((tsup, LANES), lambda b, h, qi: (b * nq + qi, h)),
        out_shape=jax.ShapeDtypeStruct((batch * seq, DIFF_HEADS * DIFF_V_DIM), BF16),
        scratch_shapes=[
            pltpu.VMEM((ns, 2 * LANES, 2 * tq), BF16),
            pltpu.VMEM((ns, 1, 2 * tq), F32),
            pltpu.VMEM((ns, VT_ROWS, 2 * tq), F32),
            pltpu.VMEM((ns, 2, tk, 2 * tq), F32),
            pltpu.VMEM((ns, 2, 1, 2 * tq), F32),
            pltpu.VMEM((ns, 2, tk, 2 * tq), BF16),
            pltpu.VMEM((ns, 2, 1, 2 * tq), F32),
        ],
        compiler_params=_cparams(("arbitrary", "arbitrary", "arbitrary")),
        name="diff_attn",
    )(slopes, first, main, qaug, kp, vt, tdiag, lam_p, subln_g)


def _retention_chunk(rows, q_ref, k_ref, v_ref, g_ref, dtab_ref, qdec_ref, kdec_ref, cdec_ref, gn_ref,
                     state_scr, r_scr):
    c = RET_C
    lane = lax.broadcasted_iota(jnp.int32, (c, LANES), 1)
    for p in range(RET_HEADS // 2):
        qp = q_ref[rows, p * LANES:(p + 1) * LANES]
        kp = k_ref[rows, p * LANES:(p + 1) * LANES]
        vp = v_ref[rows, p * 2 * RET_V_DIM:(p + 1) * 2 * RET_V_DIM]
        state = state_scr[p]
        state_b = state.astype(BF16)
        qdec = qdec_ref[:, p * LANES:(p + 1) * LANES]
        for e in range(2):
            head = 2 * p + e
            keep = (lane < RET_QK_DIM) if e == 0 else (lane >= RET_QK_DIM)
            qm = jnp.where(keep, qp, jnp.zeros_like(qp))
            s = lax.dot_general(qm, kp, (((1,), (1,)), ((), ())), preferred_element_type=F32)
            s = s * dtab_ref[head]
            ve = vp[:, e * RET_V_DIM:(e + 1) * RET_V_DIM]
            intra = jnp.dot(s.astype(BF16), ve, preferred_element_type=F32)
            qd = (qm.astype(F32) * qdec).astype(BF16)
            cross = jnp.dot(qd, state_b[:, e * RET_V_DIM:(e + 1) * RET_V_DIM],
                            preferred_element_type=F32)
            o = intra + cross
            mu = jnp.mean(o, axis=-1, keepdims=True)
            d = o - mu
            var = jnp.mean(d * d, axis=-1, keepdims=True)
            cols = slice(head * RET_V_DIM, (head + 1) * RET_V_DIM)
            gate = g_ref[rows, cols].astype(F32)
            y = d * lax.rsqrt(var + LN_EPS) * gn_ref[:, cols] * (gate * jax.nn.sigmoid(gate))
            r_scr[rows, cols] = y.astype(r_scr.dtype)
        kd = (kp.astype(F32) * kdec_ref[:, p * LANES:(p + 1) * LANES]).astype(BF16)
        kv = lax.dot_general(kd, vp, (((0,), (0,)), ((), ())), preferred_element_type=F32)
        state_scr[p] = state * cdec_ref[p] + kv


def _ret_tables(c):
    hh = np.arange(RET_HEADS, dtype=np.float64)
    lg = np.log(1.0 - 2.0 ** (-5.0 - hh))
    idx = np.arange(c, dtype=np.float64)
    nm = idx[:, None] - idx[None, :]
    dtab = np.where(nm >= 0, np.exp(lg[:, None, None] * np.maximum(nm, 0.0)), 0.0) * RET_QK_DIM ** -0.5
    qdec = np.exp(lg[None, :] * (idx + 1.0)[:, None])
    kdec = np.exp(lg[None, :] * (c - 1.0 - idx)[:, None]) * RET_QK_DIM ** -0.5
    qdec = np.repeat(qdec, RET_QK_DIM, axis=1)
    kdec = np.repeat(kdec, RET_QK_DIM, axis=1)
    cdec = np.repeat(np.exp(lg * c), RET_QK_DIM).reshape(RET_HEADS // 2, 2 * RET_QK_DIM, 1)
    return tuple(jnp.asarray(a, F32) for a in (dtab, qdec, kdec, cdec))


def _merge_kernel(x_ref, a_ref, rq_ref, rk_ref, rv_ref, rg_ref, mq_ref, mk_ref, mv_ref,
                  dtab_ref, qdec_ref, kdec_ref, cdec_ref, gn_ref,
                  wg_ref, bg_ref, wd_ref, wr_ref, wm_ref, wo_ref, g1_ref, b1_ref, o_ref,
                  state_scr, r_scr, m_scr, merged_scr, *, nc, tiles_per_seq, reset_here=True):
    tm, d = x_ref.shape
    x = x_ref[...]
    xb = x.astype(BF16)

    if reset_here:
        @pl.when(pl.program_id(0) % tiles_per_seq == 0)
        def _():
            state_scr[...] = jnp.zeros(state_scr.shape, F32)

    for r0 in range(0, tm, RET_C):
        _retention_chunk(slice(r0, r0 + RET_C), rq_ref, rk_ref, rv_ref, rg_ref, dtab_ref, qdec_ref, kdec_ref,
                         cdec_ref, gn_ref, state_scr, r_scr)

    scale = jnp.asarray(MEM_HEAD_DIM ** -0.5, BF16)
    for h in range(MEM_HEADS):
        cols = slice(h * MEM_HEAD_DIM, (h + 1) * MEM_HEAD_DIM)
        qh = mq_ref[:, cols] * scale
        s = lax.dot_general(qh, mk_ref[0, :, cols], (((1,), (1,)), ((), ())),
                            preferred_element_type=F32)
        p = jnp.exp(s - jnp.max(s, axis=-1, keepdims=True))
        l = jnp.sum(p, axis=-1, keepdims=True)
        mh = jnp.dot(p.astype(BF16), mv_ref[0, :, cols], preferred_element_type=F32) / l
        m_scr[:, cols] = mh.astype(BF16)

    a = a_ref[...]
    r = r_scr[...]
    m = m_scr[...]
    for c in range(0, d, nc):
        cs = slice(c, c + nc)
        acc = None
        for br, (val, w_ref) in enumerate(((a, wd_ref), (r, wr_ref), (m, wm_ref))):
            gs = slice(br * d + c, br * d + c + nc)
            gate = jax.nn.sigmoid(jnp.dot(xb, wg_ref[:, gs], preferred_element_type=F32) + bg_ref[:, gs])
            term = gate * jnp.dot(val, w_ref[:, cs], preferred_element_type=F32)
            acc = term if acc is None else acc + term
        merged_scr[:, cs] = acc.astype(BF16)

    for r0 in range(0, tm, EPI_RB):
        rows = slice(r0, r0 + EPI_RB)
        mix = jnp.dot(merged_scr[rows, :], wo_ref[...], preferred_element_type=F32)
        o_ref[rows, :] = _layer_norm(DEEPNORM_ALPHA * x_ref[rows, :] + mix, g1_ref[...], b1_ref[...])


def _merge(x2, a, main, memkv, ret_g, wg, bg, wd, wr, wm, wo, g1, b1, batch, seq):
    tm = MERGE_TM
    t, d = x2.shape
    nt = seq // tm
    mlen = memkv.shape[1]
    mw = MEM_HEADS * MEM_HEAD_DIM
    qkw = RET_HEADS * RET_QK_DIM
    vw = RET_HEADS * RET_V_DIM
    dtab, qdec, kdec, cdec = _ret_tables(RET_C)
    row = lambda i: (i, 0)
    return pl.pallas_call(
        functools.partial(_merge_kernel, nc=MERGE_NC, tiles_per_seq=nt),
        grid=(t // tm,),
        in_specs=[
            pl.BlockSpec((tm, d), row),
            pl.BlockSpec((tm, a.shape[1]), row),
            pl.BlockSpec((tm, qkw), lambda i: (i, OFF_RQ // qkw)),
            pl.BlockSpec((tm, qkw), lambda i: (i, OFF_RK // qkw)),
            pl.BlockSpec((tm, vw), lambda i: (i, OFF_RV // vw)),
            pl.BlockSpec((tm, vw), lambda i: (i, OFF_RG // vw)),
            pl.BlockSpec((tm, mw), lambda i: (i, OFF_MQ // mw)),
            pl.BlockSpec((1, mlen, mw), lambda i: (i // nt, 0, 0)),
            pl.BlockSpec((1, mlen, mw), lambda i: (i // nt, 0, 1)),
            _resident(dtab.shape), _resident(qdec.shape), _resident(kdec.shape), _resident(cdec.shape),
            _resident(ret_g.shape),
            _resident(wg.shape), _resident(bg.shape), _resident(wd.shape), _resident(wr.shape),
            _resident(wm.shape), _resident(wo.shape), _resident(g1.shape), _resident(b1.shape),
        ],
        out_specs=pl.BlockSpec((tm, d), row),
        out_shape=jax.ShapeDtypeStruct((t, d), F32),
        scratch_shapes=[
            pltpu.VMEM((RET_HEADS // 2, 2 * RET_QK_DIM, 2 * RET_V_DIM), F32),
            pltpu.VMEM((tm, vw), BF16),
            pltpu.VMEM((tm, mw), BF16),
            pltpu.VMEM((tm, d), BF16),
        ],
        compiler_params=_cparams(("arbitrary",)),
        name="merge",
    )(x2, a, main, main, main, main, main, memkv, memkv, dtab, qdec, kdec, cdec, ret_g,
      wg, bg, wd, wr, wm, wo, g1, b1)


def _ffn_kernel(h_ref, wu_ref, cw_ref, cb_ref, wdn_ref, g2_ref, b2_ref, o_ref,
                carry_scr, hs_scr, act_scr, *, fc, rb, ffn_dim, tiles_per_seq, reset_here=True):
    tm = h_ref.shape[0]
    halo = SUBLANES

    if reset_here:
        @pl.when(pl.program_id(0) % tiles_per_seq == 0)
        def _():
            carry_scr[...] = jnp.zeros(carry_scr.shape, F32)

    x = h_ref[...]
    xb = x.astype(BF16)

    def up_project(slot, col):
        cs = slice(col, col + fc)
        up = jnp.dot(xb, wu_ref[:, cs], preferred_element_type=F32)
        hs_scr[slot, 0:halo, :] = carry_scr[:, cs]
        hs_scr[slot, halo:halo + tm, :] = up
        carry_scr[:, cs] = up[tm - halo:tm, :]

    def conv_rows(slot, col, r0):
        cs = slice(col, col + fc)
        blk = hs_scr[slot, r0:r0 + rb + halo, :]
        w = cw_ref[:, cs]
        y = pltpu.roll(blk, 2, 0) * w[0:1] + pltpu.roll(blk, 1, 0) * w[1:2] + blk * w[2:3] + cb_ref[:, cs]
        return y[halo:, :]

    for j, c in enumerate(range(0, ffn_dim, fc)):
        sg, sv = 2 * (j % 2), 2 * (j % 2) + 1
        up_project(sg, c)
        up_project(sv, ffn_dim + c)
        for r0 in range(0, tm, rb):
            g = conv_rows(sg, c, r0)
            val = conv_rows(sv, ffn_dim + c, r0)
            act_scr[r0:r0 + rb, c:c + fc] = (g * jax.nn.sigmoid(g) * val).astype(BF16)

    for r0 in range(0, tm, EPI_RB):
        rows = slice(r0, r0 + EPI_RB)
        ffn = jnp.dot(act_scr[rows, :], wdn_ref[...], preferred_element_type=F32)
        o_ref[rows, :] = _layer_norm(DEEPNORM_ALPHA * h_ref[rows, :] + ffn, g2_ref[...], b2_ref[...])


def _ffn(h1, wu, cw, cb, wdn, g2, b2, seq):
    tm = FFN_TM
    t, d = h1.shape
    ffn_dim = wdn.shape[0]
    row = lambda i: (i, 0)
    return pl.pallas_call(
        functools.partial(_ffn_kernel, fc=FFN_FC, rb=FFN_RB, ffn_dim=ffn_dim, tiles_per_seq=seq // tm),
        grid=(t // tm,),
        in_specs=[
            pl.BlockSpec((tm, d), row),
            _resident(wu.shape), _resident(cw.shape), _resident(cb.shape), _resident(wdn.shape),
            _resident(g2.shape), _resident(b2.shape),
        ],
        out_specs=pl.BlockSpec((tm, d), row),
        out_shape=jax.ShapeDtypeStruct((t, d), F32),
        scratch_shapes=[
            pltpu.VMEM((SUBLANES, 2 * ffn_dim), F32),
            pltpu.VMEM((4, tm + SUBLANES, FFN_FC), F32),
            pltpu.VMEM((tm, ffn_dim), BF16),
        ],
        compiler_params=_cparams(("arbitrary",)),
        name="conv_ffn",
    )(h1, wu, cw, cb, wdn, g2, b2)


N_MERGE_IN = 22
N_FFN_W = 6


def _merge_ffn_kernel(*refs, nc, fc, rb, ffn_dim, tiles_per_seq):
    merge_in = refs[:N_MERGE_IN]
    ffn_w = refs[N_MERGE_IN:N_MERGE_IN + N_FFN_W]
    o_ref = refs[N_MERGE_IN + N_FFN_W]
    state_scr, r_scr, m_scr, merged_scr, h1_scr, carry_scr, hs_scr, act_scr = refs[N_MERGE_IN + N_FFN_W + 1:]

    @pl.when(pl.program_id(0) % tiles_per_seq == 0)
    def _():
        state_scr[...] = jnp.zeros(state_scr.shape, F32)
        carry_scr[...] = jnp.zeros(carry_scr.shape, F32)

    _merge_kernel(*merge_in, h1_scr, state_scr, r_scr, m_scr, merged_scr,
                  nc=nc, tiles_per_seq=tiles_per_seq, reset_here=False)
    _ffn_kernel(h1_scr, *ffn_w, o_ref, carry_scr, hs_scr, act_scr,
                fc=fc, rb=rb, ffn_dim=ffn_dim, tiles_per_seq=tiles_per_seq, reset_here=False)


def _merge_ffn(x2, a, main, memkv, ret_g, wg, bg, wd, wr, wm, wo, g1, b1, wu, cw, cb, wdn, g2, b2, batch, seq):
    tm = MERGE_TM
    t, d = x2.shape
    nt = seq // tm
    mlen = memkv.shape[1]
    mw = MEM_HEADS * MEM_HEAD_DIM
    qkw = RET_HEADS * RET_QK_DIM
    vw = RET_HEADS * RET_V_DIM
    ffn_dim = wdn.shape[0]
    dtab, qdec, kdec, cdec = _ret_tables(RET_C)
    row = lambda i: (i, 0)
    resident = (dtab, qdec, kdec, cdec, ret_g, wg, bg, wd, wr, wm, wo, g1, b1, wu, cw, cb, wdn, g2, b2)
    return pl.pallas_call(
        functools.partial(_merge_ffn_kernel, nc=MERGE_NC, fc=FFN_FC, rb=FFN_RB, ffn_dim=ffn_dim,
                          tiles_per_seq=nt),
        grid=(t // tm,),
        in_specs=[
            pl.BlockSpec((tm, d), row),
            pl.BlockSpec((tm, a.shape[1]), row),
            pl.BlockSpec((tm, qkw), lambda i: (i, OFF_RQ // qkw)),
            pl.BlockSpec((tm, qkw), lambda i: (i, OFF_RK // qkw)),
            pl.BlockSpec((tm, vw), lambda i: (i, OFF_RV // vw)),
            pl.BlockSpec((tm, vw), lambda i: (i, OFF_RG // vw)),
            pl.BlockSpec((tm, mw), lambda i: (i, OFF_MQ // mw)),
            pl.BlockSpec((1, mlen, mw), lambda i: (i // nt, 0, 0)),
            pl.BlockSpec((1, mlen, mw), lambda i: (i // nt, 0, 1)),
        ] + [_resident(w.shape) for w in resident],
        out_specs=pl.BlockSpec((tm, d), row),
        out_shape=jax.ShapeDtypeStruct((t, d), F32),
        scratch_shapes=[
            pltpu.VMEM((RET_HEADS // 2, 2 * RET_QK_DIM, 2 * RET_V_DIM), F32),
            pltpu.VMEM((tm, vw), BF16),
            pltpu.VMEM((tm, mw), BF16),
            pltpu.VMEM((tm, d), BF16),
            pltpu.VMEM((tm, d), F32),
            pltpu.VMEM((SUBLANES, 2 * ffn_dim), F32),
            pltpu.VMEM((4, tm + SUBLANES, FFN_FC), F32),
            pltpu.VMEM((tm, ffn_dim), BF16),
        ],
        compiler_params=_cparams(("arbitrary",)),
        name="merge_ffn",
    )(x2, a, main, main, main, main, main, memkv, memkv, *resident)


def kernel(x, mem, w_in, diff_lambda, diff_subln_g, ret_norm_g, w_mem_kv, w_diff_o, w_ret_o, w_mem_o,
           w_gate, b_gate, w_mix_out, ln1_g, ln1_b, w_up, conv_w, conv_b, w_down, ln2_g, ln2_b):
    batch, seq, d = x.shape
    mlen = mem.shape[1]
    assert w_in.shape[0] == DEPTH == 1 and conv_w.shape[1] == CONV_WIDTH == 3
    assert seq % PROJ_TM == 0 and MERGE_TM % RET_C == 0 and seq % FFN_TM == 0 and seq % MERGE_TM == 0
    assert PROJ_TM % ATT_TK == 0 and ATT_TQ % CHUNK == 0 and ATT_TK % ATT_TQ == 0 and seq % (ATT_TQ * ATT_NS) == 0
    assert MERGE_TM == FFN_TM
    assert ATT_UNROLL == 2 and (ATT_NS * ATT_TQ // ATT_TK) % ATT_UNROLL == 0
    l = 0
    row = lambda v: v[l].reshape(1, -1)

    x2 = x.reshape(batch * seq, d)
    main, kp, vt, nrm = _inproj(x2, w_in[l])
    memkv = _proj(mem.reshape(batch * mlen, d), w_mem_kv[l].astype(BF16), batch * mlen, PROJ_NC)
    memkv = memkv.reshape(batch, mlen, 2 * MEM_HEADS * MEM_HEAD_DIM)

    a = _diff_attention(main, kp, vt, nrm, diff_lambda[l], row(diff_subln_g), batch, seq)
    out = _merge_ffn(x2, a, main, memkv, row(ret_norm_g), w_gate[l].astype(BF16), row(b_gate),
                     w_diff_o[l].astype(BF16), w_ret_o[l].astype(BF16), w_mem_o[l].astype(BF16),
                     w_mix_out[l].astype(BF16), row(ln1_g), row(ln1_b),
                     w_up[l].astype(BF16), conv_w[l], row(conv_b), w_down[l].astype(BF16),
                     row(ln2_g), row(ln2_b), batch, seq)
    return out.reshape(batch, seq, d)
```

```python
import functools
import math

import numpy as np
import jax
import jax.numpy as jnp
from jax import lax
from jax.experimental import pallas as pl
from jax.experimental.pallas import tpu as pltpu

F32 = jnp.float32
BF16 = jnp.bfloat16

CHUNK = 64
DIFF_HEADS = 4
DIFF_QK_DIM = 64
DIFF_V_DIM = 128
RET_HEADS = 4
RET_QK_DIM = 64
RET_V_DIM = 128
MEM_HEADS = 4
MEM_HEAD_DIM = 128
CONV_WIDTH = 3
LN_EPS = 1e-5
DEPTH = 1
DEEPNORM_ALPHA = (2.0 * DEPTH) ** 0.25
LAMBDA_INIT = 0.8 - 0.6 * math.exp(-0.3 * 0)
LOG2E = math.log2(math.e)

IN_DQ, IN_DK, IN_DV = 0, 512, 1024
IN_RQ, IN_END = 1536, 3584
OFF_DQ, OFF_RQ, OFF_RK, OFF_RV, OFF_RG, OFF_MQ, MAIN_W = 0, 512, 768, 1024, 1536, 2048, 2560

LANES = 128
SUBLANES = 8
BF16_ROWS = 16
VMEM_LIMIT = 56 * 1024 * 1024

PROJ_TM = 512
PROJ_NC = 512
ATT_TQ = 256
ATT_TK = 512
ATT_NS = 4
ATT_UNROLL = 2
ATT_RB = 256
ATT_UNDERFLOW_LOG2 = 134.0
ATT_BOUND_SCALE = 1.02
ATT_BOUND_SLACK = 0.05
VT_ROWS = DIFF_V_DIM
RET_C = 256
MERGE_TM = 512
MERGE_NC = 512
FFN_TM = 512
FFN_FC = 256
FFN_RB = 128
EPI_RB = 256


def _cparams(sem):
    return pltpu.CompilerParams(dimension_semantics=sem, vmem_limit_bytes=VMEM_LIMIT)


def _resident(shape):
    nd = len(shape)
    return pl.BlockSpec(shape, lambda *_: (0,) * nd, pipeline_mode=pl.Buffered(1))


def _layer_norm(y, g, b):
    mu = jnp.mean(y, axis=-1, keepdims=True)
    d = y - mu
    var = jnp.mean(d * d, axis=-1, keepdims=True)
    return d * lax.rsqrt(var + LN_EPS) * g + b


def _alibi_slopes():
    return np.array([(2.0 ** (-8.0 / DIFF_HEADS)) ** (i + 1) for i in range(DIFF_HEADS)], np.float64)


def _bf16_split3(v):
    parts = []
    rem = np.float64(v)
    for _ in range(3):
        p = np.asarray(rem, np.float32).astype(jnp.bfloat16)
        parts.append(p)
        rem = rem - np.float64(p.astype(np.float32))
    return parts


def _proj_kernel(x_ref, w_ref, o_ref, *, nc):
    xb = x_ref[...].astype(BF16)
    n = w_ref.shape[1]
    for c in range(0, n, nc):
        o_ref[:, c:c + nc] = jnp.dot(xb, w_ref[:, c:c + nc],
                                     preferred_element_type=F32).astype(o_ref.dtype)


def _proj(x2, w, tm, nc):
    t, d = x2.shape
    n = w.shape[1]
    return pl.pallas_call(
        functools.partial(_proj_kernel, nc=nc),
        grid=(t // tm,),
        in_specs=[pl.BlockSpec((tm, d), lambda i: (i, 0)), _resident((d, n))],
        out_specs=pl.BlockSpec((tm, n), lambda i: (i, 0)),
        out_shape=jax.ShapeDtypeStruct((t, n), BF16),
        compiler_params=_cparams(("arbitrary",)),
        name="memkv",
    )(x2, w)


def _inproj_kernel(x_ref, wm_ref, wk_ref, wvt_ref, kaug_ref, sel_ref, main_ref, kp_ref, vt_ref, nrm_ref, *, nc):
    xb = x_ref[...].astype(BF16)
    qscale = LOG2E * DIFF_QK_DIM ** -0.5
    for c in range(0, MAIN_W, nc):
        acc = jnp.dot(xb, wm_ref[:, c:c + nc], preferred_element_type=F32)
        if c < OFF_RQ:
            acc = acc * qscale
        main_ref[:, c:c + nc] = acc.astype(main_ref.dtype)

    k = jnp.dot(xb, wk_ref[...], preferred_element_type=F32).astype(kp_ref.dtype)
    for h in range(DIFF_HEADS):
        kp_ref[:, h * 2 * LANES:h * 2 * LANES + LANES] = k[:, h * LANES:(h + 1) * LANES]
        kp_ref[:, h * 2 * LANES + LANES:(h + 1) * 2 * LANES] = kaug_ref[:, h * LANES:(h + 1) * LANES]

    def max_sq_norm(v):
        vf = v.astype(F32)
        sq = jnp.dot((vf * vf).astype(BF16), sel_ref[...], preferred_element_type=F32)
        return jnp.broadcast_to(jnp.max(sq, axis=0, keepdims=True), (SUBLANES // 2, LANES))

    nrm_ref[0] = jnp.concatenate([max_sq_norm(main_ref[:, OFF_DQ:OFF_RQ]), max_sq_norm(k)], axis=0)

    vt = lax.dot_general(wvt_ref[...], xb, (((1,), (1,)), ((), ())), preferred_element_type=F32)
    vt_ref[...] = vt.astype(vt_ref.dtype)


def _inproj(x2, w_in):
    t, d = x2.shape
    tm = PROJ_TM
    cols = lambda a, b: w_in[:, a:b]
    w_main = jnp.concatenate([cols(IN_DQ, IN_DK), cols(IN_RQ, IN_END)], axis=1).astype(BF16)
    w_k = cols(IN_DK, IN_DV).astype(BF16)
    wv = cols(IN_DV, IN_RQ).astype(BF16).T.reshape(DIFF_HEADS, DIFF_V_DIM, d)
    w_vt = jnp.pad(wv, ((0, 0), (0, VT_ROWS - DIFF_V_DIM), (0, 0))).reshape(DIFF_HEADS * VT_ROWS, d)

    kaug = _alibi_key_columns(tm, ATT_TK)
    sel = (np.arange(DIFF_HEADS * 2 * DIFF_QK_DIM)[:, None] // DIFF_QK_DIM == np.arange(LANES)[None, :])
    sel = jnp.asarray(sel, BF16)

    row = lambda i: (i, 0)
    return pl.pallas_call(
        functools.partial(_inproj_kernel, nc=PROJ_NC),
        grid=(t // tm,),
        in_specs=[pl.BlockSpec((tm, d), row), _resident(w_main.shape), _resident(w_k.shape),
                  _resident(w_vt.shape), _resident(kaug.shape), _resident(sel.shape)],
        out_specs=[pl.BlockSpec((tm, MAIN_W), row),
                   pl.BlockSpec((tm, DIFF_HEADS * 2 * LANES), row),
                   pl.BlockSpec((DIFF_HEADS * VT_ROWS, tm), lambda i: (0, i)),
                   pl.BlockSpec((1, SUBLANES, LANES), lambda i: (i, 0, 0))],
        out_shape=[jax.ShapeDtypeStruct((t, MAIN_W), BF16),
                   jax.ShapeDtypeStruct((t, DIFF_HEADS * 2 * LANES), BF16),
                   jax.ShapeDtypeStruct((DIFF_HEADS * VT_ROWS, t), BF16),
                   jax.ShapeDtypeStruct((t // tm, SUBLANES, LANES), F32)],
        compiler_params=_cparams(("arbitrary",)),
        name="inproj",
    )(x2, w_main, w_k, w_vt, kaug, sel)


def _attn_kernel(slope_ref, first_ref, q_ref, qaug_ref, kp_ref, vt_ref, tdiag_ref, lam_ref, g_ref, o_ref,
                 qs_scr, m_scr, l_scr, acc_scr, z_scr, zmax_scr, p_scr, psum_scr, alpha_scr, *, tq, tk, ns):
    b, h, qi = pl.program_id(0), pl.program_id(1), pl.program_id(2)
    per_tile = tk // tq
    n0 = (ns // per_tile) * qi
    first = first_ref[b, h, qi]
    slope = slope_ref[h]

    @pl.when((b == 0) & (h == 0) & (qi == 0))
    def _():
        p_scr[...] = jnp.zeros(p_scr.shape, BF16)
        psum_scr[...] = jnp.zeros(psum_scr.shape, F32)

    qa = qaug_ref[0]
    for j in range(ns):
        q = q_ref[j * tq:(j + 1) * tq, :]
        lane = lax.broadcasted_iota(jnp.int32, q.shape, 1)
        zero = jnp.zeros_like(q)
        q1 = jnp.concatenate([jnp.where(lane < DIFF_QK_DIM, q, zero), qa], axis=1).astype(F32)
        q2 = jnp.concatenate([jnp.where(lane >= DIFF_QK_DIM, q, zero), qa], axis=1).astype(F32)
        qs_scr[j, :, 0:tq] = q1.T.astype(BF16)
        qs_scr[j, :, tq:2 * tq] = q2.T.astype(BF16)
        m_scr[j] = jnp.full(m_scr.shape[1:], -jnp.inf, F32)
        acc_scr[j] = jnp.zeros(acc_scr.shape[1:], F32)
        l_scr[j] = jnp.zeros(l_scr.shape[1:], F32)
        alpha_scr[j] = jnp.ones(alpha_scr.shape[1:], F32)

    def score_stage(j, kb, slot, rows=tk):
        start = pl.multiple_of(kb * tk, tk)
        z = jnp.dot(kp_ref[pl.ds(start, rows), :], qs_scr[j],
                    preferred_element_type=F32)
        zmax = None
        for r in range(0, rows, ATT_RB):
            zb = z[r:r + ATT_RB]
            z_scr[j, slot, r:r + ATT_RB] = zb
            bmax = jnp.max(zb, axis=0, keepdims=True)
            zmax = bmax if zmax is None else jnp.maximum(zmax, bmax)
        zmax_scr[j, slot] = zmax

    def softmax_stage(j, z, zmax, c, slot, rows=tk):
        m_prev = m_scr[j]
        m_new = jnp.maximum(m_prev, zmax + c)
        shift = m_new - c
        psum = None
        for r in range(0, rows, ATT_RB):
            pb = jnp.exp2(z[r:r + ATT_RB] - shift)
            p_scr[j, slot, r:r + ATT_RB] = pb.astype(BF16)
            bsum = jnp.sum(pb, axis=0, keepdims=True)
            psum = bsum if psum is None else psum + bsum
        psum_scr[j, slot] = psum
        alpha_scr[j, slot] = jnp.exp2(m_prev - m_new)
        m_scr[j] = m_new

    def value_stage(j, kb, slot, rows=tk):
        start = pl.multiple_of(jnp.maximum(kb, first) * tk, tk)
        v = vt_ref[:, pl.ds(start, rows)]
        v = jnp.where(kb >= first, v, jnp.zeros_like(v))
        alpha = alpha_scr[j, slot]
        acc_scr[j] = alpha * acc_scr[j] + jnp.dot(v, p_scr[j, slot, 0:rows], preferred_element_type=F32)
        l_scr[j] = alpha * l_scr[j] + jnp.where(kb >= first, psum_scr[j, slot], 0.0)

    def tile_bias(j, kb):
        return -slope * ((n0 - kb) * tk + j * tq).astype(F32)

    def full_trip(kb, cur, nxt):
        for j in range(ns):
            score_stage(j, kb + 1, nxt)
            softmax_stage(j, z_scr.at[j, cur], zmax_scr[j, cur], tile_bias(j, kb), cur)
            value_stage(j, kb - 1, nxt)

    lead = (n0 - first) % ATT_UNROLL
    base = first + lead

    @pl.when(lead == 0)
    def _():
        for j in range(ns):
            score_stage(j, first, 0)

    @pl.when(lead == 1)
    def _():
        for j in range(ns):
            score_stage(j, first, 1)
        full_trip(first, 1, 0)

    def unrolled(i, carry):
        for u in range(ATT_UNROLL):
            full_trip(base + ATT_UNROLL * i + u, u % 2, 1 - u % 2)
        return carry

    lax.fori_loop(0, (n0 - base) // ATT_UNROLL, unrolled, 0)

    for i in range(ns // per_tile + 1):
        kb = n0 + i
        cur, nxt = i % 2, 1 - i % 2
        for j in range(ns):
            last = j // per_tile
            seen = (j % per_tile + 1) * tq
            if i < last:
                score_stage(j, kb + 1, nxt, rows=seen if i + 1 == last else tk)
                softmax_stage(j, z_scr.at[j, cur], zmax_scr[j, cur], tile_bias(j, kb), cur)
            elif i == last:
                td = tdiag_ref[0, j % per_tile, 0:seen, :]
                z = z_scr[j, cur, 0:seen]
                z = jnp.concatenate([z[:, 0:tq] + td, z[:, tq:2 * tq] + td], axis=1)
                softmax_stage(j, z, jnp.max(z, axis=0, keepdims=True), tile_bias(j, kb), cur, rows=seen)
            if i <= last + 1:
                value_stage(j, kb - 1, nxt, rows=seen if i == last + 1 else tk)

    lp = lam_ref[...]
    lam = (jnp.exp(jnp.sum(lp[0:1] * lp[1:2], axis=-1, keepdims=True))
           - jnp.exp(jnp.sum(lp[2:3] * lp[3:4], axis=-1, keepdims=True)) + LAMBDA_INIT)
    for j in range(ns):
        acc = acc_scr[j]
        o = acc / l_scr[j]
        y = o[:, 0:tq] - lam * o[:, tq:2 * tq]
        y = y * lax.rsqrt(jnp.mean(y * y, axis=0, keepdims=True) + LN_EPS)
        o_ref[j * tq:(j + 1) * tq, :] = (y.T * g_ref[...] * (1.0 - LAMBDA_INIT)).astype(o_ref.dtype)


def _alibi_key_columns(n_rows, tk):
    l3 = _bf16_split3(LOG2E)
    off = np.arange(n_rows) % tk
    kaug = np.zeros((n_rows, DIFF_HEADS, LANES), np.float32)
    for hh, m in enumerate(_alibi_slopes()):
        for i in range(3):
            kaug[:, hh, i] = np.float32(l3[i])
            kaug[:, hh, 3 + i] = m * (off % ATT_TQ)
            kaug[:, hh, 6 + i] = m * (off - off % ATT_TQ)
    return jnp.asarray(kaug.reshape(n_rows, DIFF_HEADS * LANES), BF16)


def _attn_tables(tq, tk):
    slopes = _alibi_slopes()
    l3 = _bf16_split3(LOG2E)
    il = np.arange(tq, dtype=np.float64)
    qaug = np.zeros((DIFF_HEADS, tq, LANES), np.float32)
    for hh, m in enumerate(slopes):
        for i in range(3):
            qaug[hh, :, i] = -m * il
            qaug[hh, :, 3 + i] = np.float32(l3[i])
            qaug[hh, :, 6 + i] = np.float32(l3[i])
    j = np.arange(tk)[None, :, None]
    i = np.arange(tq)[None, None, :] + tq * np.arange(tk // tq)[:, None, None]
    allowed = (j // CHUNK) <= (i // CHUNK)
    corr = LOG2E * slopes[:, None, None, None] * ((i - j) - np.abs(i - j))[None]
    tdiag = np.where(allowed[None], corr, -np.inf)
    return (jnp.asarray(slopes * LOG2E, F32), jnp.asarray(qaug, BF16), jnp.asarray(tdiag, F32))


def _first_key_tiles(nrm, batch, seq):
    tiles_per_seq = seq // PROJ_TM
    tsup = ATT_TQ * ATT_NS
    n = nrm.reshape(batch, tiles_per_seq, 2, SUBLANES // 2, LANES)[:, :, :, 0, :DIFF_HEADS * 2]
    n = n.reshape(batch, tiles_per_seq, 2, DIFF_HEADS, 2).max(axis=-1)
    q2 = n[:, :, 0].reshape(batch, seq // tsup, tsup // PROJ_TM, DIFF_HEADS).max(axis=2)
    k2 = n[:, :, 1].max(axis=1)
    bound = jnp.sqrt(q2 * k2[:, None, :]) * ATT_BOUND_SCALE + ATT_BOUND_SLACK
    slope = jnp.asarray(_alibi_slopes() * LOG2E, F32)
    reach = jnp.minimum((2.0 * bound + ATT_UNDERFLOW_LOG2) / slope, float(seq))
    needed = jnp.floor(jnp.maximum(reach - 1.0, 0.0) / ATT_TK).astype(jnp.int32) + 1
    n0 = (tsup // ATT_TK) * jnp.arange(seq // tsup, dtype=jnp.int32)[None, :, None]
    first = jnp.maximum(n0 - needed, 0)
    return first.transpose(0, 2, 1)


def _diff_attention(main, kp, vt, nrm, lam_p, subln_g, batch, seq):
    tq, tk, ns = ATT_TQ, ATT_TK, ATT_NS
    tsup = tq * ns
    nq = seq // tsup
    slopes, qaug, tdiag = _attn_tables(tq, tk)
    first = _first_key_tiles(nrm, batch, seq)
    return pl.pallas_call(
        functools.partial(_attn_kernel, tq=tq, tk=tk, ns=ns),
        grid=(batch, DIFF_HEADS, nq),
        in_specs=[
            pl.BlockSpec(memory_space=pltpu.SMEM),
            pl.BlockSpec(memory_space=pltpu.SMEM),
            pl.BlockSpec((tsup, LANES), lambda b, h, qi: (b * nq + qi, OFF_DQ // LANES + h)),
            pl.BlockSpec((1, tq, LANES), lambda b, h, qi: (h, 0, 0)),
            pl.BlockSpec((seq, 2 * LANES), lambda b, h, qi: (b, h)),
            pl.BlockSpec((VT_ROWS, seq), lambda b, h, qi: (h, b)),
            pl.BlockSpec((1,) + tdiag.shape[1:], lambda b, h, qi: (h, 0, 0, 0)),
            pl.BlockSpec(lam_p.shape, lambda b, h, qi: (0, 0)),
            pl.BlockSpec((1, LANES), lambda b, h, qi: (0, h)),
        ],
        out_specs=pl.BlockSpec((tsup, LANES), lambda b, h, qi: (b * nq + qi, h)),
        out_shape=jax.ShapeDtypeStruct((batch * seq, DIFF_HEADS * DIFF_V_DIM), BF16),
        scratch_shapes=[
            pltpu.VMEM((ns, 2 * LANES, 2 * tq), BF16),
            pltpu.VMEM((ns, 1, 2 * tq), F32),
            pltpu.VMEM((ns, 1, 2 * tq), F32),
            pltpu.VMEM((ns, VT_ROWS, 2 * tq), F32),
            pltpu.VMEM((ns, 2, tk, 2 * tq), F32),
            pltpu.VMEM((ns, 2, 1, 2 * tq), F32),
            pltpu.VMEM((ns, 2, tk, 2 * tq), BF16),
            pltpu.VMEM((ns, 2, 1, 2 * tq), F32),
            pltpu.VMEM((ns, 2, 1, 2 * tq), F32),
        ],
        compiler_params=_cparams(("arbitrary", "arbitrary", "arbitrary")),
        name="diff_attn",
    )(slopes, first, main, qaug, kp, vt, tdiag, lam_p, subln_g)


def _retention_chunk(rows, q_ref, k_ref, v_ref, g_ref, dtab_ref, qdec_ref, kdec_ref, cdec_ref, gn_ref,
                     state_scr, r_scr):
    c = RET_C
    lane = lax.broadcasted_iota(jnp.int32, (c, LANES), 1)
    for p in range(RET_HEADS // 2):
        qp = q_ref[rows, p * LANES:(p + 1) * LANES]
        kp = k_ref[rows, p * LANES:(p + 1) * LANES]
        vp = v_ref[rows, p * 2 * RET_V_DIM:(p + 1) * 2 * RET_V_DIM]
        state = state_scr[p]
        state_b = state.astype(BF16)
        qdec = qdec_ref[:, p * LANES:(p + 1) * LANES]
        for e in range(2):
            head = 2 * p + e
            keep = (lane < RET_QK_DIM) if e == 0 else (lane >= RET_QK_DIM)
            qm = jnp.where(keep, qp, jnp.zeros_like(qp))
            s = lax.dot_general(qm, kp, (((1,), (1,)), ((), ())), preferred_element_type=F32)
            s = s * dtab_ref[head]
            ve = vp[:, e * RET_V_DIM:(e + 1) * RET_V_DIM]
            intra = jnp.dot(s.astype(BF16), ve, preferred_element_type=F32)
            qd = (qm.astype(F32) * qdec).astype(BF16)
            cross = jnp.dot(qd, state_b[:, e * RET_V_DIM:(e + 1) * RET_V_DIM],
                            preferred_element_type=F32)
            o = intra + cross
            mu = jnp.mean(o, axis=-1, keepdims=True)
            d = o - mu
            var = jnp.mean(d * d, axis=-1, keepdims=True)
            cols = slice(head * RET_V_DIM, (head + 1) * RET_V_DIM)
            gate = g_ref[rows, cols].astype(F32)
            y = d * lax.rsqrt(var + LN_EPS) * gn_ref[:, cols] * (gate * jax.nn.sigmoid(gate))
            r_scr[rows, cols] = y.astype(r_scr.dtype)
        kd = (kp.astype(F32) * kdec_ref[:, p * LANES:(p + 1) * LANES]).astype(BF16)
        kv = lax.dot_general(kd, vp, (((0,), (0,)), ((), ())), preferred_element_type=F32)
        state_scr[p] = state * cdec_ref[p] + kv


def _ret_tables(c):
    hh = np.arange(RET_HEADS, dtype=np.float64)
    lg = np.log(1.0 - 2.0 ** (-5.0 - hh))
    idx = np.arange(c, dtype=np.float64)
    nm = idx[:, None] - idx[None, :]
    dtab = np.where(nm >= 0, np.exp(lg[:, None, None] * np.maximum(nm, 0.0)), 0.0) * RET_QK_DIM ** -0.5
    qdec = np.exp(lg[None, :] * (idx + 1.0)[:, None])
    kdec = np.exp(lg[None, :] * (c - 1.0 - idx)[:, None]) * RET_QK_DIM ** -0.5
    qdec = np.repeat(qdec, RET_QK_DIM, axis=1)
    kdec = np.repeat(kdec, RET_QK_DIM, axis=1)
    cdec = np.repeat(np.exp(lg * c), RET_QK_DIM).reshape(RET_HEADS // 2, 2 * RET_QK_DIM, 1)
    return tuple(jnp.asarray(a, F32) for a in (dtab, qdec, kdec, cdec))


def _merge_kernel(x_ref, a_ref, rq_ref, rk_ref, rv_ref, rg_ref, mq_ref, mk_ref, mv_ref,
                  dtab_ref, qdec_ref, kdec_ref, cdec_ref, gn_ref,
                  wg_ref, bg_ref, wd_ref, wr_ref, wm_ref, wo_ref, g1_ref, b1_ref, o_ref,
                  state_scr, r_scr, m_scr, merged_scr, *, nc, tiles_per_seq):
    tm, d = x_ref.shape
    x = x_ref[...]
    xb = x.astype(BF16)

    @pl.when(pl.program_id(0) % tiles_per_seq == 0)
    def _():
        state_scr[...] = jnp.zeros(state_scr.shape, F32)

    for r0 in range(0, tm, RET_C):
        _retention_chunk(slice(r0, r0 + RET_C), rq_ref, rk_ref, rv_ref, rg_ref, dtab_ref, qdec_ref, kdec_ref,
                         cdec_ref, gn_ref, state_scr, r_scr)

    scale = jnp.asarray(MEM_HEAD_DIM ** -0.5, BF16)
    for h in range(MEM_HEADS):
        cols = slice(h * MEM_HEAD_DIM, (h + 1) * MEM_HEAD_DIM)
        qh = mq_ref[:, cols] * scale
        s = lax.dot_general(qh, mk_ref[0, :, cols], (((1,), (1,)), ((), ())),
                            preferred_element_type=F32)
        p = jnp.exp(s - jnp.max(s, axis=-1, keepdims=True))
        l = jnp.sum(p, axis=-1, keepdims=True)
        mh = jnp.dot(p.astype(BF16), mv_ref[0, :, cols], preferred_element_type=F32) / l
        m_scr[:, cols] = mh.astype(BF16)

    a = a_ref[...]
    r = r_scr[...]
    m = m_scr[...]
    for c in range(0, d, nc):
        cs = slice(c, c + nc)
        acc = None
        for br, (val, w_ref) in enumerate(((a, wd_ref), (r, wr_ref), (m, wm_ref))):
            gs = slice(br * d + c, br * d + c + nc)
            gate = jax.nn.sigmoid(jnp.dot(xb, wg_ref[:, gs], preferred_element_type=F32) + bg_ref[:, gs])
            term = gate * jnp.dot(val, w_ref[:, cs], preferred_element_type=F32)
            acc = term if acc is None else acc + term
        merged_scr[:, cs] = acc.astype(BF16)

    for r0 in range(0, tm, EPI_RB):
        rows = slice(r0, r0 + EPI_RB)
        mix = jnp.dot(merged_scr[rows, :], wo_ref[...], preferred_element_type=F32)
        o_ref[rows, :] = _layer_norm(DEEPNORM_ALPHA * x_ref[rows, :] + mix, g1_ref[...], b1_ref[...])


def _merge(x2, a, main, memkv, ret_g, wg, bg, wd, wr, wm, wo, g1, b1, batch, seq):
    tm = MERGE_TM
    t, d = x2.shape
    nt = seq // tm
    mlen = memkv.shape[1]
    mw = MEM_HEADS * MEM_HEAD_DIM
    qkw = RET_HEADS * RET_QK_DIM
    vw = RET_HEADS * RET_V_DIM
    dtab, qdec, kdec, cdec = _ret_tables(RET_C)
    row = lambda i: (i, 0)
    return pl.pallas_call(
        functools.partial(_merge_kernel, nc=MERGE_NC, tiles_per_seq=nt),
        grid=(t // tm,),
        in_specs=[
            pl.BlockSpec((tm, d), row),
            pl.BlockSpec((tm, a.shape[1]), row),
            pl.BlockSpec((tm, qkw), lambda i: (i, OFF_RQ // qkw)),
            pl.BlockSpec((tm, qkw), lambda i: (i, OFF_RK // qkw)),
            pl.BlockSpec((tm, vw), lambda i: (i, OFF_RV // vw)),
            pl.BlockSpec((tm, vw), lambda i: (i, OFF_RG // vw)),
            pl.BlockSpec((tm, mw), lambda i: (i, OFF_MQ // mw)),
            pl.BlockSpec((1, mlen, mw), lambda i: (i // nt, 0, 0)),
            pl.BlockSpec((1, mlen, mw), lambda i: (i // nt, 0, 1)),
            _resident(dtab.shape), _resident(qdec.shape), _resident(kdec.shape), _resident(cdec.shape),
            _resident(ret_g.shape),
            _resident(wg.shape), _resident(bg.shape), _resident(wd.shape), _resident(wr.shape),
            _resident(wm.shape), _resident(wo.shape), _resident(g1.shape), _resident(b1.shape),
        ],
        out_specs=pl.BlockSpec((tm, d), row),
        out_shape=jax.ShapeDtypeStruct((t, d), F32),
        scratch_shapes=[
            pltpu.VMEM((RET_HEADS // 2, 2 * RET_QK_DIM, 2 * RET_V_DIM), F32),
            pltpu.VMEM((tm, vw), BF16),
            pltpu.VMEM((tm, mw), BF16),
            pltpu.VMEM((tm, d), BF16),
        ],
        compiler_params=_cparams(("arbitrary",)),
        name="merge",
    )(x2, a, main, main, main, main, main, memkv, memkv, dtab, qdec, kdec, cdec, ret_g,
      wg, bg, wd, wr, wm, wo, g1, b1)


def _ffn_kernel(h_ref, wu_ref, cw_ref, cb_ref, wdn_ref, g2_ref, b2_ref, o_ref,
                carry_scr, hs_scr, act_scr, *, fc, rb, ffn_dim, tiles_per_seq):
    tm = h_ref.shape[0]
    halo = SUBLANES

    @pl.when(pl.program_id(0) % tiles_per_seq == 0)
    def _():
        carry_scr[...] = jnp.zeros(carry_scr.shape, F32)

    x = h_ref[...]
    xb = x.astype(BF16)

    def up_project(slot, col):
        cs = slice(col, col + fc)
        up = jnp.dot(xb, wu_ref[:, cs], preferred_element_type=F32)
        hs_scr[slot, 0:halo, :] = carry_scr[:, cs]
        hs_scr[slot, halo:halo + tm, :] = up
        carry_scr[:, cs] = up[tm - halo:tm, :]

    def conv_rows(slot, col, r0):
        cs = slice(col, col + fc)
        blk = hs_scr[slot, r0:r0 + rb + halo, :]
        w = cw_ref[:, cs]
        y = pltpu.roll(blk, 2, 0) * w[0:1] + pltpu.roll(blk, 1, 0) * w[1:2] + blk * w[2:3] + cb_ref[:, cs]
        return y[halo:, :]

    for j, c in enumerate(range(0, ffn_dim, fc)):
        sg, sv = 2 * (j % 2), 2 * (j % 2) + 1
        up_project(sg, c)
        up_project(sv, ffn_dim + c)
        for r0 in range(0, tm, rb):
            g = conv_rows(sg, c, r0)
            val = conv_rows(sv, ffn_dim + c, r0)
            act_scr[r0:r0 + rb, c:c + fc] = (g * jax.nn.sigmoid(g) * val).astype(BF16)

    for r0 in range(0, tm, EPI_RB):
        rows = slice(r0, r0 + EPI_RB)
        ffn = jnp.dot(act_scr[rows, :], wdn_ref[...], preferred_element_type=F32)
        o_ref[rows, :] = _layer_norm(DEEPNORM_ALPHA * h_ref[rows, :] + ffn, g2_ref[...], b2_ref[...])


def _ffn(h1, wu, cw, cb, wdn, g2, b2, seq):
    tm = FFN_TM
    t, d = h1.shape
    ffn_dim = wdn.shape[0]
    row = lambda i: (i, 0)
    return pl.pallas_call(
        functools.partial(_ffn_kernel, fc=FFN_FC, rb=FFN_RB, ffn_dim=ffn_dim, tiles_per_seq=seq // tm),
        grid=(t // tm,),
        in_specs=[
            pl.BlockSpec((tm, d), row),
            _resident(wu.shape), _resident(cw.shape), _resident(cb.shape), _resident(wdn.shape),
            _resident(g2.shape), _resident(b2.shape),
        ],
        out_specs=pl.BlockSpec((tm, d), row),
        out_shape=jax.ShapeDtypeStruct((t, d), F32),
        scratch_shapes=[
            pltpu.VMEM((SUBLANES, 2 * ffn_dim), F32),
            pltpu.VMEM((4, tm + SUBLANES, FFN_FC), F32),
            pltpu.VMEM((tm, ffn_dim), BF16),
        ],
        compiler_params=_cparams(("arbitrary",)),
        name="conv_ffn",
    )(h1, wu, cw, cb, wdn, g2, b2)


def kernel(x, mem, w_in, diff_lambda, diff_subln_g, ret_norm_g, w_mem_kv, w_diff_o, w_ret_o, w_mem_o,
           w_gate, b_gate, w_mix_out, ln1_g, ln1_b, w_up, conv_w, conv_b, w_down, ln2_g, ln2_b):
    batch, seq, d = x.shape
    mlen = mem.shape[1]
    assert w_in.shape[0] == DEPTH == 1 and conv_w.shape[1] == CONV_WIDTH == 3
    assert seq % PROJ_TM == 0 and MERGE_TM % RET_C == 0 and seq % FFN_TM == 0 and seq % MERGE_TM == 0
    assert PROJ_TM % ATT_TK == 0 and ATT_TQ % CHUNK == 0 and ATT_TK % ATT_TQ == 0 and seq % (ATT_TQ * ATT_NS) == 0
    assert ATT_UNROLL == 2 and (ATT_NS * ATT_TQ // ATT_TK) % ATT_UNROLL == 0
    l = 0
    row = lambda v: v[l].reshape(1, -1)

    x2 = x.reshape(batch * seq, d)
    main, kp, vt, nrm = _inproj(x2, w_in[l])
    memkv = _proj(mem.reshape(batch * mlen, d), w_mem_kv[l].astype(BF16), batch * mlen, PROJ_NC)
    memkv = memkv.reshape(batch, mlen, 2 * MEM_HEADS * MEM_HEAD_DIM)

    a = _diff_attention(main, kp, vt, nrm, diff_lambda[l], row(diff_subln_g), batch, seq)
    h1 = _merge(x2, a, main, memkv, row(ret_norm_g), w_gate[l].astype(BF16), row(b_gate),
                w_diff_o[l].astype(BF16), w_ret_o[l].astype(BF16), w_mem_o[l].astype(BF16),
                w_mix_out[l].astype(BF16), row(ln1_g), row(ln1_b), batch, seq)
    out = _ffn(h1, w_up[l].astype(BF16), conv_w[l], row(conv_b), w_down[l].astype(BF16),
               row(ln2_g), row(ln2_b), seq)
    return out.reshape(batch, seq, d)
```

```python
import functools
import math

import numpy as np
import jax
import jax.numpy as jnp
from jax import lax
from jax.experimental import pallas as pl
from jax.experimental.pallas import tpu as pltpu

F32 = jnp.float32
BF16 = jnp.bfloat16

CHUNK = 64
DIFF_HEADS = 4
DIFF_QK_DIM = 64
DIFF_V_DIM = 128
RET_HEADS = 4
RET_QK_DIM = 64
RET_V_DIM = 128
MEM_HEADS = 4
MEM_HEAD_DIM = 128
CONV_WIDTH = 3
LN_EPS = 1e-5
DEPTH = 1
DEEPNORM_ALPHA = (2.0 * DEPTH) ** 0.25
LAMBDA_INIT = 0.8 - 0.6 * math.exp(-0.3 * 0)
LOG2E = math.log2(math.e)

IN_DQ, IN_DK, IN_DV = 0, 512, 1024
IN_RQ, IN_END = 1536, 3584
OFF_DQ, OFF_RQ, OFF_RK, OFF_RV, OFF_RG, OFF_MQ, MAIN_W = 0, 512, 768, 1024, 1536, 2048, 2560

LANES = 128
SUBLANES = 8
BF16_ROWS = 16
VMEM_LIMIT = 56 * 1024 * 1024

PROJ_TM = 512
PROJ_NC = 512
ATT_TQ = 256
ATT_TK = 512
ATT_NS = 4
ATT_UNROLL = 2
ATT_RB = 256
ATT_UNDERFLOW_LOG2 = 134.0
ATT_BOUND_SCALE = 1.02
ATT_BOUND_SLACK = 0.05
VT_ROWS = DIFF_V_DIM + BF16_ROWS
RET_C = 256
MERGE_TM = 512
MERGE_NC = 512
FFN_TM = 512
FFN_FC = 256
FFN_RB = 128
EPI_RB = 256


def _cparams(sem):
    return pltpu.CompilerParams(dimension_semantics=sem, vmem_limit_bytes=VMEM_LIMIT)


def _resident(shape):
    nd = len(shape)
    return pl.BlockSpec(shape, lambda *_: (0,) * nd, pipeline_mode=pl.Buffered(1))


def _layer_norm(y, g, b):
    mu = jnp.mean(y, axis=-1, keepdims=True)
    d = y - mu
    var = jnp.mean(d * d, axis=-1, keepdims=True)
    return d * lax.rsqrt(var + LN_EPS) * g + b


def _alibi_slopes():
    return np.array([(2.0 ** (-8.0 / DIFF_HEADS)) ** (i + 1) for i in range(DIFF_HEADS)], np.float64)


def _bf16_split3(v):
    parts = []
    rem = np.float64(v)
    for _ in range(3):
        p = np.asarray(rem, np.float32).astype(jnp.bfloat16)
        parts.append(p)
        rem = rem - np.float64(p.astype(np.float32))
    return parts


def _proj_kernel(x_ref, w_ref, o_ref, *, nc):
    xb = x_ref[...].astype(BF16)
    n = w_ref.shape[1]
    for c in range(0, n, nc):
        o_ref[:, c:c + nc] = jnp.dot(xb, w_ref[:, c:c + nc],
                                     preferred_element_type=F32).astype(o_ref.dtype)


def _proj(x2, w, tm, nc):
    t, d = x2.shape
    n = w.shape[1]
    return pl.pallas_call(
        functools.partial(_proj_kernel, nc=nc),
        grid=(t // tm,),
        in_specs=[pl.BlockSpec((tm, d), lambda i: (i, 0)), _resident((d, n))],
        out_specs=pl.BlockSpec((tm, n), lambda i: (i, 0)),
        out_shape=jax.ShapeDtypeStruct((t, n), BF16),
        compiler_params=_cparams(("arbitrary",)),
        name="memkv",
    )(x2, w)


def _inproj_kernel(x_ref, wm_ref, wk_ref, wvt_ref, kaug_ref, sel_ref, main_ref, kp_ref, vt_ref, nrm_ref, *, nc):
    xb = x_ref[...].astype(BF16)
    qscale = LOG2E * DIFF_QK_DIM ** -0.5
    for c in range(0, MAIN_W, nc):
        acc = jnp.dot(xb, wm_ref[:, c:c + nc], preferred_element_type=F32)
        if c < OFF_RQ:
            acc = acc * qscale
        main_ref[:, c:c + nc] = acc.astype(main_ref.dtype)

    k = jnp.dot(xb, wk_ref[...], preferred_element_type=F32).astype(kp_ref.dtype)
    for h in range(DIFF_HEADS):
        kp_ref[:, h * 2 * LANES:h * 2 * LANES + LANES] = k[:, h * LANES:(h + 1) * LANES]
        kp_ref[:, h * 2 * LANES + LANES:(h + 1) * 2 * LANES] = kaug_ref[:, h * LANES:(h + 1) * LANES]

    def max_sq_norm(v):
        vf = v.astype(F32)
        sq = jnp.dot((vf * vf).astype(BF16), sel_ref[...], preferred_element_type=F32)
        return jnp.broadcast_to(jnp.max(sq, axis=0, keepdims=True), (SUBLANES // 2, LANES))

    nrm_ref[0] = jnp.concatenate([max_sq_norm(main_ref[:, OFF_DQ:OFF_RQ]), max_sq_norm(k)], axis=0)

    vt = lax.dot_general(wvt_ref[...], xb, (((1,), (1,)), ((), ())), preferred_element_type=F32)
    row = lax.broadcasted_iota(jnp.int32, vt.shape, 0)
    ones_row = row == DIFF_V_DIM
    for h in range(1, DIFF_HEADS):
        ones_row = ones_row | (row == h * VT_ROWS + DIFF_V_DIM)
    vt_ref[0] = jnp.where(ones_row, 1.0, vt).astype(vt_ref.dtype)


def _inproj(x2, w_in):
    t, d = x2.shape
    tm = PROJ_TM
    cols = lambda a, b: w_in[:, a:b]
    w_main = jnp.concatenate([cols(IN_DQ, IN_DK), cols(IN_RQ, IN_END)], axis=1).astype(BF16)
    w_k = cols(IN_DK, IN_DV).astype(BF16)
    wv = cols(IN_DV, IN_RQ).astype(BF16).T.reshape(DIFF_HEADS, DIFF_V_DIM, d)
    w_vt = jnp.pad(wv, ((0, 0), (0, VT_ROWS - DIFF_V_DIM), (0, 0))).reshape(DIFF_HEADS * VT_ROWS, d)

    kaug = _alibi_key_columns(tm, ATT_TK)
    sel = (np.arange(DIFF_HEADS * 2 * DIFF_QK_DIM)[:, None] // DIFF_QK_DIM == np.arange(LANES)[None, :])
    sel = jnp.asarray(sel, BF16)

    row = lambda i: (i, 0)
    return pl.pallas_call(
        functools.partial(_inproj_kernel, nc=PROJ_NC),
        grid=(t // tm,),
        in_specs=[pl.BlockSpec((tm, d), row), _resident(w_main.shape), _resident(w_k.shape),
                  _resident(w_vt.shape), _resident(kaug.shape), _resident(sel.shape)],
        out_specs=[pl.BlockSpec((tm, MAIN_W), row),
                   pl.BlockSpec((tm, DIFF_HEADS * 2 * LANES), row),
                   pl.BlockSpec((1, DIFF_HEADS * VT_ROWS, tm), lambda i: (i, 0, 0)),
                   pl.BlockSpec((1, SUBLANES, LANES), lambda i: (i, 0, 0))],
        out_shape=[jax.ShapeDtypeStruct((t, MAIN_W), BF16),
                   jax.ShapeDtypeStruct((t, DIFF_HEADS * 2 * LANES), BF16),
                   jax.ShapeDtypeStruct((t // tm, DIFF_HEADS * VT_ROWS, tm), BF16),
                   jax.ShapeDtypeStruct((t // tm, SUBLANES, LANES), F32)],
        compiler_params=_cparams(("arbitrary",)),
        name="inproj",
    )(x2, w_main, w_k, w_vt, kaug, sel)


def _attn_kernel(slope_ref, first_ref, q_ref, qaug_ref, kp_ref, vt_ref, tdiag_ref, lam_ref, g_ref, o_ref,
                 qs_scr, m_scr, acc_scr, z_scr, zmax_scr, p_scr, alpha_scr, *, tq, tk, ns):
    b, h, qi = pl.program_id(0), pl.program_id(1), pl.program_id(2)
    per_tile = tk // tq
    n0 = (ns // per_tile) * qi
    first = first_ref[b, h, qi]
    slope = slope_ref[h]

    @pl.when((b == 0) & (h == 0) & (qi == 0))
    def _():
        p_scr[...] = jnp.zeros(p_scr.shape, BF16)

    qa = qaug_ref[0]
    for j in range(ns):
        q = q_ref[j * tq:(j + 1) * tq, :]
        lane = lax.broadcasted_iota(jnp.int32, q.shape, 1)
        zero = jnp.zeros_like(q)
        q1 = jnp.concatenate([jnp.where(lane < DIFF_QK_DIM, q, zero), qa], axis=1).astype(F32)
        q2 = jnp.concatenate([jnp.where(lane >= DIFF_QK_DIM, q, zero), qa], axis=1).astype(F32)
        qs_scr[j, :, 0:tq] = q1.T.astype(BF16)
        qs_scr[j, :, tq:2 * tq] = q2.T.astype(BF16)
        m_scr[j] = jnp.full(m_scr.shape[1:], -jnp.inf, F32)
        acc_scr[j] = jnp.zeros(acc_scr.shape[1:], F32)
        alpha_scr[j] = jnp.ones(alpha_scr.shape[1:], F32)

    def score_stage(j, kb, slot, rows=tk):
        start = pl.multiple_of(kb * tk, tk)
        z = jnp.dot(kp_ref[pl.ds(start, rows), :], qs_scr[j],
                    preferred_element_type=F32)
        zmax = None
        for r in range(0, rows, ATT_RB):
            zb = z[r:r + ATT_RB]
            z_scr[j, slot, r:r + ATT_RB] = zb
            bmax = jnp.max(zb, axis=0, keepdims=True)
            zmax = bmax if zmax is None else jnp.maximum(zmax, bmax)
        zmax_scr[j, slot] = zmax

    def softmax_stage(j, z, zmax, c, slot, rows=tk):
        m_prev = m_scr[j]
        m_new = jnp.maximum(m_prev, zmax + c)
        shift = m_new - c
        for r in range(0, rows, ATT_RB):
            p_scr[j, slot, r:r + ATT_RB] = jnp.exp2(z[r:r + ATT_RB] - shift).astype(BF16)
        alpha_scr[j, slot] = jnp.exp2(m_prev - m_new)
        m_scr[j] = m_new

    def value_stage(j, kb, slot, rows=tk):
        v = vt_ref[jnp.maximum(kb, first), :, 0:rows]
        v = jnp.where(kb >= first, v, jnp.zeros_like(v))
        acc_scr[j] = alpha_scr[j, slot] * acc_scr[j] + jnp.dot(v, p_scr[j, slot, 0:rows],
                                                               preferred_element_type=F32)

    def tile_bias(j, kb):
        return -slope * ((n0 - kb) * tk + j * tq).astype(F32)

    def full_trip(kb, cur, nxt):
        for j in range(ns):
            score_stage(j, kb + 1, nxt)
            softmax_stage(j, z_scr.at[j, cur], zmax_scr[j, cur], tile_bias(j, kb), cur)
            value_stage(j, kb - 1, nxt)

    lead = (n0 - first) % ATT_UNROLL
    base = first + lead

    @pl.when(lead == 0)
    def _():
        for j in range(ns):
            score_stage(j, first, 0)

    @pl.when(lead == 1)
    def _():
        for j in range(ns):
            score_stage(j, first, 1)
        full_trip(first, 1, 0)

    def unrolled(i, carry):
        for u in range(ATT_UNROLL):
            full_trip(base + ATT_UNROLL * i + u, u % 2, 1 - u % 2)
        return carry

    lax.fori_loop(0, (n0 - base) // ATT_UNROLL, unrolled, 0)

    for i in range(ns // per_tile + 1):
        kb = n0 + i
        cur, nxt = i % 2, 1 - i % 2
        for j in range(ns):
            last = j // per_tile
            seen = (j % per_tile + 1) * tq
            if i < last:
                score_stage(j, kb + 1, nxt, rows=seen if i + 1 == last else tk)
                softmax_stage(j, z_scr.at[j, cur], zmax_scr[j, cur], tile_bias(j, kb), cur)
            elif i == last:
                td = tdiag_ref[0, j % per_tile, 0:seen, :]
                z = z_scr[j, cur, 0:seen]
                z = jnp.concatenate([z[:, 0:tq] + td, z[:, tq:2 * tq] + td], axis=1)
                softmax_stage(j, z, jnp.max(z, axis=0, keepdims=True), tile_bias(j, kb), cur, rows=seen)
            if i <= last + 1:
                value_stage(j, kb - 1, nxt, rows=seen if i == last + 1 else tk)

    lp = lam_ref[...]
    lam = (jnp.exp(jnp.sum(lp[0:1] * lp[1:2], axis=-1, keepdims=True))
           - jnp.exp(jnp.sum(lp[2:3] * lp[3:4], axis=-1, keepdims=True)) + LAMBDA_INIT)
    for j in range(ns):
        acc = acc_scr[j]
        o = acc[0:DIFF_V_DIM, :] / acc[DIFF_V_DIM:DIFF_V_DIM + 1, :]
        y = o[:, 0:tq] - lam * o[:, tq:2 * tq]
        y = y * lax.rsqrt(jnp.mean(y * y, axis=0, keepdims=True) + LN_EPS)
        o_ref[j * tq:(j + 1) * tq, :] = (y.T * g_ref[...] * (1.0 - LAMBDA_INIT)).astype(o_ref.dtype)


def _alibi_key_columns(n_rows, tk):
    l3 = _bf16_split3(LOG2E)
    off = np.arange(n_rows) % tk
    kaug = np.zeros((n_rows, DIFF_HEADS, LANES), np.float32)
    for hh, m in enumerate(_alibi_slopes()):
        for i in range(3):
            kaug[:, hh, i] = np.float32(l3[i])
            kaug[:, hh, 3 + i] = m * (off % ATT_TQ)
            kaug[:, hh, 6 + i] = m * (off - off % ATT_TQ)
    return jnp.asarray(kaug.reshape(n_rows, DIFF_HEADS * LANES), BF16)


def _attn_tables(tq, tk):
    slopes = _alibi_slopes()
    l3 = _bf16_split3(LOG2E)
    il = np.arange(tq, dtype=np.float64)
    qaug = np.zeros((DIFF_HEADS, tq, LANES), np.float32)
    for hh, m in enumerate(slopes):
        for i in range(3):
            qaug[hh, :, i] = -m * il
            qaug[hh, :, 3 + i] = np.float32(l3[i])
            qaug[hh, :, 6 + i] = np.float32(l3[i])
    j = np.arange(tk)[None, :, None]
    i = np.arange(tq)[None, None, :] + tq * np.arange(tk // tq)[:, None, None]
    allowed = (j // CHUNK) <= (i // CHUNK)
    corr = LOG2E * slopes[:, None, None, None] * ((i - j) - np.abs(i - j))[None]
    tdiag = np.where(allowed[None], corr, -np.inf)
    return (jnp.asarray(slopes * LOG2E, F32), jnp.asarray(qaug, BF16), jnp.asarray(tdiag, F32))


def _first_key_tiles(nrm, batch, seq):
    tiles_per_seq = seq // PROJ_TM
    tsup = ATT_TQ * ATT_NS
    n = nrm.reshape(batch, tiles_per_seq, 2, SUBLANES // 2, LANES)[:, :, :, 0, :DIFF_HEADS * 2]
    n = n.reshape(batch, tiles_per_seq, 2, DIFF_HEADS, 2).max(axis=-1)
    q2 = n[:, :, 0].reshape(batch, seq // tsup, tsup // PROJ_TM, DIFF_HEADS).max(axis=2)
    k2 = n[:, :, 1].max(axis=1)
    bound = jnp.sqrt(q2 * k2[:, None, :]) * ATT_BOUND_SCALE + ATT_BOUND_SLACK
    slope = jnp.asarray(_alibi_slopes() * LOG2E, F32)
    reach = jnp.minimum((2.0 * bound + ATT_UNDERFLOW_LOG2) / slope, float(seq))
    needed = jnp.floor(jnp.maximum(reach - 1.0, 0.0) / ATT_TK).astype(jnp.int32) + 1
    n0 = (tsup // ATT_TK) * jnp.arange(seq // tsup, dtype=jnp.int32)[None, :, None]
    first = jnp.maximum(n0 - needed, 0)
    return first.transpose(0, 2, 1)


def _diff_attention(main, kp, vt, nrm, lam_p, subln_g, batch, seq):
    tq, tk, ns = ATT_TQ, ATT_TK, ATT_NS
    tsup = tq * ns
    nq = seq // tsup
    slopes, qaug, tdiag = _attn_tables(tq, tk)
    first = _first_key_tiles(nrm, batch, seq)
    return pl.pallas_call(
        functools.partial(_attn_kernel, tq=tq, tk=tk, ns=ns),
        grid=(batch, DIFF_HEADS, nq),
        in_specs=[
            pl.BlockSpec(memory_space=pltpu.SMEM),
            pl.BlockSpec(memory_space=pltpu.SMEM),
            pl.BlockSpec((tsup, LANES), lambda b, h, qi: (b * nq + qi, OFF_DQ // LANES + h)),
            pl.BlockSpec((1, tq, LANES), lambda b, h, qi: (h, 0, 0)),
            pl.BlockSpec((seq, 2 * LANES), lambda b, h, qi: (b, h)),
            pl.BlockSpec((seq // tk, VT_ROWS, tk), lambda b, h, qi: (b, h, 0)),
            pl.BlockSpec((1,) + tdiag.shape[1:], lambda b, h, qi: (h, 0, 0, 0)),
            pl.BlockSpec(lam_p.shape, lambda b, h, qi: (0, 0)),
            pl.BlockSpec((1, LANES), lambda b, h, qi: (0, h)),
        ],
        out_specs=pl.BlockSpec((tsup, LANES), lambda b, h, qi: (b * nq + qi, h)),
        out_shape=jax.ShapeDtypeStruct((batch * seq, DIFF_HEADS * DIFF_V_DIM), BF16),
        scratch_shapes=[
            pltpu.VMEM((ns, 2 * LANES, 2 * tq), BF16),
            pltpu.VMEM((ns, 1, 2 * tq), F32),
            pltpu.VMEM((ns, VT_ROWS, 2 * tq), F32),
            pltpu.VMEM((ns, 2, tk, 2 * tq), F32),
            pltpu.VMEM((ns, 2, 1, 2 * tq), F32),
            pltpu.VMEM((ns, 2, tk, 2 * tq), BF16),
            pltpu.VMEM((ns, 2, 1, 2 * tq), F32),
        ],
        compiler_params=_cparams(("arbitrary", "arbitrary", "arbitrary")),
        name="diff_attn",
    )(slopes, first, main, qaug, kp, vt, tdiag, lam_p, subln_g)


def _retention_chunk(rows, q_ref, k_ref, v_ref, g_ref, dtab_ref, qdec_ref, kdec_ref, cdec_ref, gn_ref,
                     state_scr, r_scr):
    c = RET_C
    lane = lax.broadcasted_iota(jnp.int32, (c, LANES), 1)
    for p in range(RET_HEADS // 2):
        qp = q_ref[rows, p * LANES:(p + 1) * LANES]
        kp = k_ref[rows, p * LANES:(p + 1) * LANES]
        vp = v_ref[rows, p * 2 * RET_V_DIM:(p + 1) * 2 * RET_V_DIM]
        state = state_scr[p]
        state_b = state.astype(BF16)
        qdec = qdec_ref[:, p * LANES:(p + 1) * LANES]
        for e in range(2):
            head = 2 * p + e
            keep = (lane < RET_QK_DIM) if e == 0 else (lane >= RET_QK_DIM)
            qm = jnp.where(keep, qp, jnp.zeros_like(qp))
            s = lax.dot_general(qm, kp, (((1,), (1,)), ((), ())), preferred_element_type=F32)
            s = s * dtab_ref[head]
            ve = vp[:, e * RET_V_DIM:(e + 1) * RET_V_DIM]
            intra = jnp.dot(s.astype(BF16), ve, preferred_element_type=F32)
            qd = (qm.astype(F32) * qdec).astype(BF16)
            cross = jnp.dot(qd, state_b[:, e * RET_V_DIM:(e + 1) * RET_V_DIM],
                            preferred_element_type=F32)
            o = intra + cross
            mu = jnp.mean(o, axis=-1, keepdims=True)
            d = o - mu
            var = jnp.mean(d * d, axis=-1, keepdims=True)
            cols = slice(head * RET_V_DIM, (head + 1) * RET_V_DIM)
            gate = g_ref[rows, cols].astype(F32)
            y = d * lax.rsqrt(var + LN_EPS) * gn_ref[:, cols] * (gate * jax.nn.sigmoid(gate))
            r_scr[rows, cols] = y.astype(r_scr.dtype)
        kd = (kp.astype(F32) * kdec_ref[:, p * LANES:(p + 1) * LANES]).astype(BF16)
        kv = lax.dot_general(kd, vp, (((0,), (0,)), ((), ())), preferred_element_type=F32)
        state_scr[p] = state * cdec_ref[p] + kv


def _ret_tables(c):
    hh = np.arange(RET_HEADS, dtype=np.float64)
    lg = np.log(1.0 - 2.0 ** (-5.0 - hh))
    idx = np.arange(c, dtype=np.float64)
    nm = idx[:, None] - idx[None, :]
    dtab = np.where(nm >= 0, np.exp(lg[:, None, None] * np.maximum(nm, 0.0)), 0.0) * RET_QK_DIM ** -0.5
    qdec = np.exp(lg[None, :] * (idx + 1.0)[:, None])
    kdec = np.exp(lg[None, :] * (c - 1.0 - idx)[:, None]) * RET_QK_DIM ** -0.5
    qdec = np.repeat(qdec, RET_QK_DIM, axis=1)
    kdec = np.repeat(kdec, RET_QK_DIM, axis=1)
    cdec = np.repeat(np.exp(lg * c), RET_QK_DIM).reshape(RET_HEADS // 2, 2 * RET_QK_DIM, 1)
    return tuple(jnp.asarray(a, F32) for a in (dtab, qdec, kdec, cdec))


def _merge_kernel(x_ref, a_ref, rq_ref, rk_ref, rv_ref, rg_ref, mq_ref, mk_ref, mv_ref,
                  dtab_ref, qdec_ref, kdec_ref, cdec_ref, gn_ref,
                  wg_ref, bg_ref, wd_ref, wr_ref, wm_ref, wo_ref, g1_ref, b1_ref, o_ref,
                  state_scr, r_scr, m_scr, merged_scr, *, nc, tiles_per_seq):
    tm, d = x_ref.shape
    x = x_ref[...]
    xb = x.astype(BF16)

    @pl.when(pl.program_id(0) % tiles_per_seq == 0)
    def _():
        state_scr[...] = jnp.zeros(state_scr.shape, F32)

    for r0 in range(0, tm, RET_C):
        _retention_chunk(slice(r0, r0 + RET_C), rq_ref, rk_ref, rv_ref, rg_ref, dtab_ref, qdec_ref, kdec_ref,
                         cdec_ref, gn_ref, state_scr, r_scr)

    scale = jnp.asarray(MEM_HEAD_DIM ** -0.5, BF16)
    for h in range(MEM_HEADS):
        cols = slice(h * MEM_HEAD_DIM, (h + 1) * MEM_HEAD_DIM)
        qh = mq_ref[:, cols] * scale
        s = lax.dot_general(qh, mk_ref[0, :, cols], (((1,), (1,)), ((), ())),
                            preferred_element_type=F32)
        p = jnp.exp(s - jnp.max(s, axis=-1, keepdims=True))
        l = jnp.sum(p, axis=-1, keepdims=True)
        mh = jnp.dot(p.astype(BF16), mv_ref[0, :, cols], preferred_element_type=F32) / l
        m_scr[:, cols] = mh.astype(BF16)

    a = a_ref[...]
    r = r_scr[...]
    m = m_scr[...]
    for c in range(0, d, nc):
        cs = slice(c, c + nc)
        acc = None
        for br, (val, w_ref) in enumerate(((a, wd_ref), (r, wr_ref), (m, wm_ref))):
            gs = slice(br * d + c, br * d + c + nc)
            gate = jax.nn.sigmoid(jnp.dot(xb, wg_ref[:, gs], preferred_element_type=F32) + bg_ref[:, gs])
            term = gate * jnp.dot(val, w_ref[:, cs], preferred_element_type=F32)
            acc = term if acc is None else acc + term
        merged_scr[:, cs] = acc.astype(BF16)

    for r0 in range(0, tm, EPI_RB):
        rows = slice(r0, r0 + EPI_RB)
        mix = jnp.dot(merged_scr[rows, :], wo_ref[...], preferred_element_type=F32)
        o_ref[rows, :] = _layer_norm(DEEPNORM_ALPHA * x_ref[rows, :] + mix, g1_ref[...], b1_ref[...])


def _merge(x2, a, main, memkv, ret_g, wg, bg, wd, wr, wm, wo, g1, b1, batch, seq):
    tm = MERGE_TM
    t, d = x2.shape
    nt = seq // tm
    mlen = memkv.shape[1]
    mw = MEM_HEADS * MEM_HEAD_DIM
    qkw = RET_HEADS * RET_QK_DIM
    vw = RET_HEADS * RET_V_DIM
    dtab, qdec, kdec, cdec = _ret_tables(RET_C)
    row = lambda i: (i, 0)
    return pl.pallas_call(
        functools.partial(_merge_kernel, nc=MERGE_NC, tiles_per_seq=nt),
        grid=(t // tm,),
        in_specs=[
            pl.BlockSpec((tm, d), row),
            pl.BlockSpec((tm, a.shape[1]), row),
            pl.BlockSpec((tm, qkw), lambda i: (i, OFF_RQ // qkw)),
            pl.BlockSpec((tm, qkw), lambda i: (i, OFF_RK // qkw)),
            pl.BlockSpec((tm, vw), lambda i: (i, OFF_RV // vw)),
            pl.BlockSpec((tm, vw), lambda i: (i, OFF_RG // vw)),
            pl.BlockSpec((tm, mw), lambda i: (i, OFF_MQ // mw)),
            pl.BlockSpec((1, mlen, mw), lambda i: (i // nt, 0, 0)),
            pl.BlockSpec((1, mlen, mw), lambda i: (i // nt, 0, 1)),
            _resident(dtab.shape), _resident(qdec.shape), _resident(kdec.shape), _resident(cdec.shape),
            _resident(ret_g.shape),
            _resident(wg.shape), _resident(bg.shape), _resident(wd.shape), _resident(wr.shape),
            _resident(wm.shape), _resident(wo.shape), _resident(g1.shape), _resident(b1.shape),
        ],
        out_specs=pl.BlockSpec((tm, d), row),
        out_shape=jax.ShapeDtypeStruct((t, d), F32),
        scratch_shapes=[
            pltpu.VMEM((RET_HEADS // 2, 2 * RET_QK_DIM, 2 * RET_V_DIM), F32),
            pltpu.VMEM((tm, vw), BF16),
            pltpu.VMEM((tm, mw), BF16),
            pltpu.VMEM((tm, d), BF16),
        ],
        compiler_params=_cparams(("arbitrary",)),
        name="merge",
    )(x2, a, main, main, main, main, main, memkv, memkv, dtab, qdec, kdec, cdec, ret_g,
      wg, bg, wd, wr, wm, wo, g1, b1)


def _ffn_kernel(h_ref, wu_ref, cw_ref, cb_ref, wdn_ref, g2_ref, b2_ref, o_ref,
                carry_scr, hs_scr, act_scr, *, fc, rb, ffn_dim, tiles_per_seq):
    tm = h_ref.shape[0]
    halo = SUBLANES

    @pl.when(pl.program_id(0) % tiles_per_seq == 0)
    def _():
        carry_scr[...] = jnp.zeros(carry_scr.shape, F32)

    x = h_ref[...]
    xb = x.astype(BF16)

    def up_project(slot, col):
        cs = slice(col, col + fc)
        up = jnp.dot(xb, wu_ref[:, cs], preferred_element_type=F32)
        hs_scr[slot, 0:halo, :] = carry_scr[:, cs]
        hs_scr[slot, halo:halo + tm, :] = up
        carry_scr[:, cs] = up[tm - halo:tm, :]

    def conv_rows(slot, col, r0):
        cs = slice(col, col + fc)
        blk = hs_scr[slot, r0:r0 + rb + halo, :]
        w = cw_ref[:, cs]
        y = pltpu.roll(blk, 2, 0) * w[0:1] + pltpu.roll(blk, 1, 0) * w[1:2] + blk * w[2:3] + cb_ref[:, cs]
        return y[halo:, :]

    for j, c in enumerate(range(0, ffn_dim, fc)):
        sg, sv = 2 * (j % 2), 2 * (j % 2) + 1
        up_project(sg, c)
        up_project(sv, ffn_dim + c)
        for r0 in range(0, tm, rb):
            g = conv_rows(sg, c, r0)
            val = conv_rows(sv, ffn_dim + c, r0)
            act_scr[r0:r0 + rb, c:c + fc] = (g * jax.nn.sigmoid(g) * val).astype(BF16)

    for r0 in range(0, tm, EPI_RB):
        rows = slice(r0, r0 + EPI_RB)
        ffn = jnp.dot(act_scr[rows, :], wdn_ref[...], preferred_element_type=F32)
        o_ref[rows, :] = _layer_norm(DEEPNORM_ALPHA * h_ref[rows, :] + ffn, g2_ref[...], b2_ref[...])


def _ffn(h1, wu, cw, cb, wdn, g2, b2, seq):
    tm = FFN_TM
    t, d = h1.shape
    ffn_dim = wdn.shape[0]
    row = lambda i: (i, 0)
    return pl.pallas_call(
        functools.partial(_ffn_kernel, fc=FFN_FC, rb=FFN_RB, ffn_dim=ffn_dim, tiles_per_seq=seq // tm),
        grid=(t // tm,),
        in_specs=[
            pl.BlockSpec((tm, d), row),
            _resident(wu.shape), _resident(cw.shape), _resident(cb.shape), _resident(wdn.shape),
            _resident(g2.shape), _resident(b2.shape),
        ],
        out_specs=pl.BlockSpec((tm, d), row),
        out_shape=jax.ShapeDtypeStruct((t, d), F32),
        scratch_shapes=[
            pltpu.VMEM((SUBLANES, 2 * ffn_dim), F32),
            pltpu.VMEM((4, tm + SUBLANES, FFN_FC), F32),
            pltpu.VMEM((tm, ffn_dim), BF16),
        ],
        compiler_params=_cparams(("arbitrary",)),
        name="conv_ffn",
    )(h1, wu, cw, cb, wdn, g2, b2)


def kernel(x, mem, w_in, diff_lambda, diff_subln_g, ret_norm_g, w_mem_kv, w_diff_o, w_ret_o, w_mem_o,
           w_gate, b_gate, w_mix_out, ln1_g, ln1_b, w_up, conv_w, conv_b, w_down, ln2_g, ln2_b):
    batch, seq, d = x.shape
    mlen = mem.shape[1]
    assert w_in.shape[0] == DEPTH == 1 and conv_w.shape[1] == CONV_WIDTH == 3
    assert seq % PROJ_TM == 0 and MERGE_TM % RET_C == 0 and seq % FFN_TM == 0 and seq % MERGE_TM == 0
    assert PROJ_TM == ATT_TK and ATT_TQ % CHUNK == 0 and ATT_TK % ATT_TQ == 0 and seq % (ATT_TQ * ATT_NS) == 0
    assert ATT_UNROLL == 2 and (ATT_NS * ATT_TQ // ATT_TK) % ATT_UNROLL == 0
    l = 0
    row = lambda v: v[l].reshape(1, -1)

    x2 = x.reshape(batch * seq, d)
    main, kp, vt, nrm = _inproj(x2, w_in[l])
    memkv = _proj(mem.reshape(batch * mlen, d), w_mem_kv[l].astype(BF16), batch * mlen, PROJ_NC)
    memkv = memkv.reshape(batch, mlen, 2 * MEM_HEADS * MEM_HEAD_DIM)

    a = _diff_attention(main, kp, vt, nrm, diff_lambda[l], row(diff_subln_g), batch, seq)
    h1 = _merge(x2, a, main, memkv, row(ret_norm_g), w_gate[l].astype(BF16), row(b_gate),
                w_diff_o[l].astype(BF16), w_ret_o[l].astype(BF16), w_mem_o[l].astype(BF16),
                w_mix_out[l].astype(BF16), row(ln1_g), row(ln1_b), batch, seq)
    out = _ffn(h1, w_up[l].astype(BF16), conv_w[l], row(conv_b), w_down[l].astype(BF16),
               row(ln2_g), row(ln2_b), seq)
    return out.reshape(batch, seq, d)
```
